```python
import math
import jax, jax.numpy as jnp
from jax import lax
import numpy as np

D_MODEL = 1024
BATCH = 4
SEQ = 4096
DEPTH = 4
DEC_BATCH = 128
DEC_SEQ = 4
PAST_LEN = 2048
PAGE_SIZE = 128

N_A_LAYERS = DEPTH // 2
N_B_LAYERS = DEPTH - N_A_LAYERS
NORM_EPS = 1e-6
DN_HEADS = 8
DN_HEAD_K = D_MODEL // DN_HEADS
DN_HEAD_V = D_MODEL // DN_HEADS
DN_KEY_DIM = DN_HEADS * DN_HEAD_K
DN_VAL_DIM = DN_HEADS * DN_HEAD_V
DN_QKV_DIM = 2 * DN_KEY_DIM + DN_VAL_DIM
DN_IN_DIM = DN_QKV_DIM + DN_VAL_DIM + 2 * DN_HEADS
DN_CONV = 4
DN_CHUNK = 64
NSA_HEADS = 16
NSA_HEAD_DIM = D_MODEL // NSA_HEADS
NSA_DIM = NSA_HEADS * NSA_HEAD_DIM
NSA_KV_HEADS = 4
NSA_GROUP = NSA_HEADS // NSA_KV_HEADS
NSA_IN_DIM = NSA_DIM + 3 * NSA_HEADS
NSA_KV_DIM = 6 * NSA_KV_HEADS * NSA_HEAD_DIM
CMP_STRIDE = 16
CMP_BLOCK = 2 * CMP_STRIDE
SEL_BLOCK = 64
N_SEL = 16
WINDOW = 512
Q_BLOCK = 128
REL_BUCKETS = 32
REL_MAX_EXACT = REL_BUCKETS // 2
REL_MAX_DIST = 1024
D_FF = ((8 * D_MODEL + 3 * 256 - 1) // (3 * 256)) * 256

kernel_name = 'yoco_gdn_nsa_decoder_step'


def rmsnorm(x, w):
    xf = x.astype(jnp.float32)
    y = xf * lax.rsqrt(jnp.mean(xf * xf, axis=-1, keepdims=True) + NORM_EPS)
    return y.astype(x.dtype) * w


def l2norm(x):
    xf = x.astype(jnp.float32)
    return xf * lax.rsqrt(jnp.sum(xf * xf, axis=-1, keepdims=True) + NORM_EPS)


def swiglu_ffn(h, w_in, w_out):
    gate, up = jnp.split(h @ w_in, 2, axis=-1)
    return (jax.nn.silu(gate) * up) @ w_out


def causal_short_conv(x, buf, w):
    t = x.shape[1]
    xp = jnp.concatenate([buf.astype(x.dtype), x], axis=1)
    y = xp[:, 0:t] * w[0]
    for i in range(1, DN_CONV):
        y = y + xp[:, i:i + t] * w[i]
    return jax.nn.silu(y), xp[:, xp.shape[1] - (DN_CONV - 1):]


def gated_delta_rule(q, k, v, g, beta, S0):
    B_, T, H, DK = q.shape
    DV = v.shape[-1]
    C = math.gcd(T, DN_CHUNK)
    N = T // C

    def to_chunks(a):
        return jnp.swapaxes(a.reshape((B_, N, C) + a.shape[2:]), 2, 3)

    qc, kc, vc, gc, bc = to_chunks(q), to_chunks(k), to_chunks(v), to_chunks(g), to_chunks(beta)
    G = jnp.cumsum(gc, axis=-1)
    ii = jnp.arange(C)
    strict = ii[:, None] > ii[None, :]
    incl = ii[:, None] >= ii[None, :]
    diff = G[..., :, None] - G[..., None, :]
    dec = jnp.where(incl, jnp.exp(jnp.where(incl, diff, 0.0)), 0.0)
    kk = jnp.einsum('bnhid,bnhjd->bnhij', kc, kc)
    A = jnp.where(strict, bc[..., :, None] * dec * kk, 0.0)
    L = A + jnp.eye(C, dtype=A.dtype)
    eG = jnp.exp(G)
    wv = lax.linalg.triangular_solve(L, bc[..., None] * vc, left_side=True, lower=True, unit_diagonal=True)
    wk = lax.linalg.triangular_solve(L, (bc * eG)[..., None] * kc, left_side=True, lower=True, unit_diagonal=True)
    aqk = dec * jnp.einsum('bnhid,bnhjd->bnhij', qc, kc)
    qg = eG[..., None] * qc
    kdec = jnp.exp(G[..., -1:] - G)[..., None] * kc
    glast = jnp.exp(G[..., -1])
    xs = tuple(jnp.swapaxes(a, 0, 1) for a in (wv, wk, aqk, qg, kdec, glast))

    def step(S, xc):
        wv_n, wk_n, aqk_n, qg_n, kdec_n, gl_n = xc
        U = wv_n - jnp.einsum('bhik,bhkv->bhiv', wk_n, S)
        O = jnp.einsum('bhik,bhkv->bhiv', qg_n, S) + jnp.einsum('bhij,bhjv->bhiv', aqk_n, U)
        S = gl_n[..., None, None] * S + jnp.einsum('bhik,bhiv->bhkv', kdec_n, U)
        return S, O

    S_fin, O = lax.scan(step, S0, xs)
    O = O.transpose(1, 0, 3, 2, 4).reshape(B_, T, H, DV)
    return O, S_fin


def deltanet_mixer(h, conv_buf, S0, w_in, conv_w, A_log, dt_bias, out_norm, w_out):
    f32 = jnp.float32
    B_, T, _ = h.shape
    qkv, z, a, b = jnp.split(h @ w_in, [DN_QKV_DIM, DN_QKV_DIM + DN_VAL_DIM, DN_QKV_DIM + DN_VAL_DIM + DN_HEADS], axis=-1)
    qkv, new_buf = causal_short_conv(qkv, conv_buf, conv_w)
    q, k, v = jnp.split(qkv, [DN_KEY_DIM, 2 * DN_KEY_DIM], axis=-1)
    q = l2norm(q.reshape(B_, T, DN_HEADS, DN_HEAD_K)) * (DN_HEAD_K ** -0.5)
    k = l2norm(k.reshape(B_, T, DN_HEADS, DN_HEAD_K))
    v = v.reshape(B_, T, DN_HEADS, DN_HEAD_V).astype(f32)
    beta = jax.nn.sigmoid(b.astype(f32))
    g = -jnp.exp(A_log.astype(f32)) * jax.nn.softplus(a.astype(f32) + dt_bias.astype(f32))
    o, S = gated_delta_rule(q, k, v, g, beta, S0.astype(f32))
    o = rmsnorm(o, out_norm) * jax.nn.silu(z.reshape(B_, T, DN_HEADS, DN_HEAD_V).astype(f32))
    return o.reshape(B_, T, DN_VAL_DIM).astype(h.dtype) @ w_out, new_buf, S


def shared_rows(x, norm_w, w_kv):
    B_, T, _ = x.shape
    return (rmsnorm(x, norm_w) @ w_kv).reshape(B_, T, 6, NSA_KV_HEADS, NSA_HEAD_DIM)


def nsa_context(hist, cmp_pos_w, w_cmp):
    B_, L = hist.shape[:2]
    L_pad = ((L + SEL_BLOCK - 1) // SEL_BLOCK) * SEL_BLOCK
    hist = jnp.pad(hist, ((0, 0), (0, L_pad - L), (0, 0), (0, 0), (0, 0)))
    n_sub = L_pad // CMP_STRIDE
    sub = hist[:, :, :2].reshape(B_, n_sub, CMP_STRIDE, 2, NSA_KV_HEADS, NSA_HEAD_DIM)
    lo = jnp.einsum('bmrckd,crkd->bmckd', sub, cmp_pos_w[:, :CMP_STRIDE])
    hi = jnp.einsum('bmrckd,crkd->bmckd', sub, cmp_pos_w[:, CMP_STRIDE:])
    blocks = lo[:, :-1] + hi[:, 1:]
    cmp = jnp.einsum('bnckd,ckde->bncke', blocks, w_cmp)
    cmp_end = jnp.arange(n_sub - 1, dtype=jnp.int32) * CMP_STRIDE + (CMP_BLOCK - 1)
    n_sel_blocks = L_pad // SEL_BLOCK
    sel = hist[:, :, 2:].reshape(B_, n_sel_blocks, SEL_BLOCK, 2, NSA_KV_HEADS, NSA_HEAD_DIM).transpose(3, 0, 4, 1, 2, 5)
    return (cmp[:, :, 0], cmp[:, :, 1], cmp_end, sel[0], sel[1])


def rel_bucket(dist):
    n = jnp.maximum(dist, 0)
    nf = jnp.maximum(n, 1).astype(jnp.float32)
    large = REL_MAX_EXACT + (jnp.log(nf / REL_MAX_EXACT) / math.log(REL_MAX_DIST / REL_MAX_EXACT) * (REL_BUCKETS - REL_MAX_EXACT)).astype(jnp.int32)
    large = jnp.minimum(large, REL_BUCKETS - 1)
    return jnp.where(n < REL_MAX_EXACT, n, large)


def masked_softmax(logits, mask):
    l = jnp.where(mask, logits.astype(jnp.float32), -1e30)
    m = jnp.max(l, axis=-1, keepdims=True)
    e = jnp.where(mask, jnp.exp(l - m), 0.0)
    return e / jnp.maximum(jnp.sum(e, axis=-1, keepdims=True), 1e-30)


def nsa_attend(q, gates, q_pos, k_cmp, v_cmp, cmp_end, k_sel, v_sel, k_win, v_win, w_pos, rel_bias):
    f32 = jnp.float32
    B_, Tq = q.shape[:2]
    table = rel_bias.astype(f32)
    qg = q.reshape(B_, Tq, NSA_KV_HEADS, NSA_GROUP, NSA_HEAD_DIM)
    d_c = q_pos[:, None] - cmp_end[None, :]
    bias_c = table[rel_bucket(d_c)].reshape(Tq, -1, NSA_KV_HEADS, NSA_GROUP).transpose(2, 3, 0, 1)
    p_c = masked_softmax(jnp.einsum('bqkgd,bckd->bkgqc', qg, k_cmp).astype(f32) + bias_c, d_c >= 0)
    o_cmp = jnp.einsum('bkgqc,bckd->bqkgd', p_c.astype(v_cmp.dtype), v_cmp)
    n_blocks = k_sel.shape[2]
    n_pick = min(N_SEL, n_blocks)
    imp = jnp.pad(jnp.sum(p_c, axis=2), ((0, 0), (0, 0), (0, 0), (0, 1)))
    cover = imp + jnp.pad(imp[..., :-1], ((0, 0), (0, 0), (0, 0), (1, 0)))
    p_slc = cover.reshape(B_, NSA_KV_HEADS, Tq, n_blocks, SEL_BLOCK // CMP_STRIDE).sum(-1)
    blk = jnp.arange(n_blocks, dtype=jnp.int32)[None, :]
    cur = (q_pos // SEL_BLOCK)[:, None]
    forced = (blk == 0) | (blk == cur) | (blk == cur - 1)
    valid = blk * SEL_BLOCK <= q_pos[:, None]
    score = jnp.where(forced, 1e4, jnp.where(valid, p_slc, -1.0))
    _, idx = lax.top_k(score, n_pick)
    b_ix = jnp.arange(B_)[:, None, None, None]
    h_ix = jnp.arange(NSA_KV_HEADS)[None, :, None, None]
    k_g = k_sel[b_ix, h_ix, idx]
    v_g = v_sel[b_ix, h_ix, idx]
    key_pos = idx[..., None] * SEL_BLOCK + jnp.arange(SEL_BLOCK, dtype=jnp.int32)
    d_s = q_pos[None, None, :, None, None] - key_pos
    bias_s = table.reshape(REL_BUCKETS, NSA_KV_HEADS, NSA_GROUP)[rel_bucket(d_s), h_ix[..., None]]
    bias_s = jnp.moveaxis(bias_s, -1, 3)
    l_s = jnp.einsum('bqkgd,bkqsrd->bkqgsr', qg, k_g).astype(f32) + bias_s
    n_keys = n_pick * SEL_BLOCK
    p_s = masked_softmax(l_s.reshape(B_, NSA_KV_HEADS, Tq, NSA_GROUP, n_keys),
                         (d_s >= 0).reshape(B_, NSA_KV_HEADS, Tq, 1, n_keys)).reshape(l_s.shape)
    o_sel = jnp.einsum('bkqgsr,bkqsrd->bqkgd', p_s.astype(v_g.dtype), v_g)
    d_w = q_pos[:, None] - w_pos[None, :]
    mask_w = (d_w >= 0) & (d_w < WINDOW) & (w_pos[None, :] >= 0)
    bias_w = table[rel_bucket(d_w)].reshape(Tq, -1, NSA_KV_HEADS, NSA_GROUP).transpose(2, 3, 0, 1)
    p_w = masked_softmax(jnp.einsum('bqkgd,bwkd->bkgqw', qg, k_win).astype(f32) + bias_w, mask_w)
    o_win = jnp.einsum('bkgqw,bwkd->bqkgd', p_w.astype(v_win.dtype), v_win)
    o = jnp.stack([o_cmp, o_sel, o_win], axis=-1).reshape(B_, Tq, NSA_HEADS, NSA_HEAD_DIM, 3)
    return jnp.einsum('bqhdr,bqhr->bqhd', o.astype(f32), gates)


def nsa_mixer(h, q_pos, ctx, kw, vw, w_pos, sweep, w_in, w_out, rel_bias):
    B_, T, _ = h.shape
    proj = h @ w_in
    q = proj[..., :NSA_DIM].reshape(B_, T, NSA_HEADS, NSA_HEAD_DIM) * (NSA_HEAD_DIM ** -0.5)
    gates = jax.nn.sigmoid(proj[..., NSA_DIM:].astype(jnp.float32)).reshape(B_, T, NSA_HEADS, 3)
    if sweep:
        qb = min(Q_BLOCK, T)
        nb = T // qb
        band = WINDOW + qb
        q_b = jnp.swapaxes(q.reshape(B_, nb, qb, NSA_HEADS, NSA_HEAD_DIM), 0, 1)
        g_b = jnp.swapaxes(gates.reshape(B_, nb, qb, NSA_HEADS, 3), 0, 1)
        starts = jnp.arange(nb, dtype=jnp.int32) * qb

        def one_block(args):
            qi, gi, s = args
            pos = s + jnp.arange(qb, dtype=jnp.int32)
            kwi = lax.dynamic_slice_in_dim(kw, s, band, axis=1)
            vwi = lax.dynamic_slice_in_dim(vw, s, band, axis=1)
            wpi = s - WINDOW + jnp.arange(band, dtype=jnp.int32)
            return nsa_attend(qi, gi, pos, ctx[0], ctx[1], ctx[2], ctx[3], ctx[4], kwi, vwi, wpi, rel_bias)

        o = jnp.swapaxes(lax.map(one_block, (q_b, g_b, starts)), 0, 1)
    else:
        o = nsa_attend(q, gates, q_pos, ctx[0], ctx[1], ctx[2], ctx[3], ctx[4], kw, vw, w_pos, rel_bias)
    return o.reshape(B_, T, NSA_DIM).astype(h.dtype) @ w_out


def run_trunk(x, dn_S0, dn_conv0, past_rows, win_prev, pos0, P):
    B_, T, _ = x.shape
    prompt = past_rows is None
    q_pos = pos0 + jnp.arange(T, dtype=jnp.int32)
    new_S, new_conv = [], []
    ctx = kw = vw = w_pos = kv_rows = new_win = None
    for l in range(DEPTH):
        h = rmsnorm(x, P['norm_mix'][l])
        if l < N_A_LAYERS:
            y, cb, S = deltanet_mixer(h, dn_conv0[l], dn_S0[l], P['dn_w_in'][l], P['dn_conv_w'][l], P['dn_A_log'][l],
                                      P['dn_dt_bias'][l], P['dn_out_norm'][l], P['dn_w_out'][l])
            new_S.append(S)
            new_conv.append(cb)
        else:
            j = l - N_A_LAYERS
            y = nsa_mixer(h, q_pos, ctx, kw, vw, w_pos, prompt, P['nsa_w_in'][j], P['nsa_w_out'][j], P['rel_bias'])
        x = x + y
        x = x + swiglu_ffn(rmsnorm(x, P['norm_ffn'][l]), P['ffn_w_in'][l], P['ffn_w_out'][l])
        if l == N_A_LAYERS - 1:
            rows = shared_rows(x, P['norm_kv'], P['nsa_w_kv'])
            kv_rows = rows[:, :, :4]
            if prompt:
                hist = kv_rows
                win_rows = rows[:, :, 4:]
                pad = ((0, 0), (WINDOW, 0), (0, 0), (0, 0))
                kw = jnp.pad(win_rows[:, :, 0], pad)
                vw = jnp.pad(win_rows[:, :, 1], pad)
                new_win = win_rows[:, T - min(WINDOW, T):]
            else:
                hist = jnp.concatenate([past_rows.astype(kv_rows.dtype), kv_rows], axis=1)
                buf = win_prev.shape[1]
                wb = jnp.concatenate([win_prev.astype(rows.dtype), rows[:, :, 4:]], axis=1)
                kw, vw = wb[:, :, 0], wb[:, :, 1]
                w_pos = pos0 - buf + jnp.arange(buf + T, dtype=jnp.int32)
                new_win = wb[:, T:]
            ctx = nsa_context(hist, P['nsa_cmp_pos_w'], P['nsa_w_cmp'])
    y = rmsnorm(x, P['norm_final'])
    return y, jnp.stack(new_S), jnp.stack(new_conv), kv_rows, new_win


def setup_inputs(seed: int = 0) -> dict:
    key = jax.random.key(seed)
    ks = jax.random.split(key, 26)
    f32 = jnp.float32

    def normal(k, shape, scale):
        return jax.random.normal(k, shape, f32) * scale

    n_pages = PAST_LEN // PAGE_SIZE
    n_used = DEC_BATCH * n_pages
    n_pool = n_used + (n_used + 3) // 4
    win_buf = min(WINDOW, PAST_LEN)
    page_table = jax.random.permutation(ks[6], n_pool)[:n_used].reshape(DEC_BATCH, n_pages).astype(jnp.int32)
    dt = jnp.exp(jax.random.uniform(ks[15], (N_A_LAYERS, DN_HEADS), f32, math.log(1e-3), math.log(1e-1)))
    return {
        'x_prompt': normal(ks[0], (BATCH, SEQ, D_MODEL), 1.0),
        'x_sample': normal(ks[1], (DEC_BATCH, DEC_SEQ, D_MODEL), 1.0),
        'state_dn_S': normal(ks[2], (N_A_LAYERS, DEC_BATCH, DN_HEADS, DN_HEAD_K, DN_HEAD_V), 0.1),
        'state_dn_conv': normal(ks[3], (N_A_LAYERS, DEC_BATCH, DN_CONV - 1, DN_QKV_DIM), 1.0),
        'cache_kv': normal(ks[4], (n_pool, PAGE_SIZE, 4, NSA_KV_HEADS, NSA_HEAD_DIM), 1.0),
        'state_win_kv': normal(ks[5], (DEC_BATCH, win_buf, 2, NSA_KV_HEADS, NSA_HEAD_DIM), 1.0),
        'page_table': page_table,
        'norm_mix': 1.0 + normal(ks[7], (DEPTH, D_MODEL), 0.02),
        'norm_ffn': 1.0 + normal(ks[8], (DEPTH, D_MODEL), 0.02),
        'norm_kv': 1.0 + normal(ks[9], (D_MODEL,), 0.02),
        'norm_final': 1.0 + normal(ks[10], (D_MODEL,), 0.02),
        'ffn_w_in': normal(ks[11], (DEPTH, D_MODEL, 2 * D_FF), D_MODEL ** -0.5),
        'ffn_w_out': normal(ks[12], (DEPTH, D_FF, D_MODEL), D_FF ** -0.5),
        'dn_w_in': normal(ks[13], (N_A_LAYERS, D_MODEL, DN_IN_DIM), D_MODEL ** -0.5),
        'dn_conv_w': normal(ks[14], (N_A_LAYERS, DN_CONV, DN_QKV_DIM), DN_CONV ** -0.5),
        'dn_A_log': jnp.log(jax.random.uniform(ks[16], (N_A_LAYERS, DN_HEADS), f32, 1.0, 16.0)),
        'dn_dt_bias': dt + jnp.log(-jnp.expm1(-dt)),
        'dn_out_norm': 1.0 + normal(ks[17], (N_A_LAYERS, DN_HEAD_V), 0.02),
        'dn_w_out': normal(ks[18], (N_A_LAYERS, DN_VAL_DIM, D_MODEL), DN_VAL_DIM ** -0.5),
        'nsa_w_kv': normal(ks[19], (D_MODEL, NSA_KV_DIM), D_MODEL ** -0.5),
        'nsa_cmp_pos_w': (1.0 + normal(ks[20], (2, CMP_BLOCK, NSA_KV_HEADS, NSA_HEAD_DIM), 0.1)) * CMP_BLOCK ** -0.5,
        'nsa_w_cmp': normal(ks[21], (2, NSA_KV_HEADS, NSA_HEAD_DIM, NSA_HEAD_DIM), NSA_HEAD_DIM ** -0.5),
        'nsa_w_in': normal(ks[22], (N_B_LAYERS, D_MODEL, NSA_IN_DIM), D_MODEL ** -0.5),
        'nsa_w_out': normal(ks[23], (N_B_LAYERS, NSA_DIM, D_MODEL), NSA_DIM ** -0.5),
        'rel_bias': normal(ks[24], (REL_BUCKETS, NSA_HEADS), 0.5),
    }


def reference(x_prompt, x_sample, state_dn_S, state_dn_conv, cache_kv, state_win_kv, page_table,
              norm_mix, norm_ffn, norm_kv, norm_final, ffn_w_in, ffn_w_out,
              dn_w_in, dn_conv_w, dn_A_log, dn_dt_bias, dn_out_norm, dn_w_out,
              nsa_w_kv, nsa_cmp_pos_w, nsa_w_cmp, nsa_w_in, nsa_w_out, rel_bias):
    P = {'norm_mix': norm_mix, 'norm_ffn': norm_ffn, 'norm_kv': norm_kv, 'norm_final': norm_final,
         'ffn_w_in': ffn_w_in, 'ffn_w_out': ffn_w_out, 'dn_w_in': dn_w_in, 'dn_conv_w': dn_conv_w,
         'dn_A_log': dn_A_log, 'dn_dt_bias': dn_dt_bias, 'dn_out_norm': dn_out_norm, 'dn_w_out': dn_w_out,
         'nsa_w_kv': nsa_w_kv, 'nsa_cmp_pos_w': nsa_cmp_pos_w, 'nsa_w_cmp': nsa_w_cmp,
         'nsa_w_in': nsa_w_in, 'nsa_w_out': nsa_w_out, 'rel_bias': rel_bias}
    bp = x_prompt.shape[0]
    p_S0 = jnp.zeros((N_A_LAYERS, bp, DN_HEADS, DN_HEAD_K, DN_HEAD_V), jnp.float32)
    p_conv0 = jnp.zeros((N_A_LAYERS, bp, DN_CONV - 1, DN_QKV_DIM), x_prompt.dtype)
    y_prompt, p_dn_S, p_dn_conv, p_kv_rows, p_win_kv = run_trunk(x_prompt, p_S0, p_conv0, None, None, 0, P)
    past = cache_kv[page_table]
    past = past.reshape(page_table.shape[0], page_table.shape[1] * PAGE_SIZE, 4, NSA_KV_HEADS, NSA_HEAD_DIM)
    y_sample, s_dn_S, s_dn_conv, s_kv_rows, s_win_kv = run_trunk(x_sample, state_dn_S, state_dn_conv, past, state_win_kv, PAST_LEN, P)
    return (y_prompt, y_sample, p_dn_S, p_dn_conv, p_kv_rows, p_win_kv, s_dn_S, s_dn_conv, s_kv_rows, s_win_kv)
```

```python
import functools
import math

import jax
import jax.numpy as jnp
from jax import lax
from jax.experimental import pallas as pl
from jax.experimental.pallas import tpu as pltpu

F32 = jnp.float32
BF16 = jnp.bfloat16

D_MODEL = 1024
N_A_LAYERS = 2
N_B_LAYERS = 2
NORM_EPS = 1e-6
DN_HEADS = 8
DN_HEAD = 128
DN_QKV = 3 * D_MODEL
DN_CONV = 4
DN_CHUNK = 64
NSA_HEADS = 16
NSA_HD = 64
NSA_KVH = 4
NSA_GROUP = 4
CMP_STRIDE = 16
CMP_BLOCK = 32
SEL_BLOCK = 64
N_SEL = 16
WINDOW = 512
PAGE = 128
REL_BUCKETS = 32
REL_MAX_EXACT = 16
REL_MAX_DIST = 1024
NEG = -1e30

V7X_VMEM_LIMIT = 56 * 1024 * 1024
LANES = 128


def _cparams(sem):
    return pltpu.CompilerParams(dimension_semantics=sem, vmem_limit_bytes=V7X_VMEM_LIMIT)


def _rms(x, w):
    ms = jnp.mean(x * x, axis=-1, keepdims=True)
    return x * lax.rsqrt(ms + NORM_EPS) * w


def _silu(x):
    return x * (1.0 / (1.0 + jnp.exp(-x)))


def _sigmoid(x):
    return 1.0 / (1.0 + jnp.exp(-x))


def _softplus(x):
    return jnp.maximum(x, 0.0) + jnp.log1p(jnp.exp(-jnp.abs(x)))


def _dot(a, b):
    return jnp.dot(a.astype(BF16), b.astype(BF16), preferred_element_type=F32)


def _dot_nt(a, b):
    return lax.dot_general(a.astype(BF16), b.astype(BF16), (((1,), (1,)), ((), ())),
                           preferred_element_type=F32)


def _dot_split(a, b_exact):
    hi = a.astype(BF16)
    lo = (a - hi.astype(F32)).astype(BF16)
    b = b_exact.astype(BF16)
    return (jnp.dot(hi, b, preferred_element_type=F32) + jnp.dot(lo, b, preferred_element_type=F32))


def _linear_body(*refs, has_norm, has_res):
    it = iter(refs)
    x_ref = next(it)
    nw_ref = next(it) if has_norm else None
    w_ref = next(it)
    res_ref = next(it) if has_res else None
    o_ref = next(it)
    xn_ref = next(it)

    @pl.when(pl.program_id(1) == 0)
    def _():
        x = x_ref[...]
        if has_norm:
            x = _rms(x, nw_ref[...])
        xn_ref[...] = x.astype(BF16)

    acc = jnp.dot(xn_ref[...], w_ref[...], preferred_element_type=F32)
    if has_res:
        acc = acc + res_ref[...]
    o_ref[...] = acc


def linear(x, w, norm_w=None, residual=None, tm=512, tn=None):
    M, K = x.shape
    N = w.shape[1]
    tn = N if tn is None else tn
    assert M % tm == 0 and N % tn == 0
    has_norm, has_res = norm_w is not None, residual is not None
    args, specs = [x], [pl.BlockSpec((tm, K), lambda i, j: (i, 0))]
    if has_norm:
        args.append(norm_w.reshape(1, K))
        specs.append(pl.BlockSpec((1, K), lambda i, j: (0, 0)))
    args.append(w)
    specs.append(pl.BlockSpec((K, tn), lambda i, j: (0, j)))
    if has_res:
        args.append(residual)
        specs.append(pl.BlockSpec((tm, tn), lambda i, j: (i, j)))
    return pl.pallas_call(
        functools.partial(_linear_body, has_norm=has_norm, has_res=has_res),
        out_shape=jax.ShapeDtypeStruct((M, N), F32),
        grid=(M // tm, N // tn),
        in_specs=specs,
        out_specs=pl.BlockSpec((tm, tn), lambda i, j: (i, j)),
        scratch_shapes=[pltpu.VMEM((tm, K), BF16)],
        compiler_params=_cparams(("parallel", "arbitrary")),
        name="linear",
    )(*args)


def _ffn_body(x_ref, nw_ref, wg_ref, wu_ref, wo_ref, o_ref, xn_ref, acc_ref):
    f = pl.program_id(1)

    @pl.when(f == 0)
    def _():
        xn_ref[...] = _rms(x_ref[...], nw_ref[...]).astype(BF16)
        acc_ref[...] = jnp.zeros_like(acc_ref)

    xn = xn_ref[...]
    g = jnp.dot(xn, wg_ref[...], preferred_element_type=F32)
    u = jnp.dot(xn, wu_ref[...], preferred_element_type=F32)
    a = (_silu(g) * u).astype(BF16)
    acc_ref[...] += jnp.dot(a, wo_ref[...], preferred_element_type=F32)

    @pl.when(f == pl.num_programs(1) - 1)
    def _():
        o_ref[...] = x_ref[...] + acc_ref[...]


def ffn(x, norm_w, w_in, w_out, tm=512, tf=256):
    M, D = x.shape
    FF = w_out.shape[0]
    nf = FF // tf
    assert M % tm == 0 and FF % tf == 0
    return pl.pallas_call(
        _ffn_body,
        out_shape=jax.ShapeDtypeStruct((M, D), F32),
        grid=(M // tm, nf),
        in_specs=[
            pl.BlockSpec((tm, D), lambda i, f: (i, 0)),
            pl.BlockSpec((1, D), lambda i, f: (0, 0)),
            pl.BlockSpec((D, tf), lambda i, f: (0, f)),
            pl.BlockSpec((D, tf), lambda i, f: (0, f + nf)),
            pl.BlockSpec((tf, D), lambda i, f: (f, 0)),
        ],
        out_specs=pl.BlockSpec((tm, D), lambda i, f: (i, 0)),
        scratch_shapes=[pltpu.VMEM((tm, D), BF16), pltpu.VMEM((tm, D), F32)],
        compiler_params=_cparams(("parallel", "arbitrary")),
        name="ffn",
    )(x, norm_w.reshape(1, D), w_in, w_in, w_out)


def _final_norm_body(x_ref, w_ref, o_ref):
    o_ref[...] = _rms(x_ref[...], w_ref[...])


def final_norm(x, w, tm=512):
    M, D = x.shape
    return pl.pallas_call(
        _final_norm_body,
        out_shape=jax.ShapeDtypeStruct((M, D), F32),
        grid=(M // tm,),
        in_specs=[pl.BlockSpec((tm, D), lambda i: (i, 0)), pl.BlockSpec((1, D), lambda i: (0, 0))],
        out_specs=pl.BlockSpec((tm, D), lambda i: (i, 0)),
        compiler_params=_cparams(("parallel",)),
        name="final_norm",
    )(x, w.reshape(1, D))


def _l2n(x):
    return x * lax.rsqrt(jnp.sum(x * x, axis=-1, keepdims=True) + NORM_EPS)


def _gated_out(o, z, onorm):
    return _rms(o, onorm) * _silu(z)


def _dn_prompt_body(qkv_ref, z_ref, ab_ref, abT_ref, cw_ref, alr_ref, dtr_ref, alc_ref, dtc_ref, on_ref,
                    o_ref, s_out_ref, xbuf_ref, s_ref):
    n = pl.program_id(1)
    C = DN_CHUNK

    @pl.when(n == 0)
    def _():
        xbuf_ref[0:8, :] = jnp.zeros((8, DN_QKV), F32)
        s_ref[...] = jnp.zeros_like(s_ref)

    xbuf_ref[8:8 + C, :] = qkv_ref[...]
    y = xbuf_ref[5:5 + C, :] * cw_ref[0:1, :]
    for i in range(1, DN_CONV):
        y = y + xbuf_ref[5 + i:5 + i + C, :] * cw_ref[i:i + 1, :]
    xbuf_ref[0:8, :] = xbuf_ref[C:C + 8, :]
    y = _silu(y)

    ab = ab_ref[...]
    g8 = -jnp.exp(alr_ref[...]) * _softplus(ab[:, 0:DN_HEADS] + dtr_ref[...])
    beta8 = _sigmoid(ab[:, DN_HEADS:2 * DN_HEADS])
    abT = abT_ref[0, 0]
    g8T = -jnp.exp(alc_ref[...]) * _softplus(abT[0:DN_HEADS, :] + dtc_ref[...])

    ii = lax.broadcasted_iota(jnp.int32, (C, C), 0)
    jj = lax.broadcasted_iota(jnp.int32, (C, C), 1)
    incl = ii >= jj
    strict = ii > jj
    onorm = on_ref[...]

    for h in range(DN_HEADS):
        sl = slice(h * DN_HEAD, (h + 1) * DN_HEAD)
        q = _l2n(y[:, sl]) * (DN_HEAD ** -0.5)
        k = _l2n(y[:, D_MODEL + h * DN_HEAD:D_MODEL + (h + 1) * DN_HEAD])
        v = y[:, 2 * D_MODEL + h * DN_HEAD:2 * D_MODEL + (h + 1) * DN_HEAD]
        g_col = g8[:, h:h + 1]
        g_row = g8T[h:h + 1, :]
        beta = beta8[:, h:h + 1]
        G_col = jnp.sum(jnp.where(incl, g_row, 0.0), axis=1, keepdims=True)
        G_row = jnp.sum(jnp.where(ii <= jj, g_col, 0.0), axis=0, keepdims=True)
        dec = jnp.where(incl, jnp.exp(jnp.where(incl, G_col - G_row, 0.0)), 0.0)
        A = jnp.where(strict, beta * dec * _dot_nt(k, k), 0.0)
        X = -A
        Tm = X
        for _ in range(int(math.log2(C)) - 1):
            X = _dot(X, X)
            Tm = Tm + X + _dot(X, Tm)
        eG = jnp.exp(G_col)
        bv = beta * v
        bk = (beta * eG) * k
        w = jnp.concatenate([bv, bk], axis=1)
        w = w + _dot(Tm, w)
        wv, wk = w[:, :DN_HEAD], w[:, DN_HEAD:]
        aqk = dec * _dot_nt(q, k)
        qg = eG * q
        G_last = G_col[C - 1:C, :]
        kdec = jnp.exp(G_last - G_col) * k
        S = s_ref[h]
        ws = _dot(jnp.concatenate([wk, qg], axis=0), S)
        U = wv - ws[:C]
        O = ws[C:] + _dot(aqk, U)
        s_ref[h] = jnp.exp(G_last) * S + _dot(kdec.T, U)
        o_ref[:, sl] = _gated_out(O, z_ref[:, sl], onorm)

    @pl.when(n == pl.num_programs(1) - 1)
    def _():
        s_out_ref[0] = s_ref[...]


def dn_prompt(proj, n_seq, T, conv_w, a_log, dt_bias, out_norm):
    M = n_seq * T
    C = DN_CHUNK
    N = T // C
    abT = proj[:M, 4 * D_MODEL:4 * D_MODEL + 2 * DN_HEADS].reshape(n_seq, N, C, 2 * DN_HEADS).transpose(0, 1, 3, 2)
    row = lambda a: a.reshape(1, DN_HEADS)
    col = lambda a: a.reshape(DN_HEADS, 1)
    return pl.pallas_call(
        _dn_prompt_body,
        out_shape=[jax.ShapeDtypeStruct((M, D_MODEL), F32),
                   jax.ShapeDtypeStruct((n_seq, DN_HEADS, DN_HEAD, DN_HEAD), F32)],
        grid=(n_seq, N),
        in_specs=[
            pl.BlockSpec((C, DN_QKV), lambda b, n: (b * N + n, 0)),
            pl.BlockSpec((C, D_MODEL), lambda b, n: (b * N + n, 3)),
            pl.BlockSpec((C, LANES), lambda b, n: (b * N + n, 4 * D_MODEL // LANES)),
            pl.BlockSpec((1, 1, 2 * DN_HEADS, C), lambda b, n: (b, n, 0, 0)),
            pl.BlockSpec((DN_CONV, DN_QKV), lambda b, n: (0, 0)),
            pl.BlockSpec((1, DN_HEADS), lambda b, n: (0, 0)),
            pl.BlockSpec((1, DN_HEADS), lambda b, n: (0, 0)),
            pl.BlockSpec((DN_HEADS, 1), lambda b, n: (0, 0)),
            pl.BlockSpec((DN_HEADS, 1), lambda b, n: (0, 0)),
            pl.BlockSpec((1, DN_HEAD), lambda b, n: (0, 0)),
        ],
        out_specs=[pl.BlockSpec((C, D_MODEL), lambda b, n: (b * N + n, 0)),
                   pl.BlockSpec((1, DN_HEADS, DN_HEAD, DN_HEAD), lambda b, n: (b, 0, 0, 0))],
        scratch_shapes=[pltpu.VMEM((C + 8, DN_QKV), F32), pltpu.VMEM((DN_HEADS, DN_HEAD, DN_HEAD), F32)],
        compiler_params=_cparams(("parallel", "arbitrary")),
        name="dn_prompt",
    )(proj, proj, proj, abT, conv_w, row(a_log), row(dt_bias), col(a_log), col(dt_bias), out_norm.reshape(1, DN_HEAD))


def _dn_decode_body(proj_ref, cbuf_ref, s0_ref, cw_ref, alr_ref, dtr_ref, on_ref, o_ref, s_out_ref, xbuf_ref, oacc_ref):
    T = proj_ref.shape[1]
    x = proj_ref[0]
    xbuf_ref[8 - (DN_CONV - 1):8, :] = cbuf_ref[0]
    xbuf_ref[8:8 + T, :] = x[:, :DN_QKV]
    y = xbuf_ref[5:5 + T, :] * cw_ref[0:1, :]
    for i in range(1, DN_CONV):
        y = y + xbuf_ref[5 + i:5 + i + T, :] * cw_ref[i:i + 1, :]
    y = _silu(y)
    ab = x[:, 4 * D_MODEL:4 * D_MODEL + LANES]
    a8 = jnp.exp(-jnp.exp(alr_ref[...]) * _softplus(ab[:, 0:DN_HEADS] + dtr_ref[...]))
    beta8 = _sigmoid(ab[:, DN_HEADS:2 * DN_HEADS])
    ii = lax.broadcasted_iota(jnp.int32, (DN_HEAD, DN_HEAD), 0)
    jj = lax.broadcasted_iota(jnp.int32, (DN_HEAD, DN_HEAD), 1)
    eye = ii == jj

    def to_col(r):
        return jnp.sum(jnp.where(eye, r, 0.0), axis=1, keepdims=True)

    for h in range(DN_HEADS):
        sl = slice(h * DN_HEAD, (h + 1) * DN_HEAD)
        q = _l2n(y[:, sl]) * (DN_HEAD ** -0.5)
        k = _l2n(y[:, D_MODEL + h * DN_HEAD:D_MODEL + (h + 1) * DN_HEAD])
        v = y[:, 2 * D_MODEL + h * DN_HEAD:2 * D_MODEL + (h + 1) * DN_HEAD]
        S = s0_ref[0, h]
        for t in range(T):
            k_col = to_col(k[t:t + 1, :])
            q_col = to_col(q[t:t + 1, :])
            a = a8[t:t + 1, h:h + 1]
            b = beta8[t:t + 1, h:h + 1]
            kS = jnp.sum(k_col * S, axis=0, keepdims=True)
            S = a * S + k_col * (b * (v[t:t + 1, :] - a * kS))
            oacc_ref[t:t + 1, sl] = jnp.sum(q_col * S, axis=0, keepdims=True)
        s_out_ref[0, h] = S
    onorm = on_ref[...]
    for h in range(DN_HEADS):
        sl = slice(h * DN_HEAD, (h + 1) * DN_HEAD)
        o_ref[0, :, sl] = _gated_out(oacc_ref[0:T, sl], x[:, DN_QKV + h * DN_HEAD:DN_QKV + (h + 1) * DN_HEAD], onorm)


def dn_decode(proj, n_seq, conv_buf, S0, conv_w, a_log, dt_bias, out_norm):
    M, W = proj.shape
    T = M // n_seq
    row = lambda a: a.reshape(1, DN_HEADS)
    o, S = pl.pallas_call(
        _dn_decode_body,
        out_shape=[jax.ShapeDtypeStruct((n_seq, T, D_MODEL), F32),
                   jax.ShapeDtypeStruct((n_seq, DN_HEADS, DN_HEAD, DN_HEAD), F32)],
        grid=(n_seq,),
        in_specs=[
            pl.BlockSpec((1, T, W), lambda b: (b, 0, 0)),
            pl.BlockSpec((1, DN_CONV - 1, DN_QKV), lambda b: (b, 0, 0)),
            pl.BlockSpec((1, DN_HEADS, DN_HEAD, DN_HEAD), lambda b: (b, 0, 0, 0)),
            pl.BlockSpec((DN_CONV, DN_QKV), lambda b: (0, 0)),
            pl.BlockSpec((1, DN_HEADS), lambda b: (0, 0)),
            pl.BlockSpec((1, DN_HEADS), lambda b: (0, 0)),
            pl.BlockSpec((1, DN_HEAD), lambda b: (0, 0)),
        ],
        out_specs=[pl.BlockSpec((1, T, D_MODEL), lambda b: (b, 0, 0)),
                   pl.BlockSpec((1, DN_HEADS, DN_HEAD, DN_HEAD), lambda b: (b, 0, 0, 0))],
        scratch_shapes=[pltpu.VMEM((16, DN_QKV), F32), pltpu.VMEM((8, D_MODEL), F32)],
        compiler_params=_cparams(("parallel",)),
        name="dn_decode",
    )(proj.reshape(n_seq, T, W), conv_buf, S0, conv_w, row(a_log), row(dt_bias), out_norm.reshape(1, DN_HEAD))
    return o.reshape(M, D_MODEL), S


def _bucket_thresholds():
    thr, prev = [], REL_MAX_EXACT
    for d in range(REL_MAX_EXACT, REL_MAX_DIST + 1):
        val = min(REL_MAX_EXACT + int(math.log(d / REL_MAX_EXACT) / math.log(REL_MAX_DIST / REL_MAX_EXACT)
                                      * (REL_BUCKETS - REL_MAX_EXACT)), REL_BUCKETS - 1)
        thr += [d] * (val - prev)
        prev = val
    assert len(thr) == REL_BUCKETS - 1 - REL_MAX_EXACT
    return tuple(thr)


_BUCKET_THR = _bucket_thresholds()
BAND_TOP = 1280
BAND_W = BAND_TOP + 256
TQ = 128
TK = 256


def _bucket(d):
    n = jnp.maximum(d, 0)
    big = jnp.full(n.shape, REL_MAX_EXACT, jnp.int32)
    for t in _BUCKET_THR:
        big = big + (n >= t).astype(jnp.int32)
    return jnp.where(n < REL_MAX_EXACT, n, big)


def _bias_lookup(bucket, table_row):
    acc = jnp.zeros(bucket.shape, F32)
    for k in range(REL_BUCKETS):
        acc = acc + jnp.where(bucket == k, table_row(k), 0.0)
    return acc


def _bias_cmp_body(tab_ref, o_ref, *, n_cmp):
    q0 = pl.program_id(0) * TQ
    shp = o_ref.shape[1:]
    t = q0 + lax.broadcasted_iota(jnp.int32, shp, 0)
    j = lax.broadcasted_iota(jnp.int32, shp, 1)
    d = t - (j * CMP_STRIDE + CMP_BLOCK - 1)
    dead = (d < 0) | (j >= n_cmp)
    bucket = _bucket(d)
    for h in range(NSA_HEADS):
        o_ref[h] = jnp.where(dead, NEG, _bias_lookup(bucket, lambda k: tab_ref[k, h]))


def bias_cmp_prompt(rel_bias, T):
    n_sub = T // CMP_STRIDE
    return pl.pallas_call(
        functools.partial(_bias_cmp_body, n_cmp=n_sub - 1),
        out_shape=jax.ShapeDtypeStruct((NSA_HEADS, T, n_sub), F32),
        grid=(T // TQ,),
        in_specs=[pl.BlockSpec(memory_space=pltpu.SMEM)],
        out_specs=pl.BlockSpec((NSA_HEADS, TQ, n_sub), lambda i: (0, i, 0)),
        compiler_params=_cparams(("parallel",)),
        name="bias_cmp",
    )(rel_bias)


def _bias_band_body(tab_ref, o_ref):
    h = pl.program_id(0)
    shp = o_ref.shape[1:]
    d = BAND_TOP + lax.broadcasted_iota(jnp.int32, shp, 0) - lax.broadcasted_iota(jnp.int32, shp, 1)
    o_ref[0] = jnp.where(d < 0, NEG, _bias_lookup(_bucket(d), lambda k: tab_ref[k, h]))


def bias_band(rel_bias):
    return pl.pallas_call(
        _bias_band_body,
        out_shape=jax.ShapeDtypeStruct((NSA_HEADS, TQ, BAND_W), F32),
        grid=(NSA_HEADS,),
        in_specs=[pl.BlockSpec(memory_space=pltpu.SMEM)],
        out_specs=pl.BlockSpec((1, TQ, BAND_W), lambda h: (h, 0, 0)),
        compiler_params=_cparams(("parallel",)),
        name="bias_band",
    )(rel_bias)


def _bias_decode_body(tabc_ref, cmp_ref, sel_ref, win_ref, *, past, tq):
    shp = (PAGE, NSA_HEADS * tq)
    i = lax.broadcasted_iota(jnp.int32, shp, 0)
    c = lax.broadcasted_iota(jnp.int32, shp, 1)
    qpos = past + (c & (tq - 1))
    row = lambda k: tabc_ref[k:k + 1, :]

    def table(d, dead):
        return jnp.where(dead | (d < 0), NEG, _bias_lookup(_bucket(d), row))

    ic = lax.broadcasted_iota(jnp.int32, cmp_ref.shape, 0)
    qc = past + (lax.broadcasted_iota(jnp.int32, cmp_ref.shape, 1) & (tq - 1))
    cmp_ref[...] = table(qc - (ic * CMP_STRIDE + CMP_BLOCK - 1), ic < 0)
    n_pages = past // PAGE
    for p in range(n_pages):
        sel_ref[p] = table(qpos - (p * PAGE + i), i < 0)
    sel_ref[n_pages] = table(qpos - (past + i), i >= tq)
    n_wt = WINDOW // PAGE
    for t in range(n_wt):
        d = qpos - (past - WINDOW + t * PAGE + i)
        win_ref[t] = table(d, d >= WINDOW)
    win_ref[n_wt] = table(qpos - (past + i), i >= tq)


def bias_decode(rel_bias, past, tq):
    assert tq & (tq - 1) == 0
    tabc = jnp.repeat(rel_bias, tq, axis=1)
    n_pages = past // PAGE
    nc = NSA_HEADS * tq
    return pl.pallas_call(
        functools.partial(_bias_decode_body, past=past, tq=tq),
        out_shape=[jax.ShapeDtypeStruct((past // CMP_STRIDE, nc), F32),
                   jax.ShapeDtypeStruct((n_pages + 1, PAGE, nc), F32),
                   jax.ShapeDtypeStruct((WINDOW // PAGE + 1, PAGE, nc), F32)],
        name="bias_decode",
    )(tabc)


def _cmp_weights(cmp_pos_w, w_cmp):
    w = jnp.concatenate([cmp_pos_w[0].reshape(CMP_BLOCK, -1), cmp_pos_w[1].reshape(CMP_BLOCK, -1)], axis=1)
    blocks = w_cmp.reshape(2 * NSA_KVH, NSA_HD, NSA_HD)
    n = 2 * NSA_KVH
    wbd = (jnp.eye(n, dtype=F32)[:, None, :, None] * blocks[:, :, None, :]).reshape(n * NSA_HD, n * NSA_HD)
    return w[:CMP_STRIDE], w[CMP_STRIDE:], wbd.astype(BF16)


def _pool16(x, w):
    n = x.shape[0] // CMP_STRIDE
    return jnp.sum(x.reshape(n, CMP_STRIDE, x.shape[1]) * w[None], axis=1)


def _compress_prompt_body(x_ref, wlo_ref, whi_ref, wbd_ref, o_ref, lo_ref, hi_ref):
    T = x_ref.shape[0]
    n_sub = T // CMP_STRIDE
    step = 512
    for c in range(T // step):
        xs = x_ref[c * step:(c + 1) * step, :]
        r = slice(c * step // CMP_STRIDE, (c + 1) * step // CMP_STRIDE)
        lo_ref[r, :] = _pool16(xs, wlo_ref[...])
        hi_ref[r, :] = _pool16(xs, whi_ref[...])
    hi_ref[n_sub:n_sub + 8, :] = jnp.zeros((8, hi_ref.shape[1]), F32)
    blocks = lo_ref[...] + hi_ref[1:n_sub + 1, :]
    o_ref[...] = _dot(blocks, wbd_ref[...]).astype(BF16)


def compress_prompt(rows, n_seq, T, wlo, whi, wbd):
    n_sub = T // CMP_STRIDE
    W = 2 * NSA_KVH * NSA_HD
    return pl.pallas_call(
        _compress_prompt_body,
        out_shape=jax.ShapeDtypeStruct((n_seq * n_sub, W), BF16),
        grid=(n_seq,),
        in_specs=[pl.BlockSpec((T, W), lambda b: (b, 0)),
                  pl.BlockSpec((CMP_STRIDE, W), lambda b: (0, 0)),
                  pl.BlockSpec((CMP_STRIDE, W), lambda b: (0, 0)),
                  pl.BlockSpec((W, W), lambda b: (0, 0))],
        out_specs=pl.BlockSpec((n_sub, W), lambda b: (b, 0)),
        scratch_shapes=[pltpu.VMEM((n_sub, W), F32), pltpu.VMEM((n_sub + 8, W), F32)],
        compiler_params=_cparams(("parallel",)),
        name="compress_prompt",
    )(rows, wlo, whi, wbd)


def _masked_softmax(s, mask, axis):
    l = jnp.where(mask, s, NEG)
    m = jnp.max(l, axis=axis, keepdims=True)
    e = jnp.where(mask, jnp.exp(l - m), 0.0)
    return e / jnp.maximum(jnp.sum(e, axis=axis, keepdims=True), 1e-30)


def _split3(x):
    hi = x.astype(BF16)
    r = x - hi.astype(F32)
    mid = r.astype(BF16)
    lo = (r - mid.astype(F32)).astype(BF16)
    return hi, mid, lo


def _topk_mask(score, blk, n_pick, axis, removed):
    sel = jnp.zeros(score.shape, F32)
    s = score
    big = jnp.int32(1 << 20)
    for _ in range(n_pick):
        mx = jnp.max(s, axis=axis, keepdims=True)
        idx = jnp.min(jnp.where(s == mx, blk, big), axis=axis, keepdims=True)
        hit = blk == idx
        sel = jnp.where(hit, 1.0, sel)
        s = jnp.where(hit, removed, s)
    return sel


def _attn_prompt_body(q_ref, g_ref, ks_ref, vs_ref, kw_ref, vw_ref, kc_ref, vc_ref, bc_ref, band_ref, o_ref):
    qb = pl.program_id(2)
    q0 = qb * TQ
    G = NSA_GROUP
    R = G * TQ
    qblk = q_ref[...] * (NSA_HD ** -0.5)
    Q = jnp.concatenate([qblk[:, g * NSA_HD:(g + 1) * NSA_HD] for g in range(G)], axis=0).astype(BF16)

    n_sub = bc_ref.shape[2]
    bc = bc_ref[...].reshape(R, n_sub)
    p_c = _masked_softmax(_dot_nt(Q, kc_ref[0, 0, 0]) + bc, bc > 0.5 * NEG, 1)
    o_cmp = _dot(p_c, vc_ref[0, 0, 0])
    imp = p_c[0:TQ]
    for g in range(1, G):
        imp = imp + p_c[g * TQ:(g + 1) * TQ]

    n_blk = n_sub * CMP_STRIDE // SEL_BLOCK
    per = SEL_BLOCK // CMP_STRIDE
    mm = lax.broadcasted_iota(jnp.int32, (n_sub, n_blk), 0)
    bb = lax.broadcasted_iota(jnp.int32, (n_sub, n_blk), 1)
    pool = ((mm // per == bb).astype(F32) + ((mm + 1) // per == bb).astype(F32)).astype(BF16)
    p_slc = sum(jnp.dot(part, pool, preferred_element_type=F32) for part in _split3(imp))
    t = q0 + lax.broadcasted_iota(jnp.int32, (TQ, n_blk), 0)
    blk = lax.broadcasted_iota(jnp.int32, (TQ, n_blk), 1)
    cur = t // SEL_BLOCK
    forced = (blk == 0) | (blk == cur) | (blk == cur - 1)
    score = jnp.where(forced, 1e4, jnp.where(blk * SEL_BLOCK <= t, p_slc, -1.0))
    sel = _topk_mask(score, blk, min(N_SEL, n_blk), 1, -2.0).astype(BF16)

    rel = (lax.broadcasted_iota(jnp.int32, (R, TK), 0) & (TQ - 1)) - lax.broadcasted_iota(jnp.int32, (R, TK), 1)
    eb = lax.broadcasted_iota(jnp.int32, (n_blk, TK), 0)
    ej = lax.broadcasted_iota(jnp.int32, (n_blk, TK), 1) // SEL_BLOCK

    def flash(k_ref, v_ref, lo, hi, selected):
        def step(kt, carry):
            m, l, acc = carry
            k0 = pl.multiple_of(kt * TK, TK)
            delta = q0 - k0
            c0 = pl.multiple_of(jnp.maximum(BAND_TOP - delta, 0), LANES)
            bias = band_ref[:, :, pl.ds(c0, TK)].reshape(R, TK)
            s = _dot_nt(Q, k_ref[0, 0, 0, pl.ds(k0, TK), :]) + bias
            mask = bias > 0.5 * NEG
            if selected:
                expand = (eb == ej + kt * (TK // SEL_BLOCK)).astype(BF16)
                st = jnp.dot(sel, expand, preferred_element_type=F32)
                mask = mask & (jnp.concatenate([st] * G, axis=0) > 0.5)
            else:
                mask = mask & (delta + rel < WINDOW)
            s = jnp.where(mask, s, NEG)
            m_new = jnp.maximum(m, jnp.max(s, axis=1, keepdims=True))
            p = jnp.where(mask, jnp.exp(s - m_new), 0.0)
            alpha = jnp.exp(m - m_new)
            l = alpha * l + jnp.sum(p, axis=1, keepdims=True)
            acc = alpha * acc + _dot(p, v_ref[0, 0, 0, pl.ds(k0, TK), :])
            return m_new, l, acc

        init = (jnp.full((R, 1), NEG, F32), jnp.zeros((R, 1), F32), jnp.zeros((R, NSA_HD), F32))
        _, l, acc = lax.fori_loop(lo, hi, step, init)
        return acc / jnp.maximum(l, 1e-30)

    o_sel = flash(ks_ref, vs_ref, 0, (q0 + TQ + TK - 1) // TK, True)
    o_win = flash(kw_ref, vw_ref, jnp.maximum(q0 - WINDOW + 1, 0) // TK, (q0 + TQ - 1) // TK + 1, False)

    gt = _sigmoid(g_ref[0])
    outs = []
    for g in range(G):
        r = slice(g * TQ, (g + 1) * TQ)
        outs.append(o_cmp[r] * gt[:, 3 * g:3 * g + 1] + o_sel[r] * gt[:, 3 * g + 1:3 * g + 2]
                    + o_win[r] * gt[:, 3 * g + 2:3 * g + 3])
    o_ref[...] = jnp.concatenate(outs, axis=1)


def attn_prompt(proj, n_seq, T, kv_h, cmp_h, bias_c, band):
    M = n_seq * T
    NQ = T // TQ
    n_sub = T // CMP_STRIDE
    gl = proj[:M, NSA_HEADS * NSA_HD:NSA_HEADS * NSA_HD + 3 * NSA_HEADS]
    gl = gl.reshape(M, NSA_KVH, 3 * NSA_GROUP).transpose(1, 0, 2)
    kv_spec = lambda kind: pl.BlockSpec((1, 1, 1, T, NSA_HD), lambda k, b, i: (kind, b, k, 0, 0))
    cmp_spec = lambda kind: pl.BlockSpec((1, 1, 1, n_sub, NSA_HD), lambda k, b, i: (kind, b, k, 0, 0))
    W = NSA_GROUP * NSA_HD
    return pl.pallas_call(
        _attn_prompt_body,
        out_shape=jax.ShapeDtypeStruct((M, NSA_HEADS * NSA_HD), F32),
        grid=(NSA_KVH, n_seq, NQ),
        in_specs=[
            pl.BlockSpec((TQ, W), lambda k, b, i: (b * NQ + i, k)),
            pl.BlockSpec((1, TQ, 3 * NSA_GROUP), lambda k, b, i: (k, b * NQ + i, 0)),
            kv_spec(2), kv_spec(3), kv_spec(4), kv_spec(5),
            cmp_spec(0), cmp_spec(1),
            pl.BlockSpec((NSA_GROUP, TQ, n_sub), lambda k, b, i: (k, i, 0)),
            pl.BlockSpec((NSA_GROUP, TQ, BAND_W), lambda k, b, i: (k, 0, 0)),
        ],
        out_specs=pl.BlockSpec((TQ, W), lambda k, b, i: (b * NQ + i, k)),
        compiler_params=_cparams(("parallel", "parallel", "arbitrary")),
        name="attn_prompt",
    )(proj, gl, kv_h, kv_h, kv_h, kv_h, cmp_h, cmp_h, bias_c, band)


def _compress_decode_body(pt_ref, *refs, n_pages):
    pages = refs[:n_pages]
    new_ref, wlo_ref, whi_ref, wbd_ref, o_ref, lo_ref, hi_ref = refs[n_pages:]
    per = PAGE // CMP_STRIDE
    for p in range(n_pages):
        x = pages[p][0]
        lo_ref[p * per:(p + 1) * per, :] = _pool16(x, wlo_ref[...])
        hi_ref[p * per:(p + 1) * per, :] = _pool16(x, whi_ref[...])
    n_sub = n_pages * per
    tq = new_ref.shape[1]
    hi_ref[n_sub:n_sub + 8, :] = jnp.zeros((8, hi_ref.shape[1]), F32)
    hi_ref[n_sub:n_sub + 1, :] = jnp.sum(new_ref[0][:, :hi_ref.shape[1]] * whi_ref[0:tq, :], axis=0, keepdims=True)
    blocks = lo_ref[...] + hi_ref[1:n_sub + 1, :]
    o_ref[0] = _dot(blocks, wbd_ref[...]).astype(BF16)


def compress_decode(cache, page_table, new_rows, wlo, whi, wbd):
    nb, n_pages = page_table.shape
    W = 2 * NSA_KVH * NSA_HD
    tq = new_rows.shape[1]
    assert tq <= CMP_STRIDE
    n_sub = n_pages * PAGE // CMP_STRIDE
    page_spec = lambda p: pl.BlockSpec((1, PAGE, W), lambda b, pt: (pt[b, p], 0, 0))
    const = lambda shape: pl.BlockSpec(shape, lambda b, pt: (0,) * len(shape))
    return pl.pallas_call(
        functools.partial(_compress_decode_body, n_pages=n_pages),
        out_shape=jax.ShapeDtypeStruct((nb, n_sub, W), BF16),
        grid_spec=pltpu.PrefetchScalarGridSpec(
            num_scalar_prefetch=1,
            grid=(nb,),
            in_specs=[page_spec(p) for p in range(n_pages)] + [
                pl.BlockSpec((1, tq, new_rows.shape[2]), lambda b, pt: (b, 0, 0)),
                const((CMP_STRIDE, W)), const((CMP_STRIDE, W)), const((W, W))],
            out_specs=pl.BlockSpec((1, n_sub, W), lambda b, pt: (b, 0, 0)),
            scratch_shapes=[pltpu.VMEM((n_sub, W), F32), pltpu.VMEM((n_sub + 8, W), F32)],
        ),
        compiler_params=_cparams(("parallel",)),
        name="compress_decode",
    )(page_table, *([cache] * n_pages), new_rows, wlo, whi, wbd)


def _attn_decode_body(pt_ref, *refs, n_pages, tq):
    pages = refs[:n_pages]
    q_ref, g_ref, new_ref, win_ref, cmp_ref, bcmp_ref, bsel_ref, bwin_ref, o_ref = refs[n_pages:]
    W = NSA_KVH * NSA_HD
    NC = NSA_HEADS * tq
    Qbd = q_ref[0]
    past = n_pages * PAGE

    cm = cmp_ref[0]
    bc = bcmp_ref[...]
    p_c = _masked_softmax(_dot_nt(cm[:, :W], Qbd) + bc, bc > 0.5 * NEG, 0)
    tn = (((0,), (0,)), ((), ()))
    o_cmp = lax.dot_general(cm[:, W:], p_c.astype(BF16), tn, preferred_element_type=F32)

    n_sub = cm.shape[0]
    per = SEL_BLOCK // CMP_STRIDE
    n_blk = past // SEL_BLOCK + 1
    nb_pad = (n_blk + 7) // 8 * 8
    ci = lax.broadcasted_iota(jnp.int32, (NC, NC), 0)
    cj = lax.broadcasted_iota(jnp.int32, (NC, NC), 1)
    gq = NSA_GROUP * tq
    same = ((ci // gq == cj // gq) & ((ci & (tq - 1)) == (cj & (tq - 1)))).astype(BF16)
    imp = sum(jnp.dot(part, same, preferred_element_type=F32) for part in _split3(p_c))
    bb = lax.broadcasted_iota(jnp.int32, (nb_pad, n_sub), 0)
    mm = lax.broadcasted_iota(jnp.int32, (nb_pad, n_sub), 1)
    pool = ((mm // per == bb).astype(F32) + ((mm + 1) // per == bb).astype(F32)).astype(BF16)
    p_slc = sum(jnp.dot(pool, part, preferred_element_type=F32) for part in _split3(imp))
    blk = lax.broadcasted_iota(jnp.int32, (nb_pad, NC), 0)
    qpos = past + (lax.broadcasted_iota(jnp.int32, (nb_pad, NC), 1) & (tq - 1))
    cur = qpos // SEL_BLOCK
    forced = (blk == 0) | (blk == cur) | (blk == cur - 1)
    score = jnp.where(forced, 1e4, jnp.where(blk * SEL_BLOCK <= qpos, p_slc, -1.0))
    score = jnp.where(blk < n_blk, score, -3.0)
    sel = _topk_mask(score, blk, min(N_SEL, n_blk), 0, -5.0)

    def flash_tile(carry, k, v, bias, mask):
        m, l, acc = carry
        s = jnp.where(mask, _dot_nt(k, Qbd) + bias, NEG)
        m_new = jnp.maximum(m, jnp.max(s, axis=0, keepdims=True))
        p = jnp.where(mask, jnp.exp(s - m_new), 0.0)
        alpha = jnp.exp(m - m_new)
        l = alpha * l + jnp.sum(p, axis=0, keepdims=True)
        acc = alpha * acc + lax.dot_general(v.astype(BF16), p.astype(BF16), tn, preferred_element_type=F32)
        return m_new, l, acc

    init = (jnp.full((1, NC), NEG, F32), jnp.zeros((1, NC), F32), jnp.zeros((W, NC), F32))
    new = new_ref[0]
    pad = jnp.zeros((8 - tq, W), F32)
    new_tile = lambda kind: jnp.concatenate([new[:, kind * W:(kind + 1) * W], pad], axis=0)

    half = lax.broadcasted_iota(jnp.int32, (PAGE, NC), 0) < SEL_BLOCK
    carry = init
    for p in range(n_pages):
        pg = pages[p][0]
        b0 = p * (PAGE // SEL_BLOCK)
        picked = jnp.where(half, sel[b0:b0 + 1, :], sel[b0 + 1:b0 + 2, :]) > 0.5
        bias = bsel_ref[p]
        carry = flash_tile(carry, pg[:, :W], pg[:, W:], bias, picked & (bias > 0.5 * NEG))
    bias = bsel_ref[n_pages][0:8]
    carry = flash_tile(carry, new_tile(2), new_tile(3), bias, (sel[n_blk - 1:n_blk, :] > 0.5) & (bias > 0.5 * NEG))
    o_sel = carry[2] / jnp.maximum(carry[1], 1e-30)

    carry = init
    for t in range(WINDOW // PAGE):
        wt = win_ref[0, t * PAGE:(t + 1) * PAGE, :]
        bias = bwin_ref[t]
        carry = flash_tile(carry, wt[:, :W], wt[:, W:], bias, bias > 0.5 * NEG)
    bias = bwin_ref[WINDOW // PAGE][0:8]
    carry = flash_tile(carry, new_tile(4), new_tile(5), bias, bias > 0.5 * NEG)
    o_win = carry[2] / jnp.maximum(carry[1], 1e-30)

    gt = _sigmoid(g_ref[0])
    o = o_cmp * gt[0:1] + o_sel * gt[1:2] + o_win * gt[2:3]
    kvh_of_col = lax.broadcasted_iota(jnp.int32, (NSA_HD, NC), 1) // gq
    out = jnp.zeros((NSA_HD, NC), F32)
    for k in range(NSA_KVH):
        out = out + jnp.where(kvh_of_col == k, o[k * NSA_HD:(k + 1) * NSA_HD, :], 0.0)
    o_ref[0] = out


def attn_decode(proj, cache_sel, page_table, new_rows, win_prev, cmp_d, bias_tabs):
    nb, n_pages = page_table.shape
    tq = new_rows.shape[1]
    W = NSA_KVH * NSA_HD
    NC = NSA_HEADS * tq
    q = proj[:, :NSA_HEADS * NSA_HD] * (NSA_HD ** -0.5)
    q = q.reshape(nb, tq, NSA_KVH, NSA_GROUP, NSA_HD).transpose(0, 2, 3, 1, 4)
    qbd = q[:, :, :, :, None, :] * jnp.eye(NSA_KVH, dtype=F32)[None, :, None, None, :, None]
    qbd = qbd.reshape(nb, NC, W).astype(BF16)
    gl = proj[:, NSA_HEADS * NSA_HD:NSA_HEADS * NSA_HD + 3 * NSA_HEADS]
    gl = gl.reshape(nb, tq, NSA_HEADS, 3).transpose(0, 3, 2, 1).reshape(nb, 3, NC)
    bcmp, bsel, bwin = bias_tabs
    page_spec = lambda p: pl.BlockSpec((1, PAGE, 2 * W), lambda b, pt: (pt[b, p], 0, 1))
    const = lambda shape: pl.BlockSpec(shape, lambda b, pt: (0,) * len(shape))
    per_b = lambda shape: pl.BlockSpec((1,) + shape, lambda b, pt: (b,) + (0,) * len(shape))
    o = pl.pallas_call(
        functools.partial(_attn_decode_body, n_pages=n_pages, tq=tq),
        out_shape=jax.ShapeDtypeStruct((nb, NSA_HD, NC), F32),
        grid_spec=pltpu.PrefetchScalarGridSpec(
            num_scalar_prefetch=1,
            grid=(nb,),
            in_specs=[page_spec(p) for p in range(n_pages)] + [
                per_b((NC, W)), per_b((3, NC)), per_b((tq, 6 * W)), per_b((WINDOW, 2 * W)),
                per_b((cmp_d.shape[1], 2 * W)),
                const(bcmp.shape), const(bsel.shape), const(bwin.shape)],
            out_specs=per_b((NSA_HD, NC)),
        ),
        compiler_params=_cparams(("parallel",)),
        name="attn_decode",
    )(page_table, *([cache_sel] * n_pages), qbd, gl, new_rows, win_prev, cmp_d, bcmp, bsel, bwin)
    return o.reshape(nb, NSA_HD, NSA_HEADS, tq).transpose(0, 3, 2, 1).reshape(nb * tq, NSA_HEADS * NSA_HD)


def _pad_cols(w, n):
    return jnp.pad(w, ((0, 0), (0, n - w.shape[1])))


def kernel(x_prompt, x_sample, state_dn_S, state_dn_conv, cache_kv, state_win_kv, page_table, norm_mix, norm_ffn, norm_kv, norm_final, ffn_w_in, ffn_w_out, dn_w_in, dn_conv_w, dn_A_log, dn_dt_bias, dn_out_norm, dn_w_out, nsa_w_kv, nsa_cmp_pos_w, nsa_w_cmp, nsa_w_in, nsa_w_out, rel_bias):
    B, T, D = x_prompt.shape
    NB, TS, _ = x_sample.shape
    Mp, Ms = B * T, NB * TS
    past = page_table.shape[1] * PAGE
    x = jnp.concatenate([x_prompt.reshape(Mp, D), x_sample.reshape(Ms, D)], axis=0)

    p_S, p_conv, s_S, s_conv = [], [], [], []
    for l in range(N_A_LAYERS):
        w_in = _pad_cols(dn_w_in[l], 4 * D + LANES).astype(BF16)
        proj = linear(x, w_in, norm_w=norm_mix[l], tn=(4 * D + LANES) // 3)
        o_p, S_p = dn_prompt(proj, B, T, dn_conv_w[l], dn_A_log[l], dn_dt_bias[l], dn_out_norm[l])
        o_s, S_s = dn_decode(proj[Mp:], NB, state_dn_conv[l], state_dn_S[l], dn_conv_w[l], dn_A_log[l],
                             dn_dt_bias[l], dn_out_norm[l])
        qkv_p = proj[:Mp, :DN_QKV].reshape(B, T, DN_QKV)
        qkv_s = proj[Mp:, :DN_QKV].reshape(NB, TS, DN_QKV)
        p_S.append(S_p)
        s_S.append(S_s)
        p_conv.append(qkv_p[:, T - (DN_CONV - 1):])
        s_conv.append(jnp.concatenate([state_dn_conv[l], qkv_s], axis=1)[:, TS:])
        o = jnp.concatenate([o_p, o_s], axis=0)
        x = linear(o, dn_w_out[l].astype(BF16), residual=x, tn=D)
        x = ffn(x, norm_ffn[l], ffn_w_in[l].astype(BF16), ffn_w_out[l].astype(BF16))

    W = NSA_KVH * NSA_HD
    rows = linear(x, nsa_w_kv.astype(BF16), norm_w=norm_kv, tn=6 * W)
    rows_p = rows[:Mp].reshape(B, T, 6, NSA_KVH, NSA_HD)
    rows_s = rows[Mp:].reshape(NB, TS, 6, NSA_KVH, NSA_HD)
    p_kv_rows = rows_p[:, :, :4]
    p_win_kv = rows_p[:, T - min(WINDOW, T):, 4:]
    s_kv_rows = rows_s[:, :, :4]
    s_win_kv = jnp.concatenate([state_win_kv, rows_s[:, :, 4:]], axis=1)[:, TS:]

    wlo, whi, wbd = _cmp_weights(nsa_cmp_pos_w, nsa_w_cmp)
    kv_h = rows_p.astype(BF16).transpose(2, 0, 3, 1, 4)
    cmp_p = compress_prompt(rows, B, T, wlo, whi, wbd)
    cmp_h = cmp_p.reshape(B, T // CMP_STRIDE, 2, NSA_KVH, NSA_HD).transpose(2, 0, 3, 1, 4)
    new_rows = rows[Mp:].reshape(NB, TS, 6 * W)
    cache2 = cache_kv.reshape(cache_kv.shape[0], PAGE, 4 * W)
    cmp_d = compress_decode(cache2, page_table, new_rows, wlo, whi, wbd)
    win_prev = state_win_kv.reshape(NB, state_win_kv.shape[1], 2 * W)
    bias_c = bias_cmp_prompt(rel_bias, T)
    band = bias_band(rel_bias)
    bias_d = bias_decode(rel_bias, past, TS)

    for j in range(N_B_LAYERS):
        l = N_A_LAYERS + j
        w_in = _pad_cols(nsa_w_in[j], D + LANES).astype(BF16)
        proj = linear(x, w_in, norm_w=norm_mix[l], tn=D + LANES)
        o_p = attn_prompt(proj, B, T, kv_h, cmp_h, bias_c, band)
        o_s = attn_decode(proj[Mp:], cache2, page_table, new_rows, win_prev, cmp_d, bias_d)
        o = jnp.concatenate([o_p, o_s], axis=0)
        x = linear(o, nsa_w_out[j].astype(BF16), residual=x, tn=D)
        x = ffn(x, norm_ffn[l], ffn_w_in[l].astype(BF16), ffn_w_out[l].astype(BF16))

    y = final_norm(x, norm_final)
    return (y[:Mp].reshape(B, T, D), y[Mp:].reshape(NB, TS, D),
            jnp.stack(p_S), jnp.stack(p_conv), p_kv_rows, p_win_kv,
            jnp.stack(s_S), jnp.stack(s_conv), s_kv_rows, s_win_kv)
```

```python
import functools
import math

import jax
import jax.numpy as jnp
from jax import lax
from jax.experimental import pallas as pl
from jax.experimental.pallas import tpu as pltpu

F32 = jnp.float32
BF16 = jnp.bfloat16

D_MODEL = 1024
N_A_LAYERS = 2
N_B_LAYERS = 2
NORM_EPS = 1e-6
DN_HEADS = 8
DN_HEAD = 128
DN_QKV = 3 * D_MODEL
DN_CONV = 4
DN_CHUNK = 64
NSA_HEADS = 16
NSA_HD = 64
NSA_KVH = 4
NSA_GROUP = 4
CMP_STRIDE = 16
CMP_BLOCK = 32
SEL_BLOCK = 64
N_SEL = 16
WINDOW = 512
PAGE = 128
REL_BUCKETS = 32
REL_MAX_EXACT = 16
REL_MAX_DIST = 1024
NEG = -1e30

V7X_VMEM_LIMIT = 56 * 1024 * 1024
LANES = 128


def _cparams(sem):
    return pltpu.CompilerParams(dimension_semantics=sem, vmem_limit_bytes=V7X_VMEM_LIMIT)


def _rms(x, w):
    ms = jnp.mean(x * x, axis=-1, keepdims=True)
    return x * lax.rsqrt(ms + NORM_EPS) * w


def _silu(x):
    return x * (1.0 / (1.0 + jnp.exp(-x)))


def _sigmoid(x):
    return 1.0 / (1.0 + jnp.exp(-x))


def _softplus(x):
    return jnp.maximum(x, 0.0) + jnp.log1p(jnp.exp(-jnp.abs(x)))


def _dot(a, b):
    return jnp.dot(a.astype(BF16), b.astype(BF16), preferred_element_type=F32)


def _dot_nt(a, b):
    return lax.dot_general(a.astype(BF16), b.astype(BF16), (((1,), (1,)), ((), ())),
                           preferred_element_type=F32)


def _bdot_dims(a, b, ca, cb):
    return lax.dot_general(a.astype(BF16), b.astype(BF16), (((ca,), (cb,)), ((0,), (0,))),
                           preferred_element_type=F32)


def _bdot(a, b):
    return _bdot_dims(a, b, 2, 1)


def _bdot_nt(a, b):
    return _bdot_dims(a, b, 2, 2)


def _bdot_tn(a, b):
    return _bdot_dims(a, b, 1, 1)


def _linear_body(*refs, has_norm, has_res):
    it = iter(refs)
    x_ref = next(it)
    nw_ref = next(it) if has_norm else None
    w_ref = next(it)
    res_ref = next(it) if has_res else None
    o_ref = next(it)
    xn_ref = next(it)

    @pl.when(pl.program_id(1) == 0)
    def _():
        x = x_ref[...]
        if has_norm:
            x = _rms(x, nw_ref[...])
        xn_ref[...] = x.astype(BF16)

    acc = jnp.dot(xn_ref[...], w_ref[...], preferred_element_type=F32)
    if has_res:
        acc = acc + res_ref[...]
    o_ref[...] = acc


def linear(x, w, norm_w=None, residual=None, tm=512, tn=None):
    M, K = x.shape
    N = w.shape[1]
    tn = N if tn is None else tn
    assert M % tm == 0 and N % tn == 0
    has_norm, has_res = norm_w is not None, residual is not None
    args, specs = [x], [pl.BlockSpec((tm, K), lambda i, j: (i, 0))]
    if has_norm:
        args.append(norm_w.reshape(1, K))
        specs.append(pl.BlockSpec((1, K), lambda i, j: (0, 0)))
    args.append(w)
    specs.append(pl.BlockSpec((K, tn), lambda i, j: (0, j)))
    if has_res:
        args.append(residual)
        specs.append(pl.BlockSpec((tm, tn), lambda i, j: (i, j)))
    return pl.pallas_call(
        functools.partial(_linear_body, has_norm=has_norm, has_res=has_res),
        out_shape=jax.ShapeDtypeStruct((M, N), F32),
        grid=(M // tm, N // tn),
        in_specs=specs,
        out_specs=pl.BlockSpec((tm, tn), lambda i, j: (i, j)),
        scratch_shapes=[pltpu.VMEM((tm, K), BF16)],
        compiler_params=_cparams(("parallel", "arbitrary")),
        name="linear",
    )(*args)


def _ffn_body(x_ref, nw_ref, wg_ref, wu_ref, wo_ref, o_ref, xn_ref, acc_ref):
    f = pl.program_id(1)

    @pl.when(f == 0)
    def _():
        xn_ref[...] = _rms(x_ref[...], nw_ref[...]).astype(BF16)
        acc_ref[...] = jnp.zeros_like(acc_ref)

    xn = xn_ref[...]
    g = jnp.dot(xn, wg_ref[...], preferred_element_type=F32)
    u = jnp.dot(xn, wu_ref[...], preferred_element_type=F32)
    a = (_silu(g) * u).astype(BF16)
    acc_ref[...] += jnp.dot(a, wo_ref[...], preferred_element_type=F32)

    @pl.when(f == pl.num_programs(1) - 1)
    def _():
        o_ref[...] = x_ref[...] + acc_ref[...]


def ffn(x, norm_w, w_in, w_out, tm=512, tf=256):
    M, D = x.shape
    FF = w_out.shape[0]
    nf = FF // tf
    assert M % tm == 0 and FF % tf == 0
    return pl.pallas_call(
        _ffn_body,
        out_shape=jax.ShapeDtypeStruct((M, D), F32),
        grid=(M // tm, nf),
        in_specs=[
            pl.BlockSpec((tm, D), lambda i, f: (i, 0)),
            pl.BlockSpec((1, D), lambda i, f: (0, 0)),
            pl.BlockSpec((D, tf), lambda i, f: (0, f)),
            pl.BlockSpec((D, tf), lambda i, f: (0, f + nf)),
            pl.BlockSpec((tf, D), lambda i, f: (f, 0)),
        ],
        out_specs=pl.BlockSpec((tm, D), lambda i, f: (i, 0)),
        scratch_shapes=[pltpu.VMEM((tm, D), BF16), pltpu.VMEM((tm, D), F32)],
        compiler_params=_cparams(("parallel", "arbitrary")),
        name="ffn",
    )(x, norm_w.reshape(1, D), w_in, w_in, w_out)


def _final_norm_body(x_ref, w_ref, o_ref):
    o_ref[...] = _rms(x_ref[...], w_ref[...])


def final_norm(x, w, tm=512):
    M, D = x.shape
    return pl.pallas_call(
        _final_norm_body,
        out_shape=jax.ShapeDtypeStruct((M, D), F32),
        grid=(M // tm,),
        in_specs=[pl.BlockSpec((tm, D), lambda i: (i, 0)), pl.BlockSpec((1, D), lambda i: (0, 0))],
        out_specs=pl.BlockSpec((tm, D), lambda i: (i, 0)),
        compiler_params=_cparams(("parallel",)),
        name="final_norm",
    )(x, w.reshape(1, D))


def _l2n(x):
    return x * lax.rsqrt(jnp.sum(x * x, axis=-1, keepdims=True) + NORM_EPS)


def _gated_out(o, z, onorm):
    return _rms(o, onorm) * _silu(z)


def _dn_prompt_body(qkv_ref, z_ref, ab_ref, abT_ref, cw_ref, alr_ref, dtr_ref, alc_ref, dtc_ref, on_ref,
                    o_ref, s_out_ref, xbuf_ref, s_ref):
    n = pl.program_id(1)
    C = DN_CHUNK

    @pl.when(n == 0)
    def _():
        xbuf_ref[0:8, :] = jnp.zeros((8, DN_QKV), F32)
        s_ref[...] = jnp.zeros_like(s_ref)

    xbuf_ref[8:8 + C, :] = qkv_ref[...]
    y = xbuf_ref[5:5 + C, :] * cw_ref[0:1, :]
    for i in range(1, DN_CONV):
        y = y + xbuf_ref[5 + i:5 + i + C, :] * cw_ref[i:i + 1, :]
    xbuf_ref[0:8, :] = xbuf_ref[C:C + 8, :]
    y = _silu(y)

    ab = ab_ref[...]
    g8 = -jnp.exp(alr_ref[...]) * _softplus(ab[:, 0:DN_HEADS] + dtr_ref[...])
    beta8 = _sigmoid(ab[:, DN_HEADS:2 * DN_HEADS])
    abT = abT_ref[0, 0]
    g8T = -jnp.exp(alc_ref[...]) * _softplus(abT[0:DN_HEADS, :] + dtc_ref[...])

    ii = lax.broadcasted_iota(jnp.int32, (C, C), 0)
    jj = lax.broadcasted_iota(jnp.int32, (C, C), 1)
    incl = (ii >= jj)[None]
    strict = (ii > jj)[None]
    onorm = on_ref[...]
    H = DN_HEADS

    heads = lambda off: jnp.stack([y[:, off + h * DN_HEAD:off + (h + 1) * DN_HEAD] for h in range(H)], axis=0)
    q = _l2n(heads(0)) * (DN_HEAD ** -0.5)
    k = _l2n(heads(D_MODEL))
    v = heads(2 * D_MODEL)
    g_col = jnp.stack([g8[:, h:h + 1] for h in range(H)], axis=0)
    beta = jnp.stack([beta8[:, h:h + 1] for h in range(H)], axis=0)
    g_row = jnp.stack([g8T[h:h + 1, :] for h in range(H)], axis=0)
    G_col = jnp.sum(jnp.where(incl, g_row, 0.0), axis=2, keepdims=True)
    G_row = jnp.sum(jnp.where((ii <= jj)[None], g_col, 0.0), axis=1, keepdims=True)
    dec = jnp.where(incl, jnp.exp(jnp.where(incl, G_col - G_row, 0.0)), 0.0)
    A = jnp.where(strict, beta * dec * _bdot_nt(k, k), 0.0)
    X = -A
    Tm = X
    for _ in range(int(math.log2(C)) - 1):
        X = _bdot(X, X)
        Tm = Tm + X + _bdot(X, Tm)
    eG = jnp.exp(G_col)
    w = jnp.concatenate([beta * v, (beta * eG) * k], axis=2)
    w = w + _bdot(Tm, w)
    wv, wk = w[:, :, :DN_HEAD], w[:, :, DN_HEAD:]
    aqk = dec * _bdot_nt(q, k)
    qg = eG * q
    G_last = G_col[:, C - 1:C, :]
    kdec = jnp.exp(G_last - G_col) * k
    S = s_ref[...]
    ws = _bdot(jnp.concatenate([wk, qg], axis=1), S)
    U = wv - ws[:, :C]
    O = ws[:, C:] + _bdot(aqk, U)
    s_ref[...] = jnp.exp(G_last) * S + _bdot_tn(kdec, U)
    for h in range(H):
        sl = slice(h * DN_HEAD, (h + 1) * DN_HEAD)
        o_ref[:, sl] = _gated_out(O[h], z_ref[:, sl], onorm)

    @pl.when(n == pl.num_programs(1) - 1)
    def _():
        s_out_ref[0] = s_ref[...]


def dn_prompt(proj, n_seq, T, conv_w, a_log, dt_bias, out_norm):
    M = n_seq * T
    C = DN_CHUNK
    N = T // C
    abT = proj[:M, 4 * D_MODEL:4 * D_MODEL + 2 * DN_HEADS].reshape(n_seq, N, C, 2 * DN_HEADS).transpose(0, 1, 3, 2)
    row = lambda a: a.reshape(1, DN_HEADS)
    col = lambda a: a.reshape(DN_HEADS, 1)
    return pl.pallas_call(
        _dn_prompt_body,
        out_shape=[jax.ShapeDtypeStruct((M, D_MODEL), F32),
                   jax.ShapeDtypeStruct((n_seq, DN_HEADS, DN_HEAD, DN_HEAD), F32)],
        grid=(n_seq, N),
        in_specs=[
            pl.BlockSpec((C, DN_QKV), lambda b, n: (b * N + n, 0)),
            pl.BlockSpec((C, D_MODEL), lambda b, n: (b * N + n, 3)),
            pl.BlockSpec((C, LANES), lambda b, n: (b * N + n, 4 * D_MODEL // LANES)),
            pl.BlockSpec((1, 1, 2 * DN_HEADS, C), lambda b, n: (b, n, 0, 0)),
            pl.BlockSpec((DN_CONV, DN_QKV), lambda b, n: (0, 0)),
            pl.BlockSpec((1, DN_HEADS), lambda b, n: (0, 0)),
            pl.BlockSpec((1, DN_HEADS), lambda b, n: (0, 0)),
            pl.BlockSpec((DN_HEADS, 1), lambda b, n: (0, 0)),
            pl.BlockSpec((DN_HEADS, 1), lambda b, n: (0, 0)),
            pl.BlockSpec((1, DN_HEAD), lambda b, n: (0, 0)),
        ],
        out_specs=[pl.BlockSpec((C, D_MODEL), lambda b, n: (b * N + n, 0)),
                   pl.BlockSpec((1, DN_HEADS, DN_HEAD, DN_HEAD), lambda b, n: (b, 0, 0, 0))],
        scratch_shapes=[pltpu.VMEM((C + 8, DN_QKV), F32), pltpu.VMEM((DN_HEADS, DN_HEAD, DN_HEAD), F32)],
        compiler_params=_cparams(("parallel", "arbitrary")),
        name="dn_prompt",
    )(proj, proj, proj, abT, conv_w, row(a_log), row(dt_bias), col(a_log), col(dt_bias), out_norm.reshape(1, DN_HEAD))


def _dn_decode_body(proj_ref, cbuf_ref, s0_ref, cw_ref, alr_ref, dtr_ref, on_ref, o_ref, s_out_ref, xbuf_ref, oacc_ref):
    T = proj_ref.shape[1]
    x = proj_ref[0]
    xbuf_ref[8 - (DN_CONV - 1):8, :] = cbuf_ref[0]
    xbuf_ref[8:8 + T, :] = x[:, :DN_QKV]
    y = xbuf_ref[5:5 + T, :] * cw_ref[0:1, :]
    for i in range(1, DN_CONV):
        y = y + xbuf_ref[5 + i:5 + i + T, :] * cw_ref[i:i + 1, :]
    y = _silu(y)
    ab = x[:, 4 * D_MODEL:4 * D_MODEL + LANES]
    a8 = jnp.exp(-jnp.exp(alr_ref[...]) * _softplus(ab[:, 0:DN_HEADS] + dtr_ref[...]))
    beta8 = _sigmoid(ab[:, DN_HEADS:2 * DN_HEADS])
    ii = lax.broadcasted_iota(jnp.int32, (DN_HEAD, DN_HEAD), 0)
    jj = lax.broadcasted_iota(jnp.int32, (DN_HEAD, DN_HEAD), 1)
    eye = ii == jj

    def to_col(r):
        return jnp.sum(jnp.where(eye, r, 0.0), axis=1, keepdims=True)

    for h in range(DN_HEADS):
        sl = slice(h * DN_HEAD, (h + 1) * DN_HEAD)
        q = _l2n(y[:, sl]) * (DN_HEAD ** -0.5)
        k = _l2n(y[:, D_MODEL + h * DN_HEAD:D_MODEL + (h + 1) * DN_HEAD])
        v = y[:, 2 * D_MODEL + h * DN_HEAD:2 * D_MODEL + (h + 1) * DN_HEAD]
        S = s0_ref[0, h]
        for t in range(T):
            k_col = to_col(k[t:t + 1, :])
            q_col = to_col(q[t:t + 1, :])
            a = a8[t:t + 1, h:h + 1]
            b = beta8[t:t + 1, h:h + 1]
            kS = jnp.sum(k_col * S, axis=0, keepdims=True)
            S = a * S + k_col * (b * (v[t:t + 1, :] - a * kS))
            oacc_ref[t:t + 1, sl] = jnp.sum(q_col * S, axis=0, keepdims=True)
        s_out_ref[0, h] = S
    onorm = on_ref[...]
    for h in range(DN_HEADS):
        sl = slice(h * DN_HEAD, (h + 1) * DN_HEAD)
        o_ref[0, :, sl] = _gated_out(oacc_ref[0:T, sl], x[:, DN_QKV + h * DN_HEAD:DN_QKV + (h + 1) * DN_HEAD], onorm)


def dn_decode(proj, n_seq, conv_buf, S0, conv_w, a_log, dt_bias, out_norm):
    M, W = proj.shape
    T = M // n_seq
    row = lambda a: a.reshape(1, DN_HEADS)
    o, S = pl.pallas_call(
        _dn_decode_body,
        out_shape=[jax.ShapeDtypeStruct((n_seq, T, D_MODEL), F32),
                   jax.ShapeDtypeStruct((n_seq, DN_HEADS, DN_HEAD, DN_HEAD), F32)],
        grid=(n_seq,),
        in_specs=[
            pl.BlockSpec((1, T, W), lambda b: (b, 0, 0)),
            pl.BlockSpec((1, DN_CONV - 1, DN_QKV), lambda b: (b, 0, 0)),
            pl.BlockSpec((1, DN_HEADS, DN_HEAD, DN_HEAD), lambda b: (b, 0, 0, 0)),
            pl.BlockSpec((DN_CONV, DN_QKV), lambda b: (0, 0)),
            pl.BlockSpec((1, DN_HEADS), lambda b: (0, 0)),
            pl.BlockSpec((1, DN_HEADS), lambda b: (0, 0)),
            pl.BlockSpec((1, DN_HEAD), lambda b: (0, 0)),
        ],
        out_specs=[pl.BlockSpec((1, T, D_MODEL), lambda b: (b, 0, 0)),
                   pl.BlockSpec((1, DN_HEADS, DN_HEAD, DN_HEAD), lambda b: (b, 0, 0, 0))],
        scratch_shapes=[pltpu.VMEM((16, DN_QKV), F32), pltpu.VMEM((8, D_MODEL), F32)],
        compiler_params=_cparams(("parallel",)),
        name="dn_decode",
    )(proj.reshape(n_seq, T, W), conv_buf, S0, conv_w, row(a_log), row(dt_bias), out_norm.reshape(1, DN_HEAD))
    return o.reshape(M, D_MODEL), S


def _bucket_thresholds():
    thr, prev = [], REL_MAX_EXACT
    for d in range(REL_MAX_EXACT, REL_MAX_DIST + 1):
        val = min(REL_MAX_EXACT + int(math.log(d / REL_MAX_EXACT) / math.log(REL_MAX_DIST / REL_MAX_EXACT)
                                      * (REL_BUCKETS - REL_MAX_EXACT)), REL_BUCKETS - 1)
        thr += [d] * (val - prev)
        prev = val
    assert len(thr) == REL_BUCKETS - 1 - REL_MAX_EXACT
    return tuple(thr)


_BUCKET_THR = _bucket_thresholds()
TQ = 128
TK = 512
BAND_TOP = (REL_MAX_DIST + TK + LANES - 1) // LANES * LANES
BAND_W = BAND_TOP + TK


def _bucket(d):
    n = jnp.maximum(d, 0)
    big = jnp.full(n.shape, REL_MAX_EXACT, jnp.int32)
    for t in _BUCKET_THR:
        big = big + (n >= t).astype(jnp.int32)
    return jnp.where(n < REL_MAX_EXACT, n, big)


def _bias_lookup(bucket, table_row):
    acc = jnp.zeros(bucket.shape, F32)
    for k in range(REL_BUCKETS):
        acc = acc + jnp.where(bucket == k, table_row(k), 0.0)
    return acc


def _bias_cmp_body(tab_ref, o_ref, *, n_cmp):
    q0 = pl.program_id(0) * TQ
    shp = o_ref.shape[1:]
    t = q0 + lax.broadcasted_iota(jnp.int32, shp, 0)
    j = lax.broadcasted_iota(jnp.int32, shp, 1)
    d = t - (j * CMP_STRIDE + CMP_BLOCK - 1)
    dead = (d < 0) | (j >= n_cmp)
    bucket = _bucket(d)
    for h in range(NSA_HEADS):
        o_ref[h] = jnp.where(dead, NEG, _bias_lookup(bucket, lambda k: tab_ref[k, h]))


def bias_cmp_prompt(rel_bias, T):
    n_sub = T // CMP_STRIDE
    return pl.pallas_call(
        functools.partial(_bias_cmp_body, n_cmp=n_sub - 1),
        out_shape=jax.ShapeDtypeStruct((NSA_HEADS, T, n_sub), F32),
        grid=(T // TQ,),
        in_specs=[pl.BlockSpec(memory_space=pltpu.SMEM)],
        out_specs=pl.BlockSpec((NSA_HEADS, TQ, n_sub), lambda i: (0, i, 0)),
        compiler_params=_cparams(("parallel",)),
        name="bias_cmp",
    )(rel_bias)


def _bias_band_body(tab_ref, o_ref, ow_ref):
    h = pl.program_id(0)
    shp = o_ref.shape[1:]
    d = BAND_TOP + lax.broadcasted_iota(jnp.int32, shp, 0) - lax.broadcasted_iota(jnp.int32, shp, 1)
    bias = _bias_lookup(_bucket(d), lambda k: tab_ref[k, h])
    o_ref[0] = jnp.where(d < 0, NEG, bias)
    ow_ref[0] = jnp.where((d < 0) | (d >= WINDOW), NEG, bias)


def bias_band(rel_bias):
    shape = jax.ShapeDtypeStruct((NSA_HEADS, TQ, BAND_W), F32)
    spec = pl.BlockSpec((1, TQ, BAND_W), lambda h: (h, 0, 0))
    return pl.pallas_call(
        _bias_band_body,
        out_shape=[shape, shape],
        grid=(NSA_HEADS,),
        in_specs=[pl.BlockSpec(memory_space=pltpu.SMEM)],
        out_specs=[spec, spec],
        compiler_params=_cparams(("parallel",)),
        name="bias_band",
    )(rel_bias)


def _bias_decode_body(tabc_ref, cmp_ref, sel_ref, win_ref, *, past, tq):
    shp = (PAGE, NSA_HEADS * tq)
    i = lax.broadcasted_iota(jnp.int32, shp, 0)
    c = lax.broadcasted_iota(jnp.int32, shp, 1)
    qpos = past + (c & (tq - 1))
    row = lambda k: tabc_ref[k:k + 1, :]

    def table(d, dead):
        return jnp.where(dead | (d < 0), NEG, _bias_lookup(_bucket(d), row))

    ic = lax.broadcasted_iota(jnp.int32, cmp_ref.shape, 0)
    qc = past + (lax.broadcasted_iota(jnp.int32, cmp_ref.shape, 1) & (tq - 1))
    cmp_ref[...] = table(qc - (ic * CMP_STRIDE + CMP_BLOCK - 1), ic < 0)
    n_pages = past // PAGE
    for p in range(n_pages):
        sel_ref[p] = table(qpos - (p * PAGE + i), i < 0)
    sel_ref[n_pages] = table(qpos - (past + i), i >= tq)
    n_wt = WINDOW // PAGE
    for t in range(n_wt):
        d = qpos - (past - WINDOW + t * PAGE + i)
        win_ref[t] = table(d, d >= WINDOW)
    win_ref[n_wt] = table(qpos - (past + i), i >= tq)


def bias_decode(rel_bias, past, tq):
    assert tq & (tq - 1) == 0
    tabc = jnp.repeat(rel_bias, tq, axis=1)
    n_pages = past // PAGE
    nc = NSA_HEADS * tq
    return pl.pallas_call(
        functools.partial(_bias_decode_body, past=past, tq=tq),
        out_shape=[jax.ShapeDtypeStruct((past // CMP_STRIDE, nc), F32),
                   jax.ShapeDtypeStruct((n_pages + 1, PAGE, nc), F32),
                   jax.ShapeDtypeStruct((WINDOW // PAGE + 1, PAGE, nc), F32)],
        name="bias_decode",
    )(tabc)


def _cmp_weights(cmp_pos_w, w_cmp):
    w = jnp.concatenate([cmp_pos_w[0].reshape(CMP_BLOCK, -1), cmp_pos_w[1].reshape(CMP_BLOCK, -1)], axis=1)
    blocks = w_cmp.reshape(2 * NSA_KVH, NSA_HD, NSA_HD)
    n = 2 * NSA_KVH
    wbd = (jnp.eye(n, dtype=F32)[:, None, :, None] * blocks[:, :, None, :]).reshape(n * NSA_HD, n * NSA_HD)
    return w[:CMP_STRIDE], w[CMP_STRIDE:], wbd.astype(BF16)


def _pool16(x, w):
    n = x.shape[0] // CMP_STRIDE
    return jnp.sum(x.reshape(n, CMP_STRIDE, x.shape[1]) * w[None], axis=1)


def _compress_prompt_body(x_ref, wlo_ref, whi_ref, wbd_ref, o_ref, lo_ref, hi_ref):
    T = x_ref.shape[0]
    n_sub = T // CMP_STRIDE
    step = 512
    for c in range(T // step):
        xs = x_ref[c * step:(c + 1) * step, :]
        r = slice(c * step // CMP_STRIDE, (c + 1) * step // CMP_STRIDE)
        lo_ref[r, :] = _pool16(xs, wlo_ref[...])
        hi_ref[r, :] = _pool16(xs, whi_ref[...])
    hi_ref[n_sub:n_sub + 8, :] = jnp.zeros((8, hi_ref.shape[1]), F32)
    blocks = lo_ref[...] + hi_ref[1:n_sub + 1, :]
    o_ref[...] = _dot(blocks, wbd_ref[...]).astype(BF16)


def compress_prompt(rows, n_seq, T, wlo, whi, wbd):
    n_sub = T // CMP_STRIDE
    W = 2 * NSA_KVH * NSA_HD
    return pl.pallas_call(
        _compress_prompt_body,
        out_shape=jax.ShapeDtypeStruct((n_seq * n_sub, W), BF16),
        grid=(n_seq,),
        in_specs=[pl.BlockSpec((T, W), lambda b: (b, 0)),
                  pl.BlockSpec((CMP_STRIDE, W), lambda b: (0, 0)),
                  pl.BlockSpec((CMP_STRIDE, W), lambda b: (0, 0)),
                  pl.BlockSpec((W, W), lambda b: (0, 0))],
        out_specs=pl.BlockSpec((n_sub, W), lambda b: (b, 0)),
        scratch_shapes=[pltpu.VMEM((n_sub, W), F32), pltpu.VMEM((n_sub + 8, W), F32)],
        compiler_params=_cparams(("parallel",)),
        name="compress_prompt",
    )(rows, wlo, whi, wbd)


def _masked_softmax(s, mask, axis):
    l = jnp.where(mask, s, NEG)
    m = jnp.max(l, axis=axis, keepdims=True)
    e = jnp.where(mask, jnp.exp(l - m), 0.0)
    return e / jnp.maximum(jnp.sum(e, axis=axis, keepdims=True), 1e-30)


def _split3(x):
    hi = x.astype(BF16)
    r = x - hi.astype(F32)
    mid = r.astype(BF16)
    lo = (r - mid.astype(F32)).astype(BF16)
    return hi, mid, lo


def _topk_mask(score, blk, n_pick, axis, removed):
    sel = jnp.zeros(score.shape, F32)
    s = score
    big = jnp.int32(1 << 20)
    for _ in range(n_pick):
        mx = jnp.max(s, axis=axis, keepdims=True)
        idx = jnp.min(jnp.where(s == mx, blk, big), axis=axis, keepdims=True)
        hit = blk == idx
        sel = jnp.where(hit, 1.0, sel)
        s = jnp.where(hit, removed, s)
    return sel


def _attn_prompt_body(q_ref, g_ref, ks_ref, vs_ref, kw_ref, vw_ref, kc_ref, vc_ref, bc_ref, band_ref, bandw_ref,
                      o_ref, qp_scr, s_scr, p_scr, m_scr, acc_scr, sc_scr):
    par = pl.program_id(0) % 2
    qb = pl.program_id(2)
    q0 = qb * TQ
    G = NSA_GROUP
    R = G * TQ
    HD = NSA_HD
    qblk = q_ref[...] * (HD ** -0.5)
    Q = jnp.concatenate([qblk[:, g * HD:(g + 1) * HD] for g in range(G)], axis=0)
    lane_half = lax.broadcasted_iota(jnp.int32, (R, 2 * HD), 1) // HD
    qp_scr[...] = jnp.where(lane_half == par, jnp.concatenate([Q, Q], axis=1), 0.0).astype(BF16)

    def own_half(x):
        return jnp.where(par == 0, x[:, :HD], x[:, HD:])

    n_sub = bc_ref.shape[2]
    bc = bc_ref[...].reshape(R, n_sub)
    p_c = _masked_softmax(_dot_nt(qp_scr[...], kc_ref[...]) + bc, bc > 0.5 * NEG, 1)
    o_cmp = own_half(_dot(p_c, vc_ref[...]))
    imp = p_c[0:TQ]
    for g in range(1, G):
        imp = imp + p_c[g * TQ:(g + 1) * TQ]

    n_blk = n_sub * CMP_STRIDE // SEL_BLOCK
    n_pick = min(N_SEL, n_blk)
    per = SEL_BLOCK // CMP_STRIDE
    bb = lax.broadcasted_iota(jnp.int32, (n_blk, n_sub), 0)
    mm = lax.broadcasted_iota(jnp.int32, (n_blk, n_sub), 1)
    pool = ((mm // per == bb).astype(F32) + ((mm + 1) // per == bb).astype(F32)).astype(BF16)
    p_slc = sum(_dot_nt(pool, part) for part in _split3(imp))
    blk = lax.broadcasted_iota(jnp.int32, (n_blk, TQ), 0)
    t = q0 + lax.broadcasted_iota(jnp.int32, (n_blk, TQ), 1)
    cur = t // SEL_BLOCK
    forced = (blk == 0) | (blk == cur) | (blk == cur - 1)
    score = jnp.where(forced, 1e4, jnp.where(blk * SEL_BLOCK <= t, p_slc, -1.0))
    sc_scr[...] = score
    beaten = jnp.zeros((n_blk, TQ), F32)
    for other in range(n_blk):
        row = sc_scr[other:other + 1, :]
        ge = jnp.where(row >= score, 1.0, 0.0)
        gt = jnp.where(row > score, 1.0, 0.0)
        beaten = beaten + jnp.where(blk > other, ge, gt)
    selneg = jnp.where(beaten < n_pick, 0.0, NEG).astype(BF16)

    eb = lax.broadcasted_iota(jnp.int32, (n_blk, TK), 0)
    ej = lax.broadcasted_iota(jnp.int32, (n_blk, TK), 1) // SEL_BLOCK
    tn = (((0,), (0,)), ((), ()))

    other_half = (lax.broadcasted_iota(jnp.int32, (TK, 2 * HD), 1) // HD) != par
    slabs = [slice(g * TQ, (g + 1) * TQ) for g in range(G)]

    def flash(k_ref, v_ref, bnd_ref, lo, hi, selected):
        m_scr[...] = jnp.full(m_scr.shape, 0.5 * NEG, F32)
        acc_scr[...] = jnp.zeros(acc_scr.shape, F32)

        def step(kt, carry):
            k0 = pl.multiple_of(kt * TK, TK)
            c0 = pl.multiple_of(jnp.maximum(BAND_TOP - (q0 - k0), 0), LANES)
            s_scr[...] = _dot_nt(qp_scr[...], k_ref[pl.ds(k0, TK), :])
            v1 = jnp.where(other_half, 1.0, v_ref[pl.ds(k0, TK), :].astype(F32)).astype(BF16)
            if selected:
                expand = (eb == ej + kt * (TK // SEL_BLOCK)).astype(BF16)
                sn = lax.dot_general(selneg, expand, tn, preferred_element_type=F32)
            rowmax = []
            for g, r in enumerate(slabs):
                s = s_scr[r, :] + bnd_ref[g, :, pl.ds(c0, TK)]
                if selected:
                    s = s + sn
                s_scr[r, :] = s
                rowmax.append(jnp.max(s, axis=1, keepdims=True))
            for g, r in enumerate(slabs):
                m_old = m_scr[r, :]
                m_new = jnp.maximum(m_old, rowmax[g])
                m_scr[r, :] = m_new
                acc_scr[r, :] = jnp.exp(m_old - m_new) * acc_scr[r, :]
                for c in range(TK // LANES):
                    cs = slice(c * LANES, (c + 1) * LANES)
                    p_scr[r, cs] = jnp.exp(s_scr[r, cs] - m_new).astype(BF16)
            acc_scr[...] += jnp.dot(p_scr[...], v1, preferred_element_type=F32)
            return carry

        lax.fori_loop(lo, hi, step, 0)
        acc = acc_scr[...]
        den = jnp.where(par == 0, acc[:, HD:HD + 1], acc[:, 0:1])
        return own_half(acc) / jnp.maximum(den, 1e-30)

    o_sel = flash(ks_ref, vs_ref, band_ref, 0, (q0 + TQ + TK - 1) // TK, True)
    o_win = flash(kw_ref, vw_ref, bandw_ref, jnp.maximum(q0 - WINDOW + 1, 0) // TK, (q0 + TQ - 1) // TK + 1, False)

    gt = _sigmoid(g_ref[0])
    outs = []
    for g in range(G):
        r = slice(g * TQ, (g + 1) * TQ)
        outs.append(o_cmp[r] * gt[:, 3 * g:3 * g + 1] + o_sel[r] * gt[:, 3 * g + 1:3 * g + 2]
                    + o_win[r] * gt[:, 3 * g + 2:3 * g + 3])
    o_ref[...] = jnp.concatenate(outs, axis=1)


def attn_prompt(proj, n_seq, T, rows_bf, cmp_p, bias_c, bands):
    M = n_seq * T
    NQ = T // TQ
    n_sub = T // CMP_STRIDE
    n_blk = T // SEL_BLOCK
    gl = proj[:M, NSA_HEADS * NSA_HD:NSA_HEADS * NSA_HD + 3 * NSA_HEADS]
    gl = gl.reshape(M, NSA_KVH, 3 * NSA_GROUP).transpose(1, 0, 2)
    pair = 2 * NSA_HD
    per_kind = NSA_KVH // 2
    kv_spec = lambda kind: pl.BlockSpec((T, pair), lambda k, b, i: (b, kind * per_kind + k // 2))
    cmp_spec = lambda kind: pl.BlockSpec((n_sub, pair), lambda k, b, i: (b, kind * per_kind + k // 2))
    band_spec = pl.BlockSpec((NSA_GROUP, TQ, BAND_W), lambda k, b, i: (k, 0, 0))
    W = NSA_GROUP * NSA_HD
    R = NSA_GROUP * TQ
    band, bandw = bands
    return pl.pallas_call(
        _attn_prompt_body,
        out_shape=jax.ShapeDtypeStruct((M, NSA_HEADS * NSA_HD), F32),
        grid=(NSA_KVH, n_seq, NQ),
        in_specs=[
            pl.BlockSpec((TQ, W), lambda k, b, i: (b * NQ + i, k)),
            pl.BlockSpec((1, TQ, 3 * NSA_GROUP), lambda k, b, i: (k, b * NQ + i, 0)),
            kv_spec(2), kv_spec(3), kv_spec(4), kv_spec(5),
            cmp_spec(0), cmp_spec(1),
            pl.BlockSpec((NSA_GROUP, TQ, n_sub), lambda k, b, i: (k, i, 0)),
            band_spec, band_spec,
        ],
        out_specs=pl.BlockSpec((TQ, W), lambda k, b, i: (b * NQ + i, k)),
        scratch_shapes=[pltpu.VMEM((R, pair), BF16), pltpu.VMEM((R, TK), F32), pltpu.VMEM((R, TK), BF16),
                        pltpu.VMEM((R, LANES), F32), pltpu.VMEM((R, pair), F32),
                        pltpu.VMEM((n_blk, TQ), F32)],
        compiler_params=_cparams(("parallel", "parallel", "arbitrary")),
        name="attn_prompt",
    )(proj, gl, rows_bf, rows_bf, rows_bf, rows_bf, cmp_p, cmp_p, bias_c, band, bandw)


def _compress_decode_body(pt_ref, *refs, n_pages):
    pages = refs[:n_pages]
    new_ref, wlo_ref, whi_ref, wbd_ref, o_ref, lo_ref, hi_ref = refs[n_pages:]
    per = PAGE // CMP_STRIDE
    for p in range(n_pages):
        x = pages[p][0]
        lo_ref[p * per:(p + 1) * per, :] = _pool16(x, wlo_ref[...])
        hi_ref[p * per:(p + 1) * per, :] = _pool16(x, whi_ref[...])
    n_sub = n_pages * per
    tq = new_ref.shape[1]
    hi_ref[n_sub:n_sub + 8, :] = jnp.zeros((8, hi_ref.shape[1]), F32)
    hi_ref[n_sub:n_sub + 1, :] = jnp.sum(new_ref[0][:, :hi_ref.shape[1]] * whi_ref[0:tq, :], axis=0, keepdims=True)
    blocks = lo_ref[...] + hi_ref[1:n_sub + 1, :]
    o_ref[0] = _dot(blocks, wbd_ref[...]).astype(BF16)


def compress_decode(cache, page_table, new_rows, wlo, whi, wbd):
    nb, n_pages = page_table.shape
    W = 2 * NSA_KVH * NSA_HD
    tq = new_rows.shape[1]
    assert tq <= CMP_STRIDE
    n_sub = n_pages * PAGE // CMP_STRIDE
    page_spec = lambda p: pl.BlockSpec((1, PAGE, W), lambda b, pt: (pt[b, p], 0, 0))
    const = lambda shape: pl.BlockSpec(shape, lambda b, pt: (0,) * len(shape))
    return pl.pallas_call(
        functools.partial(_compress_decode_body, n_pages=n_pages),
        out_shape=jax.ShapeDtypeStruct((nb, n_sub, W), BF16),
        grid_spec=pltpu.PrefetchScalarGridSpec(
            num_scalar_prefetch=1,
            grid=(nb,),
            in_specs=[page_spec(p) for p in range(n_pages)] + [
                pl.BlockSpec((1, tq, new_rows.shape[2]), lambda b, pt: (b, 0, 0)),
                const((CMP_STRIDE, W)), const((CMP_STRIDE, W)), const((W, W))],
            out_specs=pl.BlockSpec((1, n_sub, W), lambda b, pt: (b, 0, 0)),
            scratch_shapes=[pltpu.VMEM((n_sub, W), F32), pltpu.VMEM((n_sub + 8, W), F32)],
        ),
        compiler_params=_cparams(("parallel",)),
        name="compress_decode",
    )(page_table, *([cache] * n_pages), new_rows, wlo, whi, wbd)


def _attn_decode_body(pt_ref, *refs, n_pages, tq):
    pages = refs[:n_pages]
    q_ref, g_ref, new_ref, win_ref, cmp_ref, bcmp_ref, bsel_ref, bwin_ref, o_ref = refs[n_pages:]
    W = NSA_KVH * NSA_HD
    NC = NSA_HEADS * tq
    Qbd = q_ref[0]
    past = n_pages * PAGE

    cm = cmp_ref[0]
    bc = bcmp_ref[...]
    p_c = _masked_softmax(_dot_nt(cm[:, :W], Qbd) + bc, bc > 0.5 * NEG, 0)
    tn = (((0,), (0,)), ((), ()))
    o_cmp = lax.dot_general(cm[:, W:], p_c.astype(BF16), tn, preferred_element_type=F32)

    n_sub = cm.shape[0]
    per = SEL_BLOCK // CMP_STRIDE
    n_blk = past // SEL_BLOCK + 1
    nb_pad = (n_blk + 7) // 8 * 8
    ci = lax.broadcasted_iota(jnp.int32, (NC, NC), 0)
    cj = lax.broadcasted_iota(jnp.int32, (NC, NC), 1)
    gq = NSA_GROUP * tq
    same = ((ci // gq == cj // gq) & ((ci & (tq - 1)) == (cj & (tq - 1)))).astype(BF16)
    imp = sum(jnp.dot(part, same, preferred_element_type=F32) for part in _split3(p_c))
    bb = lax.broadcasted_iota(jnp.int32, (nb_pad, n_sub), 0)
    mm = lax.broadcasted_iota(jnp.int32, (nb_pad, n_sub), 1)
    pool = ((mm // per == bb).astype(F32) + ((mm + 1) // per == bb).astype(F32)).astype(BF16)
    p_slc = sum(jnp.dot(pool, part, preferred_element_type=F32) for part in _split3(imp))
    blk = lax.broadcasted_iota(jnp.int32, (nb_pad, NC), 0)
    qpos = past + (lax.broadcasted_iota(jnp.int32, (nb_pad, NC), 1) & (tq - 1))
    cur = qpos // SEL_BLOCK
    forced = (blk == 0) | (blk == cur) | (blk == cur - 1)
    score = jnp.where(forced, 1e4, jnp.where(blk * SEL_BLOCK <= qpos, p_slc, -1.0))
    score = jnp.where(blk < n_blk, score, -3.0)
    sel = _topk_mask(score, blk, min(N_SEL, n_blk), 0, -5.0)

    def flash_tile(carry, k, v, bias, mask):
        m, l, acc = carry
        s = jnp.where(mask, _dot_nt(k, Qbd) + bias, NEG)
        m_new = jnp.maximum(m, jnp.max(s, axis=0, keepdims=True))
        p = jnp.where(mask, jnp.exp(s - m_new), 0.0)
        alpha = jnp.exp(m - m_new)
        l = alpha * l + jnp.sum(p, axis=0, keepdims=True)
        acc = alpha * acc + lax.dot_general(v.astype(BF16), p.astype(BF16), tn, preferred_element_type=F32)
        return m_new, l, acc

    init = (jnp.full((1, NC), NEG, F32), jnp.zeros((1, NC), F32), jnp.zeros((W, NC), F32))
    new = new_ref[0]
    pad = jnp.zeros((8 - tq, W), F32)
    new_tile = lambda kind: jnp.concatenate([new[:, kind * W:(kind + 1) * W], pad], axis=0)

    half = lax.broadcasted_iota(jnp.int32, (PAGE, NC), 0) < SEL_BLOCK
    carry = init
    for p in range(n_pages):
        pg = pages[p][0]
        b0 = p * (PAGE // SEL_BLOCK)
        picked = jnp.where(half, sel[b0:b0 + 1, :], sel[b0 + 1:b0 + 2, :]) > 0.5
        bias = bsel_ref[p]
        carry = flash_tile(carry, pg[:, :W], pg[:, W:], bias, picked & (bias > 0.5 * NEG))
    bias = bsel_ref[n_pages][0:8]
    carry = flash_tile(carry, new_tile(2), new_tile(3), bias, (sel[n_blk - 1:n_blk, :] > 0.5) & (bias > 0.5 * NEG))
    o_sel = carry[2] / jnp.maximum(carry[1], 1e-30)

    carry = init
    for t in range(WINDOW // PAGE):
        wt = win_ref[0, t * PAGE:(t + 1) * PAGE, :]
        bias = bwin_ref[t]
        carry = flash_tile(carry, wt[:, :W], wt[:, W:], bias, bias > 0.5 * NEG)
    bias = bwin_ref[WINDOW // PAGE][0:8]
    carry = flash_tile(carry, new_tile(4), new_tile(5), bias, bias > 0.5 * NEG)
    o_win = carry[2] / jnp.maximum(carry[1], 1e-30)

    gt = _sigmoid(g_ref[0])
    o = o_cmp * gt[0:1] + o_sel * gt[1:2] + o_win * gt[2:3]
    kvh_of_col = lax.broadcasted_iota(jnp.int32, (NSA_HD, NC), 1) // gq
    out = jnp.zeros((NSA_HD, NC), F32)
    for k in range(NSA_KVH):
        out = out + jnp.where(kvh_of_col == k, o[k * NSA_HD:(k + 1) * NSA_HD, :], 0.0)
    o_ref[0] = out


def attn_decode(proj, cache_sel, page_table, new_rows, win_prev, cmp_d, bias_tabs):
    nb, n_pages = page_table.shape
    tq = new_rows.shape[1]
    W = NSA_KVH * NSA_HD
    NC = NSA_HEADS * tq
    q = proj[:, :NSA_HEADS * NSA_HD] * (NSA_HD ** -0.5)
    q = q.reshape(nb, tq, NSA_KVH, NSA_GROUP, NSA_HD).transpose(0, 2, 3, 1, 4)
    qbd = q[:, :, :, :, None, :] * jnp.eye(NSA_KVH, dtype=F32)[None, :, None, None, :, None]
    qbd = qbd.reshape(nb, NC, W).astype(BF16)
    gl = proj[:, NSA_HEADS * NSA_HD:NSA_HEADS * NSA_HD + 3 * NSA_HEADS]
    gl = gl.reshape(nb, tq, NSA_HEADS, 3).transpose(0, 3, 2, 1).reshape(nb, 3, NC)
    bcmp, bsel, bwin = bias_tabs
    page_spec = lambda p: pl.BlockSpec((1, PAGE, 2 * W), lambda b, pt: (pt[b, p], 0, 1))
    const = lambda shape: pl.BlockSpec(shape, lambda b, pt: (0,) * len(shape))
    per_b = lambda shape: pl.BlockSpec((1,) + shape, lambda b, pt: (b,) + (0,) * len(shape))
    o = pl.pallas_call(
        functools.partial(_attn_decode_body, n_pages=n_pages, tq=tq),
        out_shape=jax.ShapeDtypeStruct((nb, NSA_HD, NC), F32),
        grid_spec=pltpu.PrefetchScalarGridSpec(
            num_scalar_prefetch=1,
            grid=(nb,),
            in_specs=[page_spec(p) for p in range(n_pages)] + [
                per_b((NC, W)), per_b((3, NC)), per_b((tq, 6 * W)), per_b((WINDOW, 2 * W)),
                per_b((cmp_d.shape[1], 2 * W)),
                const(bcmp.shape), const(bsel.shape), const(bwin.shape)],
            out_specs=per_b((NSA_HD, NC)),
        ),
        compiler_params=_cparams(("parallel",)),
        name="attn_decode",
    )(page_table, *([cache_sel] * n_pages), qbd, gl, new_rows, win_prev, cmp_d, bcmp, bsel, bwin)
    return o.reshape(nb, NSA_HD, NSA_HEADS, tq).transpose(0, 3, 2, 1).reshape(nb * tq, NSA_HEADS * NSA_HD)


def _pad_cols(w, n):
    return jnp.pad(w, ((0, 0), (0, n - w.shape[1])))


def kernel(x_prompt, x_sample, state_dn_S, state_dn_conv, cache_kv, state_win_kv, page_table, norm_mix, norm_ffn, norm_kv, norm_final, ffn_w_in, ffn_w_out, dn_w_in, dn_conv_w, dn_A_log, dn_dt_bias, dn_out_norm, dn_w_out, nsa_w_kv, nsa_cmp_pos_w, nsa_w_cmp, nsa_w_in, nsa_w_out, rel_bias):
    B, T, D = x_prompt.shape
    NB, TS, _ = x_sample.shape
    Mp, Ms = B * T, NB * TS
    past = page_table.shape[1] * PAGE
    x = jnp.concatenate([x_prompt.reshape(Mp, D), x_sample.reshape(Ms, D)], axis=0)

    p_S, p_conv, s_S, s_conv = [], [], [], []
    for l in range(N_A_LAYERS):
        w_in = _pad_cols(dn_w_in[l], 4 * D + LANES).astype(BF16)
        proj = linear(x, w_in, norm_w=norm_mix[l], tn=(4 * D + LANES) // 3)
        o_p, S_p = dn_prompt(proj, B, T, dn_conv_w[l], dn_A_log[l], dn_dt_bias[l], dn_out_norm[l])
        o_s, S_s = dn_decode(proj[Mp:], NB, state_dn_conv[l], state_dn_S[l], dn_conv_w[l], dn_A_log[l],
                             dn_dt_bias[l], dn_out_norm[l])
        qkv_p = proj[:Mp, :DN_QKV].reshape(B, T, DN_QKV)
        qkv_s = proj[Mp:, :DN_QKV].reshape(NB, TS, DN_QKV)
        p_S.append(S_p)
        s_S.append(S_s)
        p_conv.append(qkv_p[:, T - (DN_CONV - 1):])
        s_conv.append(jnp.concatenate([state_dn_conv[l], qkv_s], axis=1)[:, TS:])
        o = jnp.concatenate([o_p, o_s], axis=0)
        x = linear(o, dn_w_out[l].astype(BF16), residual=x, tn=D)
        x = ffn(x, norm_ffn[l], ffn_w_in[l].astype(BF16), ffn_w_out[l].astype(BF16))

    W = NSA_KVH * NSA_HD
    rows = linear(x, nsa_w_kv.astype(BF16), norm_w=norm_kv, tn=6 * W)
    rows_p = rows[:Mp].reshape(B, T, 6, NSA_KVH, NSA_HD)
    rows_s = rows[Mp:].reshape(NB, TS, 6, NSA_KVH, NSA_HD)
    p_kv_rows = rows_p[:, :, :4]
    p_win_kv = rows_p[:, T - min(WINDOW, T):, 4:]
    s_kv_rows = rows_s[:, :, :4]
    s_win_kv = jnp.concatenate([state_win_kv, rows_s[:, :, 4:]], axis=1)[:, TS:]

    wlo, whi, wbd = _cmp_weights(nsa_cmp_pos_w, nsa_w_cmp)
    rows_bf = rows.astype(BF16)
    cmp_p = compress_prompt(rows, B, T, wlo, whi, wbd)
    new_rows = rows[Mp:].reshape(NB, TS, 6 * W)
    cache2 = cache_kv.reshape(cache_kv.shape[0], PAGE, 4 * W)
    cmp_d = compress_decode(cache2, page_table, new_rows, wlo, whi, wbd)
    win_prev = state_win_kv.reshape(NB, state_win_kv.shape[1], 2 * W)
    bias_c = bias_cmp_prompt(rel_bias, T)
    bands = bias_band(rel_bias)
    bias_d = bias_decode(rel_bias, past, TS)

    for j in range(N_B_LAYERS):
        l = N_A_LAYERS + j
        w_in = _pad_cols(nsa_w_in[j], D + LANES).astype(BF16)
        proj = linear(x, w_in, norm_w=norm_mix[l], tn=D + LANES)
        o_p = attn_prompt(proj, B, T, rows_bf, cmp_p, bias_c, bands)
        o_s = attn_decode(proj[Mp:], cache2, page_table, new_rows, win_prev, cmp_d, bias_d)
        o = jnp.concatenate([o_p, o_s], axis=0)
        x = linear(o, nsa_w_out[j].astype(BF16), residual=x, tn=D)
        x = ffn(x, norm_ffn[l], ffn_w_in[l].astype(BF16), ffn_w_out[l].astype(BF16))

    y = final_norm(x, norm_final)
    return (y[:Mp].reshape(B, T, D), y[Mp:].reshape(NB, TS, D),
            jnp.stack(p_S), jnp.stack(p_conv), p_kv_rows, p_win_kv,
            jnp.stack(s_S), jnp.stack(s_conv), s_kv_rows, s_win_kv)
```

```python
import functools
import math

import jax
import jax.numpy as jnp
from jax import lax
from jax.experimental import pallas as pl
from jax.experimental.pallas import tpu as pltpu

F32 = jnp.float32
BF16 = jnp.bfloat16

D_MODEL = 1024
N_A_LAYERS = 2
N_B_LAYERS = 2
NORM_EPS = 1e-6
DN_HEADS = 8
DN_HEAD = 128
DN_QKV = 3 * D_MODEL
DN_CONV = 4
DN_CHUNK = 64
NSA_HEADS = 16
NSA_HD = 64
NSA_KVH = 4
NSA_GROUP = 4
CMP_STRIDE = 16
CMP_BLOCK = 32
SEL_BLOCK = 64
N_SEL = 16
WINDOW = 512
PAGE = 128
REL_BUCKETS = 32
REL_MAX_EXACT = 16
REL_MAX_DIST = 1024
NEG = -1e30

V7X_VMEM_LIMIT = 56 * 1024 * 1024
LANES = 128


def _cparams(sem):
    return pltpu.CompilerParams(dimension_semantics=sem, vmem_limit_bytes=V7X_VMEM_LIMIT)


def _rms(x, w):
    ms = jnp.mean(x * x, axis=-1, keepdims=True)
    return x * lax.rsqrt(ms + NORM_EPS) * w


def _silu(x):
    return x * (1.0 / (1.0 + jnp.exp(-x)))


def _sigmoid(x):
    return 1.0 / (1.0 + jnp.exp(-x))


def _softplus(x):
    return jnp.maximum(x, 0.0) + jnp.log1p(jnp.exp(-jnp.abs(x)))


def _dot(a, b):
    return jnp.dot(a.astype(BF16), b.astype(BF16), preferred_element_type=F32)


def _dot_nt(a, b):
    return lax.dot_general(a.astype(BF16), b.astype(BF16), (((1,), (1,)), ((), ())),
                           preferred_element_type=F32)


def _bdot_dims(a, b, ca, cb):
    return lax.dot_general(a.astype(BF16), b.astype(BF16), (((ca,), (cb,)), ((0,), (0,))),
                           preferred_element_type=F32)


def _bdot(a, b):
    return _bdot_dims(a, b, 2, 1)


def _bdot_nt(a, b):
    return _bdot_dims(a, b, 2, 2)


def _bdot_tn(a, b):
    return _bdot_dims(a, b, 1, 1)


def _linear_body(*refs, has_norm, has_res):
    it = iter(refs)
    x_ref = next(it)
    nw_ref = next(it) if has_norm else None
    w_ref = next(it)
    res_ref = next(it) if has_res else None
    o_ref = next(it)
    xn_ref = next(it)

    @pl.when(pl.program_id(1) == 0)
    def _():
        x = x_ref[...]
        if has_norm:
            x = _rms(x, nw_ref[...])
        xn_ref[...] = x.astype(BF16)

    acc = jnp.dot(xn_ref[...], w_ref[...], preferred_element_type=F32)
    if has_res:
        acc = acc + res_ref[...]
    o_ref[...] = acc


def linear(x, w, norm_w=None, residual=None, tm=512, tn=None):
    M, K = x.shape
    N = w.shape[1]
    tn = N if tn is None else tn
    assert M % tm == 0 and N % tn == 0
    has_norm, has_res = norm_w is not None, residual is not None
    args, specs = [x], [pl.BlockSpec((tm, K), lambda i, j: (i, 0))]
    if has_norm:
        args.append(norm_w.reshape(1, K))
        specs.append(pl.BlockSpec((1, K), lambda i, j: (0, 0)))
    args.append(w)
    specs.append(pl.BlockSpec((K, tn), lambda i, j: (0, j)))
    if has_res:
        args.append(residual)
        specs.append(pl.BlockSpec((tm, tn), lambda i, j: (i, j)))
    return pl.pallas_call(
        functools.partial(_linear_body, has_norm=has_norm, has_res=has_res),
        out_shape=jax.ShapeDtypeStruct((M, N), F32),
        grid=(M // tm, N // tn),
        in_specs=specs,
        out_specs=pl.BlockSpec((tm, tn), lambda i, j: (i, j)),
        scratch_shapes=[pltpu.VMEM((tm, K), BF16)],
        compiler_params=_cparams(("parallel", "arbitrary")),
        name="linear",
    )(*args)


def _ffn_body(x_ref, nw_ref, wg_ref, wu_ref, wo_ref, o_ref, xn_ref, acc_ref):
    f = pl.program_id(1)

    @pl.when(f == 0)
    def _():
        xn_ref[...] = _rms(x_ref[...], nw_ref[...]).astype(BF16)
        acc_ref[...] = jnp.zeros_like(acc_ref)

    xn = xn_ref[...]
    g = jnp.dot(xn, wg_ref[...], preferred_element_type=F32)
    u = jnp.dot(xn, wu_ref[...], preferred_element_type=F32)
    a = (_silu(g) * u).astype(BF16)
    acc_ref[...] += jnp.dot(a, wo_ref[...], preferred_element_type=F32)

    @pl.when(f == pl.num_programs(1) - 1)
    def _():
        o_ref[...] = x_ref[...] + acc_ref[...]


def ffn(x, norm_w, w_in, w_out, tm=512, tf=256):
    M, D = x.shape
    FF = w_out.shape[0]
    nf = FF // tf
    assert M % tm == 0 and FF % tf == 0
    return pl.pallas_call(
        _ffn_body,
        out_shape=jax.ShapeDtypeStruct((M, D), F32),
        grid=(M // tm, nf),
        in_specs=[
            pl.BlockSpec((tm, D), lambda i, f: (i, 0)),
            pl.BlockSpec((1, D), lambda i, f: (0, 0)),
            pl.BlockSpec((D, tf), lambda i, f: (0, f)),
            pl.BlockSpec((D, tf), lambda i, f: (0, f + nf)),
            pl.BlockSpec((tf, D), lambda i, f: (f, 0)),
        ],
        out_specs=pl.BlockSpec((tm, D), lambda i, f: (i, 0)),
        scratch_shapes=[pltpu.VMEM((tm, D), BF16), pltpu.VMEM((tm, D), F32)],
        compiler_params=_cparams(("parallel", "arbitrary")),
        name="ffn",
    )(x, norm_w.reshape(1, D), w_in, w_in, w_out)


def _final_norm_body(x_ref, w_ref, o_ref):
    o_ref[...] = _rms(x_ref[...], w_ref[...])


def final_norm(x, w, tm=512):
    M, D = x.shape
    return pl.pallas_call(
        _final_norm_body,
        out_shape=jax.ShapeDtypeStruct((M, D), F32),
        grid=(M // tm,),
        in_specs=[pl.BlockSpec((tm, D), lambda i: (i, 0)), pl.BlockSpec((1, D), lambda i: (0, 0))],
        out_specs=pl.BlockSpec((tm, D), lambda i: (i, 0)),
        compiler_params=_cparams(("parallel",)),
        name="final_norm",
    )(x, w.reshape(1, D))


def _l2n(x):
    return x * lax.rsqrt(jnp.sum(x * x, axis=-1, keepdims=True) + NORM_EPS)


def _gated_out(o, z, onorm):
    return _rms(o, onorm) * _silu(z)


def _dn_prompt_body(qkv_ref, z_ref, ab_ref, abT_ref, cw_ref, alr_ref, dtr_ref, alc_ref, dtc_ref, on_ref,
                    o_ref, s_out_ref, xbuf_ref, s_ref):
    n = pl.program_id(1)
    C = DN_CHUNK

    @pl.when(n == 0)
    def _():
        xbuf_ref[0:8, :] = jnp.zeros((8, DN_QKV), F32)
        s_ref[...] = jnp.zeros_like(s_ref)

    xbuf_ref[8:8 + C, :] = qkv_ref[...]
    y = xbuf_ref[5:5 + C, :] * cw_ref[0:1, :]
    for i in range(1, DN_CONV):
        y = y + xbuf_ref[5 + i:5 + i + C, :] * cw_ref[i:i + 1, :]
    xbuf_ref[0:8, :] = xbuf_ref[C:C + 8, :]
    y = _silu(y)

    ab = ab_ref[...]
    g8 = -jnp.exp(alr_ref[...]) * _softplus(ab[:, 0:DN_HEADS] + dtr_ref[...])
    beta8 = _sigmoid(ab[:, DN_HEADS:2 * DN_HEADS])
    abT = abT_ref[0, 0]
    g8T = -jnp.exp(alc_ref[...]) * _softplus(abT[0:DN_HEADS, :] + dtc_ref[...])

    ii = lax.broadcasted_iota(jnp.int32, (C, C), 0)
    jj = lax.broadcasted_iota(jnp.int32, (C, C), 1)
    incl = (ii >= jj)[None]
    strict = (ii > jj)[None]
    onorm = on_ref[...]
    H = DN_HEADS

    heads = lambda off: jnp.stack([y[:, off + h * DN_HEAD:off + (h + 1) * DN_HEAD] for h in range(H)], axis=0)
    q = _l2n(heads(0)) * (DN_HEAD ** -0.5)
    k = _l2n(heads(D_MODEL))
    v = heads(2 * D_MODEL)
    g_col = jnp.stack([g8[:, h:h + 1] for h in range(H)], axis=0)
    beta = jnp.stack([beta8[:, h:h + 1] for h in range(H)], axis=0)
    g_row = jnp.stack([g8T[h:h + 1, :] for h in range(H)], axis=0)
    G_col = jnp.sum(jnp.where(incl, g_row, 0.0), axis=2, keepdims=True)
    G_row = jnp.sum(jnp.where((ii <= jj)[None], g_col, 0.0), axis=1, keepdims=True)
    dec = jnp.where(incl, jnp.exp(jnp.where(incl, G_col - G_row, 0.0)), 0.0)
    A = jnp.where(strict, beta * dec * _bdot_nt(k, k), 0.0)
    X = -A
    Tm = X
    for _ in range(int(math.log2(C)) - 1):
        X = _bdot(X, X)
        Tm = Tm + X + _bdot(X, Tm)
    eG = jnp.exp(G_col)
    w = jnp.concatenate([beta * v, (beta * eG) * k], axis=2)
    w = w + _bdot(Tm, w)
    wv, wk = w[:, :, :DN_HEAD], w[:, :, DN_HEAD:]
    aqk = dec * _bdot_nt(q, k)
    qg = eG * q
    G_last = G_col[:, C - 1:C, :]
    kdec = jnp.exp(G_last - G_col) * k
    S = s_ref[...]
    ws = _bdot(jnp.concatenate([wk, qg], axis=1), S)
    U = wv - ws[:, :C]
    O = ws[:, C:] + _bdot(aqk, U)
    s_ref[...] = jnp.exp(G_last) * S + _bdot_tn(kdec, U)
    for h in range(H):
        sl = slice(h * DN_HEAD, (h + 1) * DN_HEAD)
        o_ref[:, sl] = _gated_out(O[h], z_ref[:, sl], onorm)

    @pl.when(n == pl.num_programs(1) - 1)
    def _():
        s_out_ref[0] = s_ref[...]


def dn_prompt(proj, n_seq, T, conv_w, a_log, dt_bias, out_norm):
    M = n_seq * T
    C = DN_CHUNK
    N = T // C
    abT = proj[:M, 4 * D_MODEL:4 * D_MODEL + 2 * DN_HEADS].reshape(n_seq, N, C, 2 * DN_HEADS).transpose(0, 1, 3, 2)
    row = lambda a: a.reshape(1, DN_HEADS)
    col = lambda a: a.reshape(DN_HEADS, 1)
    return pl.pallas_call(
        _dn_prompt_body,
        out_shape=[jax.ShapeDtypeStruct((M, D_MODEL), F32),
                   jax.ShapeDtypeStruct((n_seq, DN_HEADS, DN_HEAD, DN_HEAD), F32)],
        grid=(n_seq, N),
        in_specs=[
            pl.BlockSpec((C, DN_QKV), lambda b, n: (b * N + n, 0)),
            pl.BlockSpec((C, D_MODEL), lambda b, n: (b * N + n, 3)),
            pl.BlockSpec((C, LANES), lambda b, n: (b * N + n, 4 * D_MODEL // LANES)),
            pl.BlockSpec((1, 1, 2 * DN_HEADS, C), lambda b, n: (b, n, 0, 0)),
            pl.BlockSpec((DN_CONV, DN_QKV), lambda b, n: (0, 0)),
            pl.BlockSpec((1, DN_HEADS), lambda b, n: (0, 0)),
            pl.BlockSpec((1, DN_HEADS), lambda b, n: (0, 0)),
            pl.BlockSpec((DN_HEADS, 1), lambda b, n: (0, 0)),
            pl.BlockSpec((DN_HEADS, 1), lambda b, n: (0, 0)),
            pl.BlockSpec((1, DN_HEAD), lambda b, n: (0, 0)),
        ],
        out_specs=[pl.BlockSpec((C, D_MODEL), lambda b, n: (b * N + n, 0)),
                   pl.BlockSpec((1, DN_HEADS, DN_HEAD, DN_HEAD), lambda b, n: (b, 0, 0, 0))],
        scratch_shapes=[pltpu.VMEM((C + 8, DN_QKV), F32), pltpu.VMEM((DN_HEADS, DN_HEAD, DN_HEAD), F32)],
        compiler_params=_cparams(("parallel", "arbitrary")),
        name="dn_prompt",
    )(proj, proj, proj, abT, conv_w, row(a_log), row(dt_bias), col(a_log), col(dt_bias), out_norm.reshape(1, DN_HEAD))


def _dn_decode_body(proj_ref, cbuf_ref, s0_ref, cw_ref, alr_ref, dtr_ref, on_ref, o_ref, s_out_ref, xbuf_ref, oacc_ref):
    T = proj_ref.shape[1]
    x = proj_ref[0]
    xbuf_ref[8 - (DN_CONV - 1):8, :] = cbuf_ref[0]
    xbuf_ref[8:8 + T, :] = x[:, :DN_QKV]
    y = xbuf_ref[5:5 + T, :] * cw_ref[0:1, :]
    for i in range(1, DN_CONV):
        y = y + xbuf_ref[5 + i:5 + i + T, :] * cw_ref[i:i + 1, :]
    y = _silu(y)
    ab = x[:, 4 * D_MODEL:4 * D_MODEL + LANES]
    a8 = jnp.exp(-jnp.exp(alr_ref[...]) * _softplus(ab[:, 0:DN_HEADS] + dtr_ref[...]))
    beta8 = _sigmoid(ab[:, DN_HEADS:2 * DN_HEADS])
    ii = lax.broadcasted_iota(jnp.int32, (DN_HEAD, DN_HEAD), 0)
    jj = lax.broadcasted_iota(jnp.int32, (DN_HEAD, DN_HEAD), 1)
    eye = ii == jj

    def to_col(r):
        return jnp.sum(jnp.where(eye, r, 0.0), axis=1, keepdims=True)

    for h in range(DN_HEADS):
        sl = slice(h * DN_HEAD, (h + 1) * DN_HEAD)
        q = _l2n(y[:, sl]) * (DN_HEAD ** -0.5)
        k = _l2n(y[:, D_MODEL + h * DN_HEAD:D_MODEL + (h + 1) * DN_HEAD])
        v = y[:, 2 * D_MODEL + h * DN_HEAD:2 * D_MODEL + (h + 1) * DN_HEAD]
        S = s0_ref[0, h]
        for t in range(T):
            k_col = to_col(k[t:t + 1, :])
            q_col = to_col(q[t:t + 1, :])
            a = a8[t:t + 1, h:h + 1]
            b = beta8[t:t + 1, h:h + 1]
            kS = jnp.sum(k_col * S, axis=0, keepdims=True)
            S = a * S + k_col * (b * (v[t:t + 1, :] - a * kS))
            oacc_ref[t:t + 1, sl] = jnp.sum(q_col * S, axis=0, keepdims=True)
        s_out_ref[0, h] = S
    onorm = on_ref[...]
    for h in range(DN_HEADS):
        sl = slice(h * DN_HEAD, (h + 1) * DN_HEAD)
        o_ref[0, :, sl] = _gated_out(oacc_ref[0:T, sl], x[:, DN_QKV + h * DN_HEAD:DN_QKV + (h + 1) * DN_HEAD], onorm)


def dn_decode(proj, n_seq, conv_buf, S0, conv_w, a_log, dt_bias, out_norm):
    M, W = proj.shape
    T = M // n_seq
    row = lambda a: a.reshape(1, DN_HEADS)
    o, S = pl.pallas_call(
        _dn_decode_body,
        out_shape=[jax.ShapeDtypeStruct((n_seq, T, D_MODEL), F32),
                   jax.ShapeDtypeStruct((n_seq, DN_HEADS, DN_HEAD, DN_HEAD), F32)],
        grid=(n_seq,),
        in_specs=[
            pl.BlockSpec((1, T, W), lambda b: (b, 0, 0)),
            pl.BlockSpec((1, DN_CONV - 1, DN_QKV), lambda b: (b, 0, 0)),
            pl.BlockSpec((1, DN_HEADS, DN_HEAD, DN_HEAD), lambda b: (b, 0, 0, 0)),
            pl.BlockSpec((DN_CONV, DN_QKV), lambda b: (0, 0)),
            pl.BlockSpec((1, DN_HEADS), lambda b: (0, 0)),
            pl.BlockSpec((1, DN_HEADS), lambda b: (0, 0)),
            pl.BlockSpec((1, DN_HEAD), lambda b: (0, 0)),
        ],
        out_specs=[pl.BlockSpec((1, T, D_MODEL), lambda b: (b, 0, 0)),
                   pl.BlockSpec((1, DN_HEADS, DN_HEAD, DN_HEAD), lambda b: (b, 0, 0, 0))],
        scratch_shapes=[pltpu.VMEM((16, DN_QKV), F32), pltpu.VMEM((8, D_MODEL), F32)],
        compiler_params=_cparams(("parallel",)),
        name="dn_decode",
    )(proj.reshape(n_seq, T, W), conv_buf, S0, conv_w, row(a_log), row(dt_bias), out_norm.reshape(1, DN_HEAD))
    return o.reshape(M, D_MODEL), S


def _bucket_thresholds():
    thr, prev = [], REL_MAX_EXACT
    for d in range(REL_MAX_EXACT, REL_MAX_DIST + 1):
        val = min(REL_MAX_EXACT + int(math.log(d / REL_MAX_EXACT) / math.log(REL_MAX_DIST / REL_MAX_EXACT)
                                      * (REL_BUCKETS - REL_MAX_EXACT)), REL_BUCKETS - 1)
        thr += [d] * (val - prev)
        prev = val
    assert len(thr) == REL_BUCKETS - 1 - REL_MAX_EXACT
    return tuple(thr)


_BUCKET_THR = _bucket_thresholds()
TQ = 256
BAND_ROWS = 128
TK = 512
BAND_TOP = (REL_MAX_DIST + TK + LANES - 1) // LANES * LANES
BAND_W = BAND_TOP + max(TK, WINDOW + TQ)
LOG2E = 1.4426950408889634


def _bucket(d):
    n = jnp.maximum(d, 0)
    big = jnp.full(n.shape, REL_MAX_EXACT, jnp.int32)
    for t in _BUCKET_THR:
        big = big + (n >= t).astype(jnp.int32)
    return jnp.where(n < REL_MAX_EXACT, n, big)


def _bias_lookup(bucket, table_row):
    acc = jnp.zeros(bucket.shape, F32)
    for k in range(REL_BUCKETS):
        acc = acc + jnp.where(bucket == k, table_row(k), 0.0)
    return acc


def _bias_cmp_body(tab_ref, o_ref, *, n_cmp):
    q0 = pl.program_id(0) * TQ
    shp = o_ref.shape[1:]
    t = q0 + lax.broadcasted_iota(jnp.int32, shp, 0)
    j = lax.broadcasted_iota(jnp.int32, shp, 1)
    d = t - (j * CMP_STRIDE + CMP_BLOCK - 1)
    dead = (d < 0) | (j >= n_cmp)
    bucket = _bucket(d)
    for h in range(NSA_HEADS):
        o_ref[h] = jnp.where(dead, NEG, _bias_lookup(bucket, lambda k: tab_ref[k, h]))


def bias_cmp_prompt(rel_bias, T):
    n_sub = T // CMP_STRIDE
    return pl.pallas_call(
        functools.partial(_bias_cmp_body, n_cmp=n_sub - 1),
        out_shape=jax.ShapeDtypeStruct((NSA_HEADS, T, n_sub), F32),
        grid=(T // TQ,),
        in_specs=[pl.BlockSpec(memory_space=pltpu.SMEM)],
        out_specs=pl.BlockSpec((NSA_HEADS, TQ, n_sub), lambda i: (0, i, 0)),
        compiler_params=_cparams(("parallel",)),
        name="bias_cmp",
    )(rel_bias)


def _bias_band_body(tab_ref, o_ref, ow_ref):
    h = pl.program_id(0)
    shp = o_ref.shape[1:]
    d = BAND_TOP + lax.broadcasted_iota(jnp.int32, shp, 0) - lax.broadcasted_iota(jnp.int32, shp, 1)
    bias = LOG2E * _bias_lookup(_bucket(d), lambda k: tab_ref[k, h])
    o_ref[0] = jnp.where(d < 0, NEG, bias)
    ow_ref[0] = jnp.where((d < 0) | (d >= WINDOW), NEG, bias)


def bias_band(rel_bias):
    shape = jax.ShapeDtypeStruct((NSA_HEADS, BAND_ROWS, BAND_W), F32)
    spec = pl.BlockSpec((1, BAND_ROWS, BAND_W), lambda h: (h, 0, 0))
    return pl.pallas_call(
        _bias_band_body,
        out_shape=[shape, shape],
        grid=(NSA_HEADS,),
        in_specs=[pl.BlockSpec(memory_space=pltpu.SMEM)],
        out_specs=[spec, spec],
        compiler_params=_cparams(("parallel",)),
        name="bias_band",
    )(rel_bias)


def _bias_decode_body(tabc_ref, cmp_ref, sel_ref, win_ref, *, past, tq):
    shp = (PAGE, NSA_HEADS * tq)
    i = lax.broadcasted_iota(jnp.int32, shp, 0)
    c = lax.broadcasted_iota(jnp.int32, shp, 1)
    qpos = past + (c & (tq - 1))
    row = lambda k: tabc_ref[k:k + 1, :]

    def table(d, dead):
        return jnp.where(dead | (d < 0), NEG, _bias_lookup(_bucket(d), row))

    ic = lax.broadcasted_iota(jnp.int32, cmp_ref.shape, 0)
    qc = past + (lax.broadcasted_iota(jnp.int32, cmp_ref.shape, 1) & (tq - 1))
    cmp_ref[...] = table(qc - (ic * CMP_STRIDE + CMP_BLOCK - 1), ic < 0)
    n_pages = past // PAGE
    for p in range(n_pages):
        sel_ref[p] = table(qpos - (p * PAGE + i), i < 0)
    sel_ref[n_pages] = table(qpos - (past + i), i >= tq)
    n_wt = WINDOW // PAGE
    for t in range(n_wt):
        d = qpos - (past - WINDOW + t * PAGE + i)
        win_ref[t] = table(d, d >= WINDOW)
    win_ref[n_wt] = table(qpos - (past + i), i >= tq)


def bias_decode(rel_bias, past, tq):
    assert tq & (tq - 1) == 0
    tabc = jnp.repeat(rel_bias, tq, axis=1)
    n_pages = past // PAGE
    nc = NSA_HEADS * tq
    return pl.pallas_call(
        functools.partial(_bias_decode_body, past=past, tq=tq),
        out_shape=[jax.ShapeDtypeStruct((past // CMP_STRIDE, nc), F32),
                   jax.ShapeDtypeStruct((n_pages + 1, PAGE, nc), F32),
                   jax.ShapeDtypeStruct((WINDOW // PAGE + 1, PAGE, nc), F32)],
        name="bias_decode",
    )(tabc)


def _cmp_weights(cmp_pos_w, w_cmp):
    w = jnp.concatenate([cmp_pos_w[0].reshape(CMP_BLOCK, -1), cmp_pos_w[1].reshape(CMP_BLOCK, -1)], axis=1)
    blocks = w_cmp.reshape(2 * NSA_KVH, NSA_HD, NSA_HD)
    n = 2 * NSA_KVH
    wbd = (jnp.eye(n, dtype=F32)[:, None, :, None] * blocks[:, :, None, :]).reshape(n * NSA_HD, n * NSA_HD)
    return w[:CMP_STRIDE], w[CMP_STRIDE:], wbd.astype(BF16)


def _pool16(x, w):
    n = x.shape[0] // CMP_STRIDE
    return jnp.sum(x.reshape(n, CMP_STRIDE, x.shape[1]) * w[None], axis=1)


def _compress_prompt_body(x_ref, wlo_ref, whi_ref, wbd_ref, o_ref, lo_ref, hi_ref):
    T = x_ref.shape[0]
    n_sub = T // CMP_STRIDE
    step = 512
    for c in range(T // step):
        xs = x_ref[c * step:(c + 1) * step, :]
        r = slice(c * step // CMP_STRIDE, (c + 1) * step // CMP_STRIDE)
        lo_ref[r, :] = _pool16(xs, wlo_ref[...])
        hi_ref[r, :] = _pool16(xs, whi_ref[...])
    hi_ref[n_sub:n_sub + 8, :] = jnp.zeros((8, hi_ref.shape[1]), F32)
    blocks = lo_ref[...] + hi_ref[1:n_sub + 1, :]
    o_ref[...] = _dot(blocks, wbd_ref[...]).astype(BF16)


def compress_prompt(rows, n_seq, T, wlo, whi, wbd):
    n_sub = T // CMP_STRIDE
    W = 2 * NSA_KVH * NSA_HD
    return pl.pallas_call(
        _compress_prompt_body,
        out_shape=jax.ShapeDtypeStruct((n_seq * n_sub, W), BF16),
        grid=(n_seq,),
        in_specs=[pl.BlockSpec((T, W), lambda b: (b, 0)),
                  pl.BlockSpec((CMP_STRIDE, W), lambda b: (0, 0)),
                  pl.BlockSpec((CMP_STRIDE, W), lambda b: (0, 0)),
                  pl.BlockSpec((W, W), lambda b: (0, 0))],
        out_specs=pl.BlockSpec((n_sub, W), lambda b: (b, 0)),
        scratch_shapes=[pltpu.VMEM((n_sub, W), F32), pltpu.VMEM((n_sub + 8, W), F32)],
        compiler_params=_cparams(("parallel",)),
        name="compress_prompt",
    )(rows, wlo, whi, wbd)


def _masked_softmax(s, mask, axis):
    l = jnp.where(mask, s, NEG)
    m = jnp.max(l, axis=axis, keepdims=True)
    e = jnp.where(mask, jnp.exp(l - m), 0.0)
    return e / jnp.maximum(jnp.sum(e, axis=axis, keepdims=True), 1e-30)


def _split3(x):
    hi = x.astype(BF16)
    r = x - hi.astype(F32)
    mid = r.astype(BF16)
    lo = (r - mid.astype(F32)).astype(BF16)
    return hi, mid, lo


def _topk_mask(score, blk, n_pick, axis, removed):
    sel = jnp.zeros(score.shape, F32)
    s = score
    big = jnp.int32(1 << 20)
    for _ in range(n_pick):
        mx = jnp.max(s, axis=axis, keepdims=True)
        idx = jnp.min(jnp.where(s == mx, blk, big), axis=axis, keepdims=True)
        hit = blk == idx
        sel = jnp.where(hit, 1.0, sel)
        s = jnp.where(hit, removed, s)
    return sel


def _attn_prompt_body(q_ref, g_ref, ka_ref, vs_ref, kw_ref, vw_ref, kc_ref, vc_ref, bc_ref, band_ref, bandw_ref,
                      o_ref, qp_scr, qw_scr, qa_scr, s_scr, p_scr, m_scr, acc_scr, sc_scr, sw_scr, pw_scr):
    qb = pl.program_id(2)
    q0 = qb * TQ
    G = NSA_GROUP
    R = G * TQ
    HD = NSA_HD
    groups = [slice(g * TQ, (g + 1) * TQ) for g in range(G)]
    slabs = [(slice(g * TQ + h, g * TQ + h + BAND_ROWS), g, h) for g in range(G) for h in range(0, TQ, BAND_ROWS)]
    qblk = q_ref[...] * (HD ** -0.5)
    Q = jnp.concatenate([qblk[:, g * HD:(g + 1) * HD] for g in range(G)], axis=0)
    zeros = jnp.zeros((R, HD), F32)
    qp_scr[...] = jnp.concatenate([Q, zeros], axis=1).astype(BF16)
    qw_scr[...] = jnp.concatenate([Q * LOG2E, zeros], axis=1).astype(BF16)

    left = lax.broadcasted_iota(jnp.int32, (TQ, 2 * HD), 1) < HD

    def normalized(acc):
        tiles = []
        for g in range(0, G, 2):
            a0, a1 = acc[groups[g]], acc[groups[g + 1]]
            r0, r1 = pltpu.roll(a0, HD, 1), pltpu.roll(a1, HD, 1)
            num = jnp.where(left, a0, r1)
            den = jnp.where(left, r0, a1)
            tiles.append(num / jnp.maximum(den, 1e-30))
        return jnp.concatenate(tiles, axis=1)

    gates = _split3(_sigmoid(g_ref[0]))
    W = G * HD
    gi = lax.broadcasted_iota(jnp.int32, (3 * G, W), 0)
    gc = lax.broadcasted_iota(jnp.int32, (3 * G, W), 1) // HD

    def gate(branch):
        spread = (gi == 3 * gc + branch).astype(BF16)
        return sum(jnp.dot(part, spread, preferred_element_type=F32) for part in gates)

    WK = WINDOW + TQ
    kw0 = pl.multiple_of(jnp.maximum(q0 - WINDOW, 0), LANES)
    sw_scr[...] = _dot_nt(qw_scr[...], kw_ref[pl.ds(kw0, WK), :])
    for r, g, h in slabs:
        cw0 = pl.multiple_of(BAND_TOP - (q0 + h - kw0), LANES)
        s = sw_scr[r, :] + bandw_ref[g, :, pl.ds(cw0, WK)]
        m = jnp.maximum(jnp.max(s, axis=1, keepdims=True), 0.5 * NEG)
        pw_scr[r, :] = jnp.exp2(s - m).astype(BF16)
    o_ref[...] = normalized(jnp.dot(pw_scr[...], vw_ref[pl.ds(kw0, WK), :], preferred_element_type=F32)) * gate(2)

    n_sub = bc_ref.shape[2]
    n_blk = n_sub * CMP_STRIDE // SEL_BLOCK
    n_pick = min(N_SEL, n_blk)
    per = SEL_BLOCK // CMP_STRIDE
    bb = lax.broadcasted_iota(jnp.int32, (n_blk, n_sub), 0)
    mm = lax.broadcasted_iota(jnp.int32, (n_blk, n_sub), 1)
    pool = ((mm // per == bb).astype(F32) + ((mm + 1) // per == bb).astype(F32)).astype(BF16)
    s_scr[:, :n_sub] = _dot_nt(qp_scr[...], kc_ref[...])
    for h in range(0, TQ, BAND_ROWS):
        imp = None
        for g in range(G):
            r = slice(g * TQ + h, g * TQ + h + BAND_ROWS)
            s = s_scr[r, :n_sub] + bc_ref[g, h:h + BAND_ROWS, :]
            e = jnp.exp(s - jnp.maximum(jnp.max(s, axis=1, keepdims=True), 0.5 * NEG))
            p_scr[r, :n_sub] = e.astype(BF16)
            p = e / jnp.maximum(jnp.sum(e, axis=1, keepdims=True), 1e-30)
            imp = p if imp is None else imp + p
        sc_scr[:, h:h + BAND_ROWS] = sum(_dot_nt(pool, part) for part in _split3(imp))
    o_ref[...] += normalized(jnp.dot(p_scr[:, :n_sub], vc_ref[...], preferred_element_type=F32)) * gate(0)

    p_slc = sc_scr[...]
    blk = lax.broadcasted_iota(jnp.int32, (n_blk, TQ), 0)
    t = q0 + lax.broadcasted_iota(jnp.int32, (n_blk, TQ), 1)
    cur = t // SEL_BLOCK
    forced = (blk == 0) | (blk == cur) | (blk == cur - 1)
    score = jnp.where(forced, 1e4, jnp.where(blk * SEL_BLOCK <= t, p_slc, -1.0))
    sc_scr[...] = score
    beaten = jnp.zeros((n_blk, TQ), F32)
    for other in range(n_blk):
        row = sc_scr[other:other + 1, :]
        ge = jnp.where(row >= score, 1.0, 0.0)
        gt = jnp.where(row > score, 1.0, 0.0)
        beaten = beaten + jnp.where(blk > other, ge, gt)
    selneg = jnp.where(beaten < n_pick, 0.0, NEG)
    selneg_q = selneg.T
    if n_blk < HD:
        selneg_q = jnp.concatenate([selneg_q, jnp.zeros((TQ, HD - n_blk), F32)], axis=1)
    qa_scr[...] = jnp.concatenate([jnp.concatenate([Q[r] * LOG2E, selneg_q], axis=1) for r in groups],
                                  axis=0).astype(BF16)

    def soften(kt, pv):
        rowmax = []
        for r, g, h in slabs:
            c0 = pl.multiple_of(jnp.maximum(BAND_TOP - (q0 + h - kt * TK), 0), LANES)
            s = s_scr[r, :] + band_ref[g, :, pl.ds(c0, TK)]
            s_scr[r, :] = s
            rowmax.append(jnp.max(s, axis=1, keepdims=True))
        for i, (r, g, h) in enumerate(slabs):
            m_old = m_scr[r, :]
            m_new = jnp.maximum(m_old, rowmax[i])
            m_scr[r, :] = m_new
            a = acc_scr[r, :] if pv is None else acc_scr[r, :] + pv[r]
            acc_scr[r, :] = jnp.exp2(m_old - m_new) * a
            for c in range(TK // LANES):
                cs = slice(c * LANES, (c + 1) * LANES)
                p_scr[r, cs] = jnp.exp2(s_scr[r, cs] - m_new).astype(BF16)

    def sel_qk(kt):
        s_scr[...] = _dot_nt(qa_scr[...], ka_ref[pl.ds(pl.multiple_of(kt * TK, TK), TK), :])

    def sel_pv(kt):
        v = vs_ref[pl.ds(pl.multiple_of(kt * TK, TK), TK), :]
        return jnp.dot(p_scr[...], v, preferred_element_type=F32)

    m_scr[...] = jnp.full(m_scr.shape, 0.5 * NEG, F32)
    acc_scr[...] = jnp.zeros(acc_scr.shape, F32)
    sel_qk(0)
    soften(0, None)

    def step(kt, carry):
        pv = sel_pv(kt - 1)
        sel_qk(kt)
        soften(kt, pv)
        return carry

    n_kt = (q0 + TQ + TK - 1) // TK
    lax.fori_loop(1, n_kt, step, 0)
    o_ref[...] += normalized(acc_scr[...] + sel_pv(n_kt - 1)) * gate(1)


def attn_prompt_operands(rows_bf, cmp_p, n_seq, T):
    M = n_seq * T
    W = NSA_KVH * NSA_HD

    def per_head(x, aux):
        x = x.reshape(x.shape[0], NSA_KVH, NSA_HD)
        out = jnp.concatenate([x, jnp.broadcast_to(aux[:, None, :], x.shape)], axis=2)
        return out.reshape(x.shape[0], NSA_KVH * 2 * NSA_HD)

    blk_of_key = (jnp.arange(M, dtype=jnp.int32) % T) // SEL_BLOCK
    onehot = (blk_of_key[:, None] == jnp.arange(NSA_HD, dtype=jnp.int32)[None, :]).astype(BF16)
    zeros, ones = jnp.zeros((M, NSA_HD), BF16), jnp.ones((M, NSA_HD), BF16)
    kind = lambda i: rows_bf[:M, i * W:(i + 1) * W]
    nc = cmp_p.shape[0]
    return (per_head(kind(2), onehot), per_head(kind(3), ones), per_head(kind(4), zeros), per_head(kind(5), ones),
            per_head(cmp_p[:, :W], zeros[:nc]), per_head(cmp_p[:, W:], ones[:nc]))


def attn_prompt(proj, n_seq, T, operands, bias_c, bands):
    M = n_seq * T
    NQ = T // TQ
    n_sub = T // CMP_STRIDE
    n_blk = T // SEL_BLOCK
    assert n_blk <= NSA_HD and T % TK == 0 and T >= WINDOW + TQ and n_sub <= TK
    gl = proj[:M, NSA_HEADS * NSA_HD:NSA_HEADS * NSA_HD + 3 * NSA_HEADS]
    gl = gl.reshape(M, NSA_KVH, 3 * NSA_GROUP).transpose(1, 0, 2)
    pair = 2 * NSA_HD
    ks_a, vs_a, kw_a, vw_a, kc_a, vc_a = operands
    kv_spec = pl.BlockSpec((T, pair), lambda k, b, i: (b, k))
    cmp_spec = pl.BlockSpec((n_sub, pair), lambda k, b, i: (b, k))
    band_spec = pl.BlockSpec((NSA_GROUP, BAND_ROWS, BAND_W), lambda k, b, i: (k, 0, 0))
    WO = NSA_GROUP * NSA_HD
    R = NSA_GROUP * TQ
    WK = WINDOW + TQ
    band, bandw = bands
    return pl.pallas_call(
        _attn_prompt_body,
        out_shape=jax.ShapeDtypeStruct((M, NSA_HEADS * NSA_HD), F32),
        grid=(NSA_KVH, n_seq, NQ),
        in_specs=[
            pl.BlockSpec((TQ, WO), lambda k, b, i: (b * NQ + i, k)),
            pl.BlockSpec((1, TQ, 3 * NSA_GROUP), lambda k, b, i: (k, b * NQ + i, 0)),
            kv_spec, kv_spec, kv_spec, kv_spec,
            cmp_spec, cmp_spec,
            pl.BlockSpec((NSA_GROUP, TQ, n_sub), lambda k, b, i: (k, i, 0)),
            band_spec, band_spec,
        ],
        out_specs=pl.BlockSpec((TQ, WO), lambda k, b, i: (b * NQ + i, k)),
        scratch_shapes=[pltpu.VMEM((R, pair), BF16), pltpu.VMEM((R, pair), BF16), pltpu.VMEM((R, pair), BF16),
                        pltpu.VMEM((R, TK), F32), pltpu.VMEM((R, TK), BF16),
                        pltpu.VMEM((R, LANES), F32), pltpu.VMEM((R, pair), F32),
                        pltpu.VMEM((n_blk, TQ), F32),
                        pltpu.VMEM((R, WK), F32), pltpu.VMEM((R, WK), BF16)],
        compiler_params=_cparams(("parallel", "parallel", "arbitrary")),
        name="attn_prompt",
    )(proj, gl, ks_a, vs_a, kw_a, vw_a, kc_a, vc_a, bias_c, band, bandw)


def _compress_decode_body(pt_ref, *refs, n_pages):
    pages = refs[:n_pages]
    new_ref, wlo_ref, whi_ref, wbd_ref, o_ref, lo_ref, hi_ref = refs[n_pages:]
    per = PAGE // CMP_STRIDE
    for p in range(n_pages):
        x = pages[p][0]
        lo_ref[p * per:(p + 1) * per, :] = _pool16(x, wlo_ref[...])
        hi_ref[p * per:(p + 1) * per, :] = _pool16(x, whi_ref[...])
    n_sub = n_pages * per
    tq = new_ref.shape[1]
    hi_ref[n_sub:n_sub + 8, :] = jnp.zeros((8, hi_ref.shape[1]), F32)
    hi_ref[n_sub:n_sub + 1, :] = jnp.sum(new_ref[0][:, :hi_ref.shape[1]] * whi_ref[0:tq, :], axis=0, keepdims=True)
    blocks = lo_ref[...] + hi_ref[1:n_sub + 1, :]
    o_ref[0] = _dot(blocks, wbd_ref[...]).astype(BF16)


def compress_decode(cache, page_table, new_rows, wlo, whi, wbd):
    nb, n_pages = page_table.shape
    W = 2 * NSA_KVH * NSA_HD
    tq = new_rows.shape[1]
    assert tq <= CMP_STRIDE
    n_sub = n_pages * PAGE // CMP_STRIDE
    page_spec = lambda p: pl.BlockSpec((1, PAGE, W), lambda b, pt: (pt[b, p], 0, 0))
    const = lambda shape: pl.BlockSpec(shape, lambda b, pt: (0,) * len(shape))
    return pl.pallas_call(
        functools.partial(_compress_decode_body, n_pages=n_pages),
        out_shape=jax.ShapeDtypeStruct((nb, n_sub, W), BF16),
        grid_spec=pltpu.PrefetchScalarGridSpec(
            num_scalar_prefetch=1,
            grid=(nb,),
            in_specs=[page_spec(p) for p in range(n_pages)] + [
                pl.BlockSpec((1, tq, new_rows.shape[2]), lambda b, pt: (b, 0, 0)),
                const((CMP_STRIDE, W)), const((CMP_STRIDE, W)), const((W, W))],
            out_specs=pl.BlockSpec((1, n_sub, W), lambda b, pt: (b, 0, 0)),
            scratch_shapes=[pltpu.VMEM((n_sub, W), F32), pltpu.VMEM((n_sub + 8, W), F32)],
        ),
        compiler_params=_cparams(("parallel",)),
        name="compress_decode",
    )(page_table, *([cache] * n_pages), new_rows, wlo, whi, wbd)


def _attn_decode_body(pt_ref, *refs, n_pages, tq):
    pages = refs[:n_pages]
    q_ref, g_ref, new_ref, win_ref, cmp_ref, bcmp_ref, bsel_ref, bwin_ref, o_ref = refs[n_pages:]
    W = NSA_KVH * NSA_HD
    NC = NSA_HEADS * tq
    Qbd = q_ref[0]
    past = n_pages * PAGE

    cm = cmp_ref[0]
    bc = bcmp_ref[...]
    p_c = _masked_softmax(_dot_nt(cm[:, :W], Qbd) + bc, bc > 0.5 * NEG, 0)
    tn = (((0,), (0,)), ((), ()))
    o_cmp = lax.dot_general(cm[:, W:], p_c.astype(BF16), tn, preferred_element_type=F32)

    n_sub = cm.shape[0]
    per = SEL_BLOCK // CMP_STRIDE
    n_blk = past // SEL_BLOCK + 1
    nb_pad = (n_blk + 7) // 8 * 8
    ci = lax.broadcasted_iota(jnp.int32, (NC, NC), 0)
    cj = lax.broadcasted_iota(jnp.int32, (NC, NC), 1)
    gq = NSA_GROUP * tq
    same = ((ci // gq == cj // gq) & ((ci & (tq - 1)) == (cj & (tq - 1)))).astype(BF16)
    imp = sum(jnp.dot(part, same, preferred_element_type=F32) for part in _split3(p_c))
    bb = lax.broadcasted_iota(jnp.int32, (nb_pad, n_sub), 0)
    mm = lax.broadcasted_iota(jnp.int32, (nb_pad, n_sub), 1)
    pool = ((mm // per == bb).astype(F32) + ((mm + 1) // per == bb).astype(F32)).astype(BF16)
    p_slc = sum(jnp.dot(pool, part, preferred_element_type=F32) for part in _split3(imp))
    blk = lax.broadcasted_iota(jnp.int32, (nb_pad, NC), 0)
    qpos = past + (lax.broadcasted_iota(jnp.int32, (nb_pad, NC), 1) & (tq - 1))
    cur = qpos // SEL_BLOCK
    forced = (blk == 0) | (blk == cur) | (blk == cur - 1)
    score = jnp.where(forced, 1e4, jnp.where(blk * SEL_BLOCK <= qpos, p_slc, -1.0))
    score = jnp.where(blk < n_blk, score, -3.0)
    sel = _topk_mask(score, blk, min(N_SEL, n_blk), 0, -5.0)

    def flash_tile(carry, k, v, bias, mask):
        m, l, acc = carry
        s = jnp.where(mask, _dot_nt(k, Qbd) + bias, NEG)
        m_new = jnp.maximum(m, jnp.max(s, axis=0, keepdims=True))
        p = jnp.where(mask, jnp.exp(s - m_new), 0.0)
        alpha = jnp.exp(m - m_new)
        l = alpha * l + jnp.sum(p, axis=0, keepdims=True)
        acc = alpha * acc + lax.dot_general(v.astype(BF16), p.astype(BF16), tn, preferred_element_type=F32)
        return m_new, l, acc

    init = (jnp.full((1, NC), NEG, F32), jnp.zeros((1, NC), F32), jnp.zeros((W, NC), F32))
    new = new_ref[0]
    pad = jnp.zeros((8 - tq, W), F32)
    new_tile = lambda kind: jnp.concatenate([new[:, kind * W:(kind + 1) * W], pad], axis=0)

    half = lax.broadcasted_iota(jnp.int32, (PAGE, NC), 0) < SEL_BLOCK
    carry = init
    for p in range(n_pages):
        pg = pages[p][0]
        b0 = p * (PAGE // SEL_BLOCK)
        picked = jnp.where(half, sel[b0:b0 + 1, :], sel[b0 + 1:b0 + 2, :]) > 0.5
        bias = bsel_ref[p]
        carry = flash_tile(carry, pg[:, :W], pg[:, W:], bias, picked & (bias > 0.5 * NEG))
    bias = bsel_ref[n_pages][0:8]
    carry = flash_tile(carry, new_tile(2), new_tile(3), bias, (sel[n_blk - 1:n_blk, :] > 0.5) & (bias > 0.5 * NEG))
    o_sel = carry[2] / jnp.maximum(carry[1], 1e-30)

    carry = init
    for t in range(WINDOW // PAGE):
        wt = win_ref[0, t * PAGE:(t + 1) * PAGE, :]
        bias = bwin_ref[t]
        carry = flash_tile(carry, wt[:, :W], wt[:, W:], bias, bias > 0.5 * NEG)
    bias = bwin_ref[WINDOW // PAGE][0:8]
    carry = flash_tile(carry, new_tile(4), new_tile(5), bias, bias > 0.5 * NEG)
    o_win = carry[2] / jnp.maximum(carry[1], 1e-30)

    gt = _sigmoid(g_ref[0])
    o = o_cmp * gt[0:1] + o_sel * gt[1:2] + o_win * gt[2:3]
    kvh_of_col = lax.broadcasted_iota(jnp.int32, (NSA_HD, NC), 1) // gq
    out = jnp.zeros((NSA_HD, NC), F32)
    for k in range(NSA_KVH):
        out = out + jnp.where(kvh_of_col == k, o[k * NSA_HD:(k + 1) * NSA_HD, :], 0.0)
    o_ref[0] = out


def attn_decode(proj, cache_sel, page_table, new_rows, win_prev, cmp_d, bias_tabs):
    nb, n_pages = page_table.shape
    tq = new_rows.shape[1]
    W = NSA_KVH * NSA_HD
    NC = NSA_HEADS * tq
    q = proj[:, :NSA_HEADS * NSA_HD] * (NSA_HD ** -0.5)
    q = q.reshape(nb, tq, NSA_KVH, NSA_GROUP, NSA_HD).transpose(0, 2, 3, 1, 4)
    qbd = q[:, :, :, :, None, :] * jnp.eye(NSA_KVH, dtype=F32)[None, :, None, None, :, None]
    qbd = qbd.reshape(nb, NC, W).astype(BF16)
    gl = proj[:, NSA_HEADS * NSA_HD:NSA_HEADS * NSA_HD + 3 * NSA_HEADS]
    gl = gl.reshape(nb, tq, NSA_HEADS, 3).transpose(0, 3, 2, 1).reshape(nb, 3, NC)
    bcmp, bsel, bwin = bias_tabs
    page_spec = lambda p: pl.BlockSpec((1, PAGE, 2 * W), lambda b, pt: (pt[b, p], 0, 1))
    const = lambda shape: pl.BlockSpec(shape, lambda b, pt: (0,) * len(shape))
    per_b = lambda shape: pl.BlockSpec((1,) + shape, lambda b, pt: (b,) + (0,) * len(shape))
    o = pl.pallas_call(
        functools.partial(_attn_decode_body, n_pages=n_pages, tq=tq),
        out_shape=jax.ShapeDtypeStruct((nb, NSA_HD, NC), F32),
        grid_spec=pltpu.PrefetchScalarGridSpec(
            num_scalar_prefetch=1,
            grid=(nb,),
            in_specs=[page_spec(p) for p in range(n_pages)] + [
                per_b((NC, W)), per_b((3, NC)), per_b((tq, 6 * W)), per_b((WINDOW, 2 * W)),
                per_b((cmp_d.shape[1], 2 * W)),
                const(bcmp.shape), const(bsel.shape), const(bwin.shape)],
            out_specs=per_b((NSA_HD, NC)),
        ),
        compiler_params=_cparams(("parallel",)),
        name="attn_decode",
    )(page_table, *([cache_sel] * n_pages), qbd, gl, new_rows, win_prev, cmp_d, bcmp, bsel, bwin)
    return o.reshape(nb, NSA_HD, NSA_HEADS, tq).transpose(0, 3, 2, 1).reshape(nb * tq, NSA_HEADS * NSA_HD)


def _pad_cols(w, n):
    return jnp.pad(w, ((0, 0), (0, n - w.shape[1])))


def kernel(x_prompt, x_sample, state_dn_S, state_dn_conv, cache_kv, state_win_kv, page_table, norm_mix, norm_ffn, norm_kv, norm_final, ffn_w_in, ffn_w_out, dn_w_in, dn_conv_w, dn_A_log, dn_dt_bias, dn_out_norm, dn_w_out, nsa_w_kv, nsa_cmp_pos_w, nsa_w_cmp, nsa_w_in, nsa_w_out, rel_bias):
    B, T, D = x_prompt.shape
    NB, TS, _ = x_sample.shape
    Mp, Ms = B * T, NB * TS
    past = page_table.shape[1] * PAGE
    x = jnp.concatenate([x_prompt.reshape(Mp, D), x_sample.reshape(Ms, D)], axis=0)

    p_S, p_conv, s_S, s_conv = [], [], [], []
    for l in range(N_A_LAYERS):
        w_in = _pad_cols(dn_w_in[l], 4 * D + LANES).astype(BF16)
        proj = linear(x, w_in, norm_w=norm_mix[l], tn=(4 * D + LANES) // 3)
        o_p, S_p = dn_prompt(proj, B, T, dn_conv_w[l], dn_A_log[l], dn_dt_bias[l], dn_out_norm[l])
        o_s, S_s = dn_decode(proj[Mp:], NB, state_dn_conv[l], state_dn_S[l], dn_conv_w[l], dn_A_log[l],
                             dn_dt_bias[l], dn_out_norm[l])
        qkv_p = proj[:Mp, :DN_QKV].reshape(B, T, DN_QKV)
        qkv_s = proj[Mp:, :DN_QKV].reshape(NB, TS, DN_QKV)
        p_S.append(S_p)
        s_S.append(S_s)
        p_conv.append(qkv_p[:, T - (DN_CONV - 1):])
        s_conv.append(jnp.concatenate([state_dn_conv[l], qkv_s], axis=1)[:, TS:])
        o = jnp.concatenate([o_p, o_s], axis=0)
        x = linear(o, dn_w_out[l].astype(BF16), residual=x, tn=D)
        x = ffn(x, norm_ffn[l], ffn_w_in[l].astype(BF16), ffn_w_out[l].astype(BF16))

    W = NSA_KVH * NSA_HD
    rows = linear(x, nsa_w_kv.astype(BF16), norm_w=norm_kv, tn=6 * W)
    rows_p = rows[:Mp].reshape(B, T, 6, NSA_KVH, NSA_HD)
    rows_s = rows[Mp:].reshape(NB, TS, 6, NSA_KVH, NSA_HD)
    p_kv_rows = rows_p[:, :, :4]
    p_win_kv = rows_p[:, T - min(WINDOW, T):, 4:]
    s_kv_rows = rows_s[:, :, :4]
    s_win_kv = jnp.concatenate([state_win_kv, rows_s[:, :, 4:]], axis=1)[:, TS:]

    wlo, whi, wbd = _cmp_weights(nsa_cmp_pos_w, nsa_w_cmp)
    rows_bf = rows.astype(BF16)
    cmp_p = compress_prompt(rows, B, T, wlo, whi, wbd)
    attn_ops = attn_prompt_operands(rows_bf, cmp_p, B, T)
    new_rows = rows[Mp:].reshape(NB, TS, 6 * W)
    cache2 = cache_kv.reshape(cache_kv.shape[0], PAGE, 4 * W)
    cmp_d = compress_decode(cache2, page_table, new_rows, wlo, whi, wbd)
    win_prev = state_win_kv.reshape(NB, state_win_kv.shape[1], 2 * W)
    bias_c = bias_cmp_prompt(rel_bias, T)
    bands = bias_band(rel_bias)
    bias_d = bias_decode(rel_bias, past, TS)

    for j in range(N_B_LAYERS):
        l = N_A_LAYERS + j
        w_in = _pad_cols(nsa_w_in[j], D + LANES).astype(BF16)
        proj = linear(x, w_in, norm_w=norm_mix[l], tn=D + LANES)
        o_p = attn_prompt(proj, B, T, attn_ops, bias_c, bands)
        o_s = attn_decode(proj[Mp:], cache2, page_table, new_rows, win_prev, cmp_d, bias_d)
        o = jnp.concatenate([o_p, o_s], axis=0)
        x = linear(o, nsa_w_out[j].astype(BF16), residual=x, tn=D)
        x = ffn(x, norm_ffn[l], ffn_w_in[l].astype(BF16), ffn_w_out[l].astype(BF16))

    y = final_norm(x, norm_final)
    return (y[:Mp].reshape(B, T, D), y[Mp:].reshape(NB, TS, D),
            jnp.stack(p_S), jnp.stack(p_conv), p_kv_rows, p_win_kv,
            jnp.stack(s_S), jnp.stack(s_conv), s_kv_rows, s_win_kv)
```

```python
import functools
import math

import jax
import jax.numpy as jnp
from jax import lax
from jax.experimental import pallas as pl
from jax.experimental.pallas import tpu as pltpu

F32 = jnp.float32
BF16 = jnp.bfloat16

D_MODEL = 1024
N_A_LAYERS = 2
N_B_LAYERS = 2
NORM_EPS = 1e-6
DN_HEADS = 8
DN_HEAD = 128
DN_QKV = 3 * D_MODEL
DN_CONV = 4
DN_CHUNK = 64
NSA_HEADS = 16
NSA_HD = 64
NSA_KVH = 4
NSA_GROUP = 4
CMP_STRIDE = 16
CMP_BLOCK = 32
SEL_BLOCK = 64
N_SEL = 16
WINDOW = 512
PAGE = 128
REL_BUCKETS = 32
REL_MAX_EXACT = 16
REL_MAX_DIST = 1024
NEG = -1e30

V7X_VMEM_LIMIT = 56 * 1024 * 1024
LANES = 128


def _cparams(sem):
    return pltpu.CompilerParams(dimension_semantics=sem, vmem_limit_bytes=V7X_VMEM_LIMIT)


def _rms(x, w):
    ms = jnp.mean(x * x, axis=-1, keepdims=True)
    return x * lax.rsqrt(ms + NORM_EPS) * w


def _silu(x):
    return x * (1.0 / (1.0 + jnp.exp(-x)))


def _sigmoid(x):
    return 1.0 / (1.0 + jnp.exp(-x))


def _softplus(x):
    return jnp.maximum(x, 0.0) + jnp.log1p(jnp.exp(-jnp.abs(x)))


def _dot(a, b):
    return jnp.dot(a.astype(BF16), b.astype(BF16), preferred_element_type=F32)


def _dot_nt(a, b):
    return lax.dot_general(a.astype(BF16), b.astype(BF16), (((1,), (1,)), ((), ())),
                           preferred_element_type=F32)


def _bdot_dims(a, b, ca, cb):
    return lax.dot_general(a.astype(BF16), b.astype(BF16), (((ca,), (cb,)), ((0,), (0,))),
                           preferred_element_type=F32)


def _bdot(a, b):
    return _bdot_dims(a, b, 2, 1)


def _bdot_nt(a, b):
    return _bdot_dims(a, b, 2, 2)


def _bdot_tn(a, b):
    return _bdot_dims(a, b, 1, 1)


def _linear_body(*refs, has_norm, has_res, n_head_tiles):
    it = iter(refs)
    x_ref = next(it)
    xt_ref = next(it) if n_head_tiles else None
    nw_ref = next(it) if has_norm else None
    w_ref = next(it)
    res_ref = next(it) if has_res else None
    o_ref = next(it)
    xn_ref = next(it)

    def stage(src_ref):
        x = src_ref[...]
        if has_norm:
            x = _rms(x, nw_ref[...])
        xn_ref[...] = x.astype(BF16)

    first = pl.program_id(1) == 0
    if n_head_tiles:
        in_head = pl.program_id(0) < n_head_tiles
        pl.when(first & in_head)(lambda: stage(x_ref))
        pl.when(first & jnp.logical_not(in_head))(lambda: stage(xt_ref))
    else:
        pl.when(first)(lambda: stage(x_ref))

    acc = jnp.dot(xn_ref[...], w_ref[...], preferred_element_type=F32)
    if has_res:
        acc = acc + res_ref[...]
    o_ref[...] = acc


def linear(x, w, norm_w=None, residual=None, x_tail=None, tm=512, tn=None):
    K = x.shape[1]
    N = w.shape[1]
    tn = N if tn is None else tn
    n_head_tiles = 0
    M = x.shape[0]
    assert M % tm == 0 and N % tn == 0
    args, specs = [x], [pl.BlockSpec((tm, K), lambda i, j: (i, 0))]
    if x_tail is not None:
        n_head_tiles = M // tm
        assert x_tail.shape[0] % tm == 0
        M += x_tail.shape[0]
        specs = [pl.BlockSpec((tm, K), lambda i, j: (jnp.minimum(i, n_head_tiles - 1), 0)),
                 pl.BlockSpec((tm, K), lambda i, j: (jnp.maximum(i - n_head_tiles, 0), 0))]
        args.append(x_tail)
    has_norm, has_res = norm_w is not None, residual is not None
    if has_norm:
        args.append(norm_w.reshape(1, K))
        specs.append(pl.BlockSpec((1, K), lambda i, j: (0, 0)))
    args.append(w)
    specs.append(pl.BlockSpec((K, tn), lambda i, j: (0, j)))
    if has_res:
        args.append(residual)
        specs.append(pl.BlockSpec((tm, tn), lambda i, j: (i, j)))
    return pl.pallas_call(
        functools.partial(_linear_body, has_norm=has_norm, has_res=has_res, n_head_tiles=n_head_tiles),
        out_shape=jax.ShapeDtypeStruct((M, N), F32),
        grid=(M // tm, N // tn),
        in_specs=specs,
        out_specs=pl.BlockSpec((tm, tn), lambda i, j: (i, j)),
        scratch_shapes=[pltpu.VMEM((tm, K), BF16)],
        compiler_params=_cparams(("parallel", "arbitrary")),
        name="linear",
    )(*args)


def _ffn_body(x_ref, nw_ref, wg_ref, wu_ref, wo_ref, o_ref, xn_ref, acc_ref):
    f = pl.program_id(1)

    @pl.when(f == 0)
    def _():
        xn_ref[...] = _rms(x_ref[...], nw_ref[...]).astype(BF16)
        acc_ref[...] = jnp.zeros_like(acc_ref)

    xn = xn_ref[...]
    g = jnp.dot(xn, wg_ref[...], preferred_element_type=F32)
    u = jnp.dot(xn, wu_ref[...], preferred_element_type=F32)
    a = (_silu(g) * u).astype(BF16)
    acc_ref[...] += jnp.dot(a, wo_ref[...], preferred_element_type=F32)

    @pl.when(f == pl.num_programs(1) - 1)
    def _():
        o_ref[...] = x_ref[...] + acc_ref[...]


def ffn(x, norm_w, w_in, w_out, tm=512, tf=256):
    M, D = x.shape
    FF = w_out.shape[0]
    nf = FF // tf
    assert M % tm == 0 and FF % tf == 0
    return pl.pallas_call(
        _ffn_body,
        out_shape=jax.ShapeDtypeStruct((M, D), F32),
        grid=(M // tm, nf),
        in_specs=[
            pl.BlockSpec((tm, D), lambda i, f: (i, 0)),
            pl.BlockSpec((1, D), lambda i, f: (0, 0)),
            pl.BlockSpec((D, tf), lambda i, f: (0, f)),
            pl.BlockSpec((D, tf), lambda i, f: (0, f + nf)),
            pl.BlockSpec((tf, D), lambda i, f: (f, 0)),
        ],
        out_specs=pl.BlockSpec((tm, D), lambda i, f: (i, 0)),
        scratch_shapes=[pltpu.VMEM((tm, D), BF16), pltpu.VMEM((tm, D), F32)],
        compiler_params=_cparams(("parallel", "arbitrary")),
        name="ffn",
    )(x, norm_w.reshape(1, D), w_in, w_in, w_out)


def _final_norm_body(x_ref, w_ref, o_ref):
    o_ref[...] = _rms(x_ref[...], w_ref[...])


def final_norm(x, w, tm=512):
    M, D = x.shape
    return pl.pallas_call(
        _final_norm_body,
        out_shape=jax.ShapeDtypeStruct((M, D), F32),
        grid=(M // tm,),
        in_specs=[pl.BlockSpec((tm, D), lambda i: (i, 0)), pl.BlockSpec((1, D), lambda i: (0, 0))],
        out_specs=pl.BlockSpec((tm, D), lambda i: (i, 0)),
        compiler_params=_cparams(("parallel",)),
        name="final_norm",
    )(x, w.reshape(1, D))


def _l2n(x):
    return x * lax.rsqrt(jnp.sum(x * x, axis=-1, keepdims=True) + NORM_EPS)


def _gated_out(o, z, onorm):
    return _rms(o, onorm) * _silu(z)


def _dn_prompt_body(qkv_ref, z_ref, ab_ref, abT_ref, cw_ref, alr_ref, dtr_ref, alc_ref, dtc_ref, on_ref,
                    o_ref, s_out_ref, xbuf_ref, s_ref):
    n = pl.program_id(1)
    C = DN_CHUNK

    @pl.when(n == 0)
    def _():
        xbuf_ref[0:8, :] = jnp.zeros((8, DN_QKV), F32)
        s_ref[...] = jnp.zeros_like(s_ref)

    xbuf_ref[8:8 + C, :] = qkv_ref[...]
    y = xbuf_ref[5:5 + C, :] * cw_ref[0:1, :]
    for i in range(1, DN_CONV):
        y = y + xbuf_ref[5 + i:5 + i + C, :] * cw_ref[i:i + 1, :]
    xbuf_ref[0:8, :] = xbuf_ref[C:C + 8, :]
    y = _silu(y)

    ab = ab_ref[...]
    g8 = -jnp.exp(alr_ref[...]) * _softplus(ab[:, 0:DN_HEADS] + dtr_ref[...])
    beta8 = _sigmoid(ab[:, DN_HEADS:2 * DN_HEADS])
    abT = abT_ref[0, 0]
    g8T = -jnp.exp(alc_ref[...]) * _softplus(abT[0:DN_HEADS, :] + dtc_ref[...])

    ii = lax.broadcasted_iota(jnp.int32, (C, C), 0)
    jj = lax.broadcasted_iota(jnp.int32, (C, C), 1)
    incl = (ii >= jj)[None]
    strict = (ii > jj)[None]
    onorm = on_ref[...]
    H = DN_HEADS

    heads = lambda off: jnp.stack([y[:, off + h * DN_HEAD:off + (h + 1) * DN_HEAD] for h in range(H)], axis=0)
    q = _l2n(heads(0)) * (DN_HEAD ** -0.5)
    k = _l2n(heads(D_MODEL))
    v = heads(2 * D_MODEL)
    g_col = jnp.stack([g8[:, h:h + 1] for h in range(H)], axis=0)
    beta = jnp.stack([beta8[:, h:h + 1] for h in range(H)], axis=0)
    g_row = jnp.stack([g8T[h:h + 1, :] for h in range(H)], axis=0)
    G_col = jnp.sum(jnp.where(incl, g_row, 0.0), axis=2, keepdims=True)
    G_row = jnp.sum(jnp.where((ii <= jj)[None], g_col, 0.0), axis=1, keepdims=True)
    dec = jnp.where(incl, jnp.exp(jnp.where(incl, G_col - G_row, 0.0)), 0.0)
    A = jnp.where(strict, beta * dec * _bdot_nt(k, k), 0.0)
    X = -A
    Tm = X
    for _ in range(int(math.log2(C)) - 1):
        X = _bdot(X, X)
        Tm = Tm + X + _bdot(X, Tm)
    eG = jnp.exp(G_col)
    w = jnp.concatenate([beta * v, (beta * eG) * k], axis=2)
    w = w + _bdot(Tm, w)
    wv, wk = w[:, :, :DN_HEAD], w[:, :, DN_HEAD:]
    aqk = dec * _bdot_nt(q, k)
    qg = eG * q
    G_last = G_col[:, C - 1:C, :]
    kdec = jnp.exp(G_last - G_col) * k
    S = s_ref[...]
    ws = _bdot(jnp.concatenate([wk, qg], axis=1), S)
    U = wv - ws[:, :C]
    O = ws[:, C:] + _bdot(aqk, U)
    s_ref[...] = jnp.exp(G_last) * S + _bdot_tn(kdec, U)
    for h in range(H):
        sl = slice(h * DN_HEAD, (h + 1) * DN_HEAD)
        o_ref[:, sl] = _gated_out(O[h], z_ref[:, sl], onorm)

    @pl.when(n == pl.num_programs(1) - 1)
    def _():
        s_out_ref[0] = s_ref[...]


def dn_prompt(proj, n_seq, T, conv_w, a_log, dt_bias, out_norm):
    M = n_seq * T
    C = DN_CHUNK
    N = T // C
    abT = proj[:M, 4 * D_MODEL:4 * D_MODEL + 2 * DN_HEADS].reshape(n_seq, N, C, 2 * DN_HEADS).transpose(0, 1, 3, 2)
    row = lambda a: a.reshape(1, DN_HEADS)
    col = lambda a: a.reshape(DN_HEADS, 1)
    return pl.pallas_call(
        _dn_prompt_body,
        out_shape=[jax.ShapeDtypeStruct((M, D_MODEL), F32),
                   jax.ShapeDtypeStruct((n_seq, DN_HEADS, DN_HEAD, DN_HEAD), F32)],
        grid=(n_seq, N),
        in_specs=[
            pl.BlockSpec((C, DN_QKV), lambda b, n: (b * N + n, 0)),
            pl.BlockSpec((C, D_MODEL), lambda b, n: (b * N + n, 3)),
            pl.BlockSpec((C, LANES), lambda b, n: (b * N + n, 4 * D_MODEL // LANES)),
            pl.BlockSpec((1, 1, 2 * DN_HEADS, C), lambda b, n: (b, n, 0, 0)),
            pl.BlockSpec((DN_CONV, DN_QKV), lambda b, n: (0, 0)),
            pl.BlockSpec((1, DN_HEADS), lambda b, n: (0, 0)),
            pl.BlockSpec((1, DN_HEADS), lambda b, n: (0, 0)),
            pl.BlockSpec((DN_HEADS, 1), lambda b, n: (0, 0)),
            pl.BlockSpec((DN_HEADS, 1), lambda b, n: (0, 0)),
            pl.BlockSpec((1, DN_HEAD), lambda b, n: (0, 0)),
        ],
        out_specs=[pl.BlockSpec((C, D_MODEL), lambda b, n: (b * N + n, 0)),
                   pl.BlockSpec((1, DN_HEADS, DN_HEAD, DN_HEAD), lambda b, n: (b, 0, 0, 0))],
        scratch_shapes=[pltpu.VMEM((C + 8, DN_QKV), F32), pltpu.VMEM((DN_HEADS, DN_HEAD, DN_HEAD), F32)],
        compiler_params=_cparams(("parallel", "arbitrary")),
        name="dn_prompt",
    )(proj, proj, proj, abT, conv_w, row(a_log), row(dt_bias), col(a_log), col(dt_bias), out_norm.reshape(1, DN_HEAD))


def _dn_decode_body(proj_ref, cbuf_ref, s0_ref, cw_ref, alr_ref, dtr_ref, on_ref, *rest, out_layer):
    o_ref, s_out_ref, xbuf_ref, oacc_ref = rest[-4:]
    for l in range(s_out_ref.shape[0]):
        if l != out_layer:
            s_out_ref[l] = jnp.zeros(s_out_ref.shape[1:], F32)
    T = proj_ref.shape[1]
    x = proj_ref[0]
    xbuf_ref[8 - (DN_CONV - 1):8, :] = cbuf_ref[0]
    xbuf_ref[8:8 + T, :] = x[:, :DN_QKV]
    y = xbuf_ref[5:5 + T, :] * cw_ref[0:1, :]
    for i in range(1, DN_CONV):
        y = y + xbuf_ref[5 + i:5 + i + T, :] * cw_ref[i:i + 1, :]
    y = _silu(y)
    ab = x[:, 4 * D_MODEL:4 * D_MODEL + LANES]
    a8 = jnp.exp(-jnp.exp(alr_ref[...]) * _softplus(ab[:, 0:DN_HEADS] + dtr_ref[...]))
    beta8 = _sigmoid(ab[:, DN_HEADS:2 * DN_HEADS])
    ii = lax.broadcasted_iota(jnp.int32, (DN_HEAD, DN_HEAD), 0)
    jj = lax.broadcasted_iota(jnp.int32, (DN_HEAD, DN_HEAD), 1)
    eye = ii == jj

    def to_col(r):
        return jnp.sum(jnp.where(eye, r, 0.0), axis=1, keepdims=True)

    for h in range(DN_HEADS):
        sl = slice(h * DN_HEAD, (h + 1) * DN_HEAD)
        q = _l2n(y[:, sl]) * (DN_HEAD ** -0.5)
        k = _l2n(y[:, D_MODEL + h * DN_HEAD:D_MODEL + (h + 1) * DN_HEAD])
        v = y[:, 2 * D_MODEL + h * DN_HEAD:2 * D_MODEL + (h + 1) * DN_HEAD]
        S = s0_ref[0, 0, h]
        for t in range(T):
            k_col = to_col(k[t:t + 1, :])
            q_col = to_col(q[t:t + 1, :])
            a = a8[t:t + 1, h:h + 1]
            b = beta8[t:t + 1, h:h + 1]
            kS = jnp.sum(k_col * S, axis=0, keepdims=True)
            S = a * S + k_col * (b * (v[t:t + 1, :] - a * kS))
            oacc_ref[t:t + 1, sl] = jnp.sum(q_col * S, axis=0, keepdims=True)
        s_out_ref[out_layer, 0, h] = S
    onorm = on_ref[...]
    for h in range(DN_HEADS):
        sl = slice(h * DN_HEAD, (h + 1) * DN_HEAD)
        o_ref[0, :, sl] = _gated_out(oacc_ref[0:T, sl], x[:, DN_QKV + h * DN_HEAD:DN_QKV + (h + 1) * DN_HEAD], onorm)


def dn_decode(proj, n_seq, conv_buf, S_all, layer, S_new, conv_w, a_log, dt_bias, out_norm):
    M, W = proj.shape
    T = M // n_seq
    row = lambda a: a.reshape(1, DN_HEADS)
    state_spec = pl.BlockSpec((1, 1, DN_HEADS, DN_HEAD, DN_HEAD), lambda b: (layer, b, 0, 0, 0))
    args = [proj.reshape(n_seq, T, W), conv_buf, S_all, conv_w, row(a_log), row(dt_bias), out_norm.reshape(1, DN_HEAD)]
    in_specs = [
        pl.BlockSpec((1, T, W), lambda b: (b, 0, 0)),
        pl.BlockSpec((1, DN_CONV - 1, DN_QKV), lambda b: (b, 0, 0)),
        state_spec,
        pl.BlockSpec((DN_CONV, DN_QKV), lambda b: (0, 0)),
        pl.BlockSpec((1, DN_HEADS), lambda b: (0, 0)),
        pl.BlockSpec((1, DN_HEADS), lambda b: (0, 0)),
        pl.BlockSpec((1, DN_HEAD), lambda b: (0, 0)),
    ]
    if S_new is None:
        aliases, out_layer = {}, layer
        out_state_spec = pl.BlockSpec((S_all.shape[0], 1, DN_HEADS, DN_HEAD, DN_HEAD), lambda b: (0, b, 0, 0, 0))
    else:
        aliases, out_layer = {len(args): 1}, 0
        out_state_spec = state_spec
        args.append(S_new)
        in_specs.append(pl.BlockSpec(memory_space=pl.ANY))
    o, S = pl.pallas_call(
        functools.partial(_dn_decode_body, out_layer=out_layer),
        out_shape=[jax.ShapeDtypeStruct((n_seq, T, D_MODEL), F32), jax.ShapeDtypeStruct(S_all.shape, F32)],
        grid=(n_seq,),
        in_specs=in_specs,
        out_specs=[pl.BlockSpec((1, T, D_MODEL), lambda b: (b, 0, 0)), out_state_spec],
        scratch_shapes=[pltpu.VMEM((16, DN_QKV), F32), pltpu.VMEM((8, D_MODEL), F32)],
        input_output_aliases=aliases,
        compiler_params=_cparams(("parallel",)),
        name="dn_decode",
    )(*args)
    return o.reshape(M, D_MODEL), S


def _bucket_thresholds():
    thr, prev = [], REL_MAX_EXACT
    for d in range(REL_MAX_EXACT, REL_MAX_DIST + 1):
        val = min(REL_MAX_EXACT + int(math.log(d / REL_MAX_EXACT) / math.log(REL_MAX_DIST / REL_MAX_EXACT)
                                      * (REL_BUCKETS - REL_MAX_EXACT)), REL_BUCKETS - 1)
        thr += [d] * (val - prev)
        prev = val
    assert len(thr) == REL_BUCKETS - 1 - REL_MAX_EXACT
    return tuple(thr)


_BUCKET_THR = _bucket_thresholds()
TQ = 256
BAND_ROWS = 128
TK = 512
BAND_TOP = (REL_MAX_DIST + TK + LANES - 1) // LANES * LANES
BAND_W = BAND_TOP + max(TK, WINDOW + TQ)
LOG2E = 1.4426950408889634


def _bucket(d):
    n = jnp.maximum(d, 0)
    big = jnp.full(n.shape, REL_MAX_EXACT, jnp.int32)
    for t in _BUCKET_THR:
        big = big + (n >= t).astype(jnp.int32)
    return jnp.where(n < REL_MAX_EXACT, n, big)


def _bias_lookup(bucket, table_row):
    acc = jnp.zeros(bucket.shape, F32)
    for k in range(REL_BUCKETS):
        acc = acc + jnp.where(bucket == k, table_row(k), 0.0)
    return acc


def _bias_cmp_body(tab_ref, o_ref, *, n_cmp):
    q0 = pl.program_id(0) * TQ
    shp = o_ref.shape[1:]
    t = q0 + lax.broadcasted_iota(jnp.int32, shp, 0)
    j = lax.broadcasted_iota(jnp.int32, shp, 1)
    d = t - (j * CMP_STRIDE + CMP_BLOCK - 1)
    dead = (d < 0) | (j >= n_cmp)
    bucket = _bucket(d)
    for h in range(NSA_HEADS):
        o_ref[h] = jnp.where(dead, NEG, _bias_lookup(bucket, lambda k: tab_ref[k, h]))


def bias_cmp_prompt(rel_bias, T):
    n_sub = T // CMP_STRIDE
    return pl.pallas_call(
        functools.partial(_bias_cmp_body, n_cmp=n_sub - 1),
        out_shape=jax.ShapeDtypeStruct((NSA_HEADS, T, n_sub), F32),
        grid=(T // TQ,),
        in_specs=[pl.BlockSpec(memory_space=pltpu.SMEM)],
        out_specs=pl.BlockSpec((NSA_HEADS, TQ, n_sub), lambda i: (0, i, 0)),
        compiler_params=_cparams(("parallel",)),
        name="bias_cmp",
    )(rel_bias)


def _bias_band_body(tab_ref, o_ref, ow_ref):
    h = pl.program_id(0)
    shp = o_ref.shape[1:]
    d = BAND_TOP + lax.broadcasted_iota(jnp.int32, shp, 0) - lax.broadcasted_iota(jnp.int32, shp, 1)
    bias = LOG2E * _bias_lookup(_bucket(d), lambda k: tab_ref[k, h])
    o_ref[0] = jnp.where(d < 0, NEG, bias)
    ow_ref[0] = jnp.where((d < 0) | (d >= WINDOW), NEG, bias)


def bias_band(rel_bias):
    shape = jax.ShapeDtypeStruct((NSA_HEADS, BAND_ROWS, BAND_W), F32)
    spec = pl.BlockSpec((1, BAND_ROWS, BAND_W), lambda h: (h, 0, 0))
    return pl.pallas_call(
        _bias_band_body,
        out_shape=[shape, shape],
        grid=(NSA_HEADS,),
        in_specs=[pl.BlockSpec(memory_space=pltpu.SMEM)],
        out_specs=[spec, spec],
        compiler_params=_cparams(("parallel",)),
        name="bias_band",
    )(rel_bias)


def _bias_decode_body(tabc_ref, cmp_ref, sel_ref, win_ref, *, past, tq):
    shp = (PAGE, NSA_HEADS * tq)
    i = lax.broadcasted_iota(jnp.int32, shp, 0)
    c = lax.broadcasted_iota(jnp.int32, shp, 1)
    qpos = past + (c & (tq - 1))
    row = lambda k: tabc_ref[k:k + 1, :]

    def table(d, dead):
        return jnp.where(dead | (d < 0), NEG, _bias_lookup(_bucket(d), row))

    ic = lax.broadcasted_iota(jnp.int32, cmp_ref.shape, 0)
    qc = past + (lax.broadcasted_iota(jnp.int32, cmp_ref.shape, 1) & (tq - 1))
    cmp_ref[...] = table(qc - (ic * CMP_STRIDE + CMP_BLOCK - 1), ic < 0)
    n_pages = past // PAGE
    for p in range(n_pages):
        sel_ref[p] = table(qpos - (p * PAGE + i), i < 0)
    sel_ref[n_pages] = table(qpos - (past + i), i >= tq)
    n_wt = WINDOW // PAGE
    for t in range(n_wt):
        d = qpos - (past - WINDOW + t * PAGE + i)
        win_ref[t] = table(d, d >= WINDOW)
    win_ref[n_wt] = table(qpos - (past + i), i >= tq)


def bias_decode(rel_bias, past, tq):
    assert tq & (tq - 1) == 0
    tabc = jnp.repeat(rel_bias, tq, axis=1)
    n_pages = past // PAGE
    nc = NSA_HEADS * tq
    return pl.pallas_call(
        functools.partial(_bias_decode_body, past=past, tq=tq),
        out_shape=[jax.ShapeDtypeStruct((past // CMP_STRIDE, nc), F32),
                   jax.ShapeDtypeStruct((n_pages + 1, PAGE, nc), F32),
                   jax.ShapeDtypeStruct((WINDOW // PAGE + 1, PAGE, nc), F32)],
        name="bias_decode",
    )(tabc)


def _cmp_weights(cmp_pos_w, w_cmp):
    w = jnp.concatenate([cmp_pos_w[0].reshape(CMP_BLOCK, -1), cmp_pos_w[1].reshape(CMP_BLOCK, -1)], axis=1)
    blocks = w_cmp.reshape(2 * NSA_KVH, NSA_HD, NSA_HD)
    n = 2 * NSA_KVH
    wbd = (jnp.eye(n, dtype=F32)[:, None, :, None] * blocks[:, :, None, :]).reshape(n * NSA_HD, n * NSA_HD)
    return w[:CMP_STRIDE], w[CMP_STRIDE:], wbd.astype(BF16)


def _pool16(x, w):
    n = x.shape[0] // CMP_STRIDE
    return jnp.sum(x.reshape(n, CMP_STRIDE, x.shape[1]) * w[None], axis=1)


def _compress_prompt_body(x_ref, wlo_ref, whi_ref, wbd_ref, o_ref, lo_ref, hi_ref):
    T = x_ref.shape[0]
    n_sub = T // CMP_STRIDE
    step = 512
    for c in range(T // step):
        xs = x_ref[c * step:(c + 1) * step, :]
        r = slice(c * step // CMP_STRIDE, (c + 1) * step // CMP_STRIDE)
        lo_ref[r, :] = _pool16(xs, wlo_ref[...])
        hi_ref[r, :] = _pool16(xs, whi_ref[...])
    hi_ref[n_sub:n_sub + 8, :] = jnp.zeros((8, hi_ref.shape[1]), F32)
    blocks = lo_ref[...] + hi_ref[1:n_sub + 1, :]
    o_ref[...] = _dot(blocks, wbd_ref[...]).astype(BF16)


def compress_prompt(rows, n_seq, T, wlo, whi, wbd):
    n_sub = T // CMP_STRIDE
    W = 2 * NSA_KVH * NSA_HD
    return pl.pallas_call(
        _compress_prompt_body,
        out_shape=jax.ShapeDtypeStruct((n_seq * n_sub, W), BF16),
        grid=(n_seq,),
        in_specs=[pl.BlockSpec((T, W), lambda b: (b, 0)),
                  pl.BlockSpec((CMP_STRIDE, W), lambda b: (0, 0)),
                  pl.BlockSpec((CMP_STRIDE, W), lambda b: (0, 0)),
                  pl.BlockSpec((W, W), lambda b: (0, 0))],
        out_specs=pl.BlockSpec((n_sub, W), lambda b: (b, 0)),
        scratch_shapes=[pltpu.VMEM((n_sub, W), F32), pltpu.VMEM((n_sub + 8, W), F32)],
        compiler_params=_cparams(("parallel",)),
        name="compress_prompt",
    )(rows, wlo, whi, wbd)


def _masked_softmax(s, mask, axis):
    l = jnp.where(mask, s, NEG)
    m = jnp.max(l, axis=axis, keepdims=True)
    e = jnp.where(mask, jnp.exp(l - m), 0.0)
    return e / jnp.maximum(jnp.sum(e, axis=axis, keepdims=True), 1e-30)


def _split3(x):
    hi = x.astype(BF16)
    r = x - hi.astype(F32)
    mid = r.astype(BF16)
    lo = (r - mid.astype(F32)).astype(BF16)
    return hi, mid, lo


def _topk_mask(score, blk, n_pick, axis, removed):
    sel = jnp.zeros(score.shape, F32)
    s = score
    big = jnp.int32(1 << 20)
    for _ in range(n_pick):
        mx = jnp.max(s, axis=axis, keepdims=True)
        idx = jnp.min(jnp.where(s == mx, blk, big), axis=axis, keepdims=True)
        hit = blk == idx
        sel = jnp.where(hit, 1.0, sel)
        s = jnp.where(hit, removed, s)
    return sel


def _attn_prompt_body(q_ref, g_ref, ka_ref, vs_ref, kw_ref, vw_ref, kc_ref, vc_ref, bc_ref, band_ref, bandw_ref,
                      o_ref, qp_scr, qw_scr, qa_scr, s_scr, p_scr, m_scr, acc_scr, sc_scr, sw_scr, pw_scr):
    qb = pl.program_id(2)
    q0 = qb * TQ
    G = NSA_GROUP
    R = G * TQ
    HD = NSA_HD
    groups = [slice(g * TQ, (g + 1) * TQ) for g in range(G)]
    slabs = [(slice(g * TQ + h, g * TQ + h + BAND_ROWS), g, h) for g in range(G) for h in range(0, TQ, BAND_ROWS)]
    qblk = q_ref[...] * (HD ** -0.5)
    Q = jnp.concatenate([qblk[:, g * HD:(g + 1) * HD] for g in range(G)], axis=0)
    zeros = jnp.zeros((R, HD), F32)
    qp_scr[...] = jnp.concatenate([Q, zeros], axis=1).astype(BF16)
    qw_scr[...] = jnp.concatenate([Q * LOG2E, zeros], axis=1).astype(BF16)

    left = lax.broadcasted_iota(jnp.int32, (TQ, 2 * HD), 1) < HD

    def normalized(acc):
        tiles = []
        for g in range(0, G, 2):
            a0, a1 = acc[groups[g]], acc[groups[g + 1]]
            r0, r1 = pltpu.roll(a0, HD, 1), pltpu.roll(a1, HD, 1)
            num = jnp.where(left, a0, r1)
            den = jnp.where(left, r0, a1)
            tiles.append(num / jnp.maximum(den, 1e-30))
        return jnp.concatenate(tiles, axis=1)

    gates = _split3(_sigmoid(g_ref[0]))
    W = G * HD
    gi = lax.broadcasted_iota(jnp.int32, (3 * G, W), 0)
    gc = lax.broadcasted_iota(jnp.int32, (3 * G, W), 1) // HD

    def gate(branch):
        spread = (gi == 3 * gc + branch).astype(BF16)
        return sum(jnp.dot(part, spread, preferred_element_type=F32) for part in gates)

    WK = WINDOW + TQ
    kw0 = pl.multiple_of(jnp.maximum(q0 - WINDOW, 0), LANES)
    sw_scr[...] = _dot_nt(qw_scr[...], kw_ref[pl.ds(kw0, WK), :])
    for r, g, h in slabs:
        cw0 = pl.multiple_of(BAND_TOP - (q0 + h - kw0), LANES)
        s = sw_scr[r, :] + bandw_ref[g, :, pl.ds(cw0, WK)]
        m = jnp.maximum(jnp.max(s, axis=1, keepdims=True), 0.5 * NEG)
        pw_scr[r, :] = jnp.exp2(s - m).astype(BF16)
    o_ref[...] = normalized(jnp.dot(pw_scr[...], vw_ref[pl.ds(kw0, WK), :], preferred_element_type=F32)) * gate(2)

    n_sub = bc_ref.shape[2]
    n_blk = n_sub * CMP_STRIDE // SEL_BLOCK
    n_pick = min(N_SEL, n_blk)
    per = SEL_BLOCK // CMP_STRIDE
    bb = lax.broadcasted_iota(jnp.int32, (n_blk, n_sub), 0)
    mm = lax.broadcasted_iota(jnp.int32, (n_blk, n_sub), 1)
    pool = ((mm // per == bb).astype(F32) + ((mm + 1) // per == bb).astype(F32)).astype(BF16)
    s_scr[:, :n_sub] = _dot_nt(qp_scr[...], kc_ref[...])
    for h in range(0, TQ, BAND_ROWS):
        imp = None
        for g in range(G):
            r = slice(g * TQ + h, g * TQ + h + BAND_ROWS)
            s = s_scr[r, :n_sub] + bc_ref[g, h:h + BAND_ROWS, :]
            e = jnp.exp(s - jnp.maximum(jnp.max(s, axis=1, keepdims=True), 0.5 * NEG))
            p_scr[r, :n_sub] = e.astype(BF16)
            p = e / jnp.maximum(jnp.sum(e, axis=1, keepdims=True), 1e-30)
            imp = p if imp is None else imp + p
        sc_scr[:, h:h + BAND_ROWS] = sum(_dot_nt(pool, part) for part in _split3(imp))
    o_ref[...] += normalized(jnp.dot(p_scr[:, :n_sub], vc_ref[...], preferred_element_type=F32)) * gate(0)

    p_slc = sc_scr[...]
    blk = lax.broadcasted_iota(jnp.int32, (n_blk, TQ), 0)
    t = q0 + lax.broadcasted_iota(jnp.int32, (n_blk, TQ), 1)
    cur = t // SEL_BLOCK
    forced = (blk == 0) | (blk == cur) | (blk == cur - 1)
    score = jnp.where(forced, 1e4, jnp.where(blk * SEL_BLOCK <= t, p_slc, -1.0))
    sc_scr[...] = score
    beaten = jnp.zeros((n_blk, TQ), F32)
    for other in range(n_blk):
        row = sc_scr[other:other + 1, :]
        ge = jnp.where(row >= score, 1.0, 0.0)
        gt = jnp.where(row > score, 1.0, 0.0)
        beaten = beaten + jnp.where(blk > other, ge, gt)
    selneg = jnp.where(beaten < n_pick, 0.0, NEG)
    selneg_q = selneg.T
    if n_blk < HD:
        selneg_q = jnp.concatenate([selneg_q, jnp.zeros((TQ, HD - n_blk), F32)], axis=1)
    qa_scr[...] = jnp.concatenate([jnp.concatenate([Q[r] * LOG2E, selneg_q], axis=1) for r in groups],
                                  axis=0).astype(BF16)

    def soften(kt, pv):
        rowmax = []
        for r, g, h in slabs:
            c0 = pl.multiple_of(jnp.maximum(BAND_TOP - (q0 + h - kt * TK), 0), LANES)
            s = s_scr[r, :] + band_ref[g, :, pl.ds(c0, TK)]
            s_scr[r, :] = s
            rowmax.append(jnp.max(s, axis=1, keepdims=True))
        for i, (r, g, h) in enumerate(slabs):
            m_old = m_scr[r, :]
            m_new = jnp.maximum(m_old, rowmax[i])
            m_scr[r, :] = m_new
            a = acc_scr[r, :] if pv is None else acc_scr[r, :] + pv[r]
            acc_scr[r, :] = jnp.exp2(m_old - m_new) * a
            for c in range(TK // LANES):
                cs = slice(c * LANES, (c + 1) * LANES)
                p_scr[r, cs] = jnp.exp2(s_scr[r, cs] - m_new).astype(BF16)

    def sel_qk(kt):
        s_scr[...] = _dot_nt(qa_scr[...], ka_ref[pl.ds(pl.multiple_of(kt * TK, TK), TK), :])

    def sel_pv(kt):
        v = vs_ref[pl.ds(pl.multiple_of(kt * TK, TK), TK), :]
        return jnp.dot(p_scr[...], v, preferred_element_type=F32)

    m_scr[...] = jnp.full(m_scr.shape, 0.5 * NEG, F32)
    acc_scr[...] = jnp.zeros(acc_scr.shape, F32)
    sel_qk(0)
    soften(0, None)

    def step(kt, carry):
        pv = sel_pv(kt - 1)
        sel_qk(kt)
        soften(kt, pv)
        return carry

    n_kt = (q0 + TQ + TK - 1) // TK
    lax.fori_loop(1, n_kt, step, 0)
    o_ref[...] += normalized(acc_scr[...] + sel_pv(n_kt - 1)) * gate(1)


def attn_prompt_operands(rows_bf, cmp_p, n_seq, T):
    M = n_seq * T
    W = NSA_KVH * NSA_HD

    def per_head(x, aux):
        x = x.reshape(x.shape[0], NSA_KVH, NSA_HD)
        out = jnp.concatenate([x, jnp.broadcast_to(aux[:, None, :], x.shape)], axis=2)
        return out.reshape(x.shape[0], NSA_KVH * 2 * NSA_HD)

    blk_of_key = (jnp.arange(M, dtype=jnp.int32) % T) // SEL_BLOCK
    onehot = (blk_of_key[:, None] == jnp.arange(NSA_HD, dtype=jnp.int32)[None, :]).astype(BF16)
    zeros, ones = jnp.zeros((M, NSA_HD), BF16), jnp.ones((M, NSA_HD), BF16)
    kind = lambda i: rows_bf[:M, i * W:(i + 1) * W]
    nc = cmp_p.shape[0]
    return (per_head(kind(2), onehot), per_head(kind(3), ones), per_head(kind(4), zeros), per_head(kind(5), ones),
            per_head(cmp_p[:, :W], zeros[:nc]), per_head(cmp_p[:, W:], ones[:nc]))


def attn_prompt(proj, n_seq, T, operands, bias_c, bands):
    M = n_seq * T
    NQ = T // TQ
    n_sub = T // CMP_STRIDE
    n_blk = T // SEL_BLOCK
    assert n_blk <= NSA_HD and T % TK == 0 and T >= WINDOW + TQ and n_sub <= TK
    gl = proj[:M, NSA_HEADS * NSA_HD:NSA_HEADS * NSA_HD + 3 * NSA_HEADS]
    gl = gl.reshape(M, NSA_KVH, 3 * NSA_GROUP).transpose(1, 0, 2)
    pair = 2 * NSA_HD
    ks_a, vs_a, kw_a, vw_a, kc_a, vc_a = operands
    kv_spec = pl.BlockSpec((T, pair), lambda k, b, i: (b, k))
    cmp_spec = pl.BlockSpec((n_sub, pair), lambda k, b, i: (b, k))
    band_spec = pl.BlockSpec((NSA_GROUP, BAND_ROWS, BAND_W), lambda k, b, i: (k, 0, 0))
    WO = NSA_GROUP * NSA_HD
    R = NSA_GROUP * TQ
    WK = WINDOW + TQ
    band, bandw = bands
    return pl.pallas_call(
        _attn_prompt_body,
        out_shape=jax.ShapeDtypeStruct((M, NSA_HEADS * NSA_HD), F32),
        grid=(NSA_KVH, n_seq, NQ),
        in_specs=[
            pl.BlockSpec((TQ, WO), lambda k, b, i: (b * NQ + i, k)),
            pl.BlockSpec((1, TQ, 3 * NSA_GROUP), lambda k, b, i: (k, b * NQ + i, 0)),
            kv_spec, kv_spec, kv_spec, kv_spec,
            cmp_spec, cmp_spec,
            pl.BlockSpec((NSA_GROUP, TQ, n_sub), lambda k, b, i: (k, i, 0)),
            band_spec, band_spec,
        ],
        out_specs=pl.BlockSpec((TQ, WO), lambda k, b, i: (b * NQ + i, k)),
        scratch_shapes=[pltpu.VMEM((R, pair), BF16), pltpu.VMEM((R, pair), BF16), pltpu.VMEM((R, pair), BF16),
                        pltpu.VMEM((R, TK), F32), pltpu.VMEM((R, TK), BF16),
                        pltpu.VMEM((R, LANES), F32), pltpu.VMEM((R, pair), F32),
                        pltpu.VMEM((n_blk, TQ), F32),
                        pltpu.VMEM((R, WK), F32), pltpu.VMEM((R, WK), BF16)],
        compiler_params=_cparams(("parallel", "parallel", "arbitrary")),
        name="attn_prompt",
    )(proj, gl, ks_a, vs_a, kw_a, vw_a, kc_a, vc_a, bias_c, band, bandw)


def _compress_decode_body(pt_ref, *refs, n_pages):
    pages = refs[:n_pages]
    new_ref, wlo_ref, whi_ref, wbd_ref, o_ref = refs[n_pages:]
    per = PAGE // CMP_STRIDE
    n_sub = n_pages * per
    W = wlo_ref.shape[0]
    pos = lax.broadcasted_iota(jnp.int32, (2 * PAGE, n_sub), 0)
    blk = lax.broadcasted_iota(jnp.int32, (2 * PAGE, n_sub), 1)

    def pooled(x, p):
        sub = p * per + (pos % PAGE) // CMP_STRIDE
        place = (blk == jnp.where(pos < PAGE, sub, sub - 1)).astype(BF16)
        y = jnp.concatenate([x * wlo_ref[...], x * whi_ref[...]], axis=1).astype(BF16)
        return jnp.dot(y, place, preferred_element_type=F32)

    blocks = pooled(new_ref[0], n_pages)
    for p in range(n_pages):
        blocks = blocks + pooled(pages[p][0].reshape(W, PAGE), p)
    o_ref[0] = jnp.dot(wbd_ref[...], blocks.astype(BF16), preferred_element_type=F32).astype(BF16)


def compress_decode(cache_t, page_table, new_rows, wlo, whi, wbd):
    nb, n_pages = page_table.shape
    W = 2 * NSA_KVH * NSA_HD
    tq = new_rows.shape[1]
    assert tq <= CMP_STRIDE
    n_sub = n_pages * PAGE // CMP_STRIDE
    wlo_t = jnp.tile(wlo.T, (1, PAGE // CMP_STRIDE))
    whi_t = jnp.tile(whi.T, (1, PAGE // CMP_STRIDE))
    new_t = jnp.pad(new_rows[:, :, :W].transpose(0, 2, 1), ((0, 0), (0, 0), (0, PAGE - tq)))
    page_spec = lambda p: pl.BlockSpec((1, 2, W // 2, PAGE), lambda b, pt: (pt[b, p], 0, 0, 0))
    const = lambda shape: pl.BlockSpec(shape, lambda b, pt: (0,) * len(shape))
    return pl.pallas_call(
        functools.partial(_compress_decode_body, n_pages=n_pages),
        out_shape=jax.ShapeDtypeStruct((nb, W, n_sub), BF16),
        grid_spec=pltpu.PrefetchScalarGridSpec(
            num_scalar_prefetch=1,
            grid=(nb,),
            in_specs=[page_spec(p) for p in range(n_pages)] + [
                pl.BlockSpec((1, W, PAGE), lambda b, pt: (b, 0, 0)),
                const((W, PAGE)), const((W, PAGE)), const((W, W))],
            out_specs=pl.BlockSpec((1, W, n_sub), lambda b, pt: (b, 0, 0)),
        ),
        compiler_params=_cparams(("parallel",)),
        name="compress_decode",
    )(page_table, *([cache_t] * n_pages), new_t, wlo_t, whi_t, wbd.T)


def _attn_decode_body(pt_ref, *refs, n_pages, tq):
    pages = refs[:n_pages]
    q_ref, qt_ref, g_ref, new_ref, win_ref, cmp_ref, bcmp_ref, bsel_ref, bwin_ref, o_ref = refs[n_pages:]
    W = NSA_KVH * NSA_HD
    NC = NSA_HEADS * tq
    Qbd = q_ref[0]
    QbdT = qt_ref[0]
    past = n_pages * PAGE
    tn = (((0,), (0,)), ((), ()))

    def logits_t(kT):
        return lax.dot_general(kT.astype(BF16), QbdT, tn, preferred_element_type=F32)

    cm = cmp_ref[0]
    bc = bcmp_ref[...]
    p_c = _masked_softmax(logits_t(cm[:W]) + bc, bc > 0.5 * NEG, 0)
    o_cmp = jnp.dot(cm[W:], p_c.astype(BF16), preferred_element_type=F32)

    n_sub = cm.shape[1]
    per = SEL_BLOCK // CMP_STRIDE
    n_blk = past // SEL_BLOCK + 1
    nb_pad = (n_blk + 7) // 8 * 8
    ci = lax.broadcasted_iota(jnp.int32, (NC, NC), 0)
    cj = lax.broadcasted_iota(jnp.int32, (NC, NC), 1)
    gq = NSA_GROUP * tq
    same = ((ci // gq == cj // gq) & ((ci & (tq - 1)) == (cj & (tq - 1)))).astype(BF16)
    imp = sum(jnp.dot(part, same, preferred_element_type=F32) for part in _split3(p_c))
    bb = lax.broadcasted_iota(jnp.int32, (nb_pad, n_sub), 0)
    mm = lax.broadcasted_iota(jnp.int32, (nb_pad, n_sub), 1)
    pool = ((mm // per == bb).astype(F32) + ((mm + 1) // per == bb).astype(F32)).astype(BF16)
    p_slc = sum(jnp.dot(pool, part, preferred_element_type=F32) for part in _split3(imp))
    blk = lax.broadcasted_iota(jnp.int32, (nb_pad, NC), 0)
    qpos = past + (lax.broadcasted_iota(jnp.int32, (nb_pad, NC), 1) & (tq - 1))
    cur = qpos // SEL_BLOCK
    forced = (blk == 0) | (blk == cur) | (blk == cur - 1)
    score = jnp.where(forced, 1e4, jnp.where(blk * SEL_BLOCK <= qpos, p_slc, -1.0))
    score = jnp.where(blk < n_blk, score, -3.0)
    sel = _topk_mask(score, blk, min(N_SEL, n_blk), 0, -5.0)

    def flash_tile(carry, k, v, bias, mask, stored):
        m, l, acc = carry
        s = jnp.where(mask, (logits_t(k) if stored else _dot_nt(k, Qbd)) + bias, NEG)
        m_new = jnp.maximum(m, jnp.max(s, axis=0, keepdims=True))
        p = jnp.where(mask, jnp.exp(s - m_new), 0.0)
        alpha = jnp.exp(m - m_new)
        l = alpha * l + jnp.sum(p, axis=0, keepdims=True)
        if stored:
            pv = jnp.dot(v.astype(BF16), p.astype(BF16), preferred_element_type=F32)
        else:
            pv = lax.dot_general(v.astype(BF16), p.astype(BF16), tn, preferred_element_type=F32)
        return m_new, l, alpha * acc + pv

    init = (jnp.full((1, NC), NEG, F32), jnp.zeros((1, NC), F32), jnp.zeros((W, NC), F32))
    new = new_ref[0]
    pad = jnp.zeros((8 - tq, W), F32)
    new_tile = lambda kind: jnp.concatenate([new[:, kind * W:(kind + 1) * W], pad], axis=0)

    half = lax.broadcasted_iota(jnp.int32, (PAGE, NC), 0) < SEL_BLOCK
    carry = init
    for p in range(n_pages):
        pg = pages[p][0]
        b0 = p * (PAGE // SEL_BLOCK)
        picked = jnp.where(half, sel[b0:b0 + 1, :], sel[b0 + 1:b0 + 2, :]) > 0.5
        bias = bsel_ref[p]
        carry = flash_tile(carry, pg[0], pg[1], bias, picked & (bias > 0.5 * NEG), True)
    bias = bsel_ref[n_pages][0:8]
    carry = flash_tile(carry, new_tile(2), new_tile(3), bias, (sel[n_blk - 1:n_blk, :] > 0.5) & (bias > 0.5 * NEG),
                       False)
    o_sel = carry[2] / jnp.maximum(carry[1], 1e-30)

    carry = init
    for t in range(WINDOW // PAGE):
        ts = slice(t * PAGE, (t + 1) * PAGE)
        bias = bwin_ref[t]
        carry = flash_tile(carry, win_ref[0, 0, :, ts], win_ref[0, 1, :, ts], bias, bias > 0.5 * NEG, True)
    bias = bwin_ref[WINDOW // PAGE][0:8]
    carry = flash_tile(carry, new_tile(4), new_tile(5), bias, bias > 0.5 * NEG, False)
    o_win = carry[2] / jnp.maximum(carry[1], 1e-30)

    gt = _sigmoid(g_ref[0])
    o = o_cmp * gt[0:1] + o_sel * gt[1:2] + o_win * gt[2:3]
    kvh_of_col = lax.broadcasted_iota(jnp.int32, (NSA_HD, NC), 1) // gq
    out = jnp.zeros((NSA_HD, NC), F32)
    for k in range(NSA_KVH):
        out = out + jnp.where(kvh_of_col == k, o[k * NSA_HD:(k + 1) * NSA_HD, :], 0.0)
    o_ref[0] = out


def attn_decode(proj, cache_t, page_table, new_rows, win_t, cmp_d, bias_tabs):
    nb, n_pages = page_table.shape
    tq = new_rows.shape[1]
    W = NSA_KVH * NSA_HD
    NC = NSA_HEADS * tq
    q = proj[:, :NSA_HEADS * NSA_HD] * (NSA_HD ** -0.5)
    q = q.reshape(nb, tq, NSA_KVH, NSA_GROUP, NSA_HD).transpose(0, 2, 3, 1, 4)
    qbd = q[:, :, :, :, None, :] * jnp.eye(NSA_KVH, dtype=F32)[None, :, None, None, :, None]
    qbd = qbd.reshape(nb, NC, W).astype(BF16)
    qbd_t = qbd.transpose(0, 2, 1)
    gl = proj[:, NSA_HEADS * NSA_HD:NSA_HEADS * NSA_HD + 3 * NSA_HEADS]
    gl = gl.reshape(nb, tq, NSA_HEADS, 3).transpose(0, 3, 2, 1).reshape(nb, 3, NC)
    bcmp, bsel, bwin = bias_tabs
    page_spec = lambda p: pl.BlockSpec((1, 2, W, PAGE), lambda b, pt: (pt[b, p], 1, 0, 0))
    const = lambda shape: pl.BlockSpec(shape, lambda b, pt: (0,) * len(shape))
    per_b = lambda shape: pl.BlockSpec((1,) + shape, lambda b, pt: (b,) + (0,) * len(shape))
    o = pl.pallas_call(
        functools.partial(_attn_decode_body, n_pages=n_pages, tq=tq),
        out_shape=jax.ShapeDtypeStruct((nb, NSA_HD, NC), F32),
        grid_spec=pltpu.PrefetchScalarGridSpec(
            num_scalar_prefetch=1,
            grid=(nb,),
            in_specs=[page_spec(p) for p in range(n_pages)] + [
                per_b((NC, W)), per_b((W, NC)), per_b((3, NC)), per_b((tq, 6 * W)), per_b((2, W, WINDOW)),
                per_b((2 * W, cmp_d.shape[2])),
                const(bcmp.shape), const(bsel.shape), const(bwin.shape)],
            out_specs=per_b((NSA_HD, NC)),
        ),
        compiler_params=_cparams(("parallel",)),
        name="attn_decode",
    )(page_table, *([cache_t] * n_pages), qbd, qbd_t, gl, new_rows, win_t, cmp_d, bcmp, bsel, bwin)
    return o.reshape(nb, NSA_HD, NSA_HEADS, tq).transpose(0, 3, 2, 1).reshape(nb * tq, NSA_HEADS * NSA_HD)


def _pad_cols(w, n):
    return jnp.pad(w, ((0, 0), (0, n - w.shape[1])))


def kernel(x_prompt, x_sample, state_dn_S, state_dn_conv, cache_kv, state_win_kv, page_table, norm_mix, norm_ffn, norm_kv, norm_final, ffn_w_in, ffn_w_out, dn_w_in, dn_conv_w, dn_A_log, dn_dt_bias, dn_out_norm, dn_w_out, nsa_w_kv, nsa_cmp_pos_w, nsa_w_cmp, nsa_w_in, nsa_w_out, rel_bias):
    B, T, D = x_prompt.shape
    NB, TS, _ = x_sample.shape
    Mp, Ms = B * T, NB * TS
    past = page_table.shape[1] * PAGE
    x = jnp.concatenate([x_prompt.reshape(Mp, D), x_sample.reshape(Ms, D)], axis=0)

    p_S, p_conv, s_conv = [], [], []
    s_S = None
    for l in range(N_A_LAYERS):
        w_in = _pad_cols(dn_w_in[l], 4 * D + LANES).astype(BF16)
        proj = linear(x, w_in, norm_w=norm_mix[l], tn=(4 * D + LANES) // 3)
        o_p, S_p = dn_prompt(proj, B, T, dn_conv_w[l], dn_A_log[l], dn_dt_bias[l], dn_out_norm[l])
        o_s, s_S = dn_decode(proj[Mp:], NB, state_dn_conv[l], state_dn_S, l, s_S, dn_conv_w[l], dn_A_log[l],
                             dn_dt_bias[l], dn_out_norm[l])
        tail = DN_CONV - 1
        qkv_s = proj[Mp:, :DN_QKV].reshape(NB, TS, DN_QKV)
        p_S.append(S_p)
        p_conv.append(jnp.stack([proj[(b + 1) * T - tail:(b + 1) * T, :DN_QKV] for b in range(B)]))
        s_conv.append(jnp.concatenate([state_dn_conv[l], qkv_s], axis=1)[:, TS:])
        x = linear(o_p, dn_w_out[l].astype(BF16), residual=x, x_tail=o_s, tn=D)
        x = ffn(x, norm_ffn[l], ffn_w_in[l].astype(BF16), ffn_w_out[l].astype(BF16))

    W = NSA_KVH * NSA_HD
    rows = linear(x, nsa_w_kv.astype(BF16), norm_w=norm_kv, tn=6 * W)
    rows_p = rows[:Mp].reshape(B, T, 6, NSA_KVH, NSA_HD)
    rows_s = rows[Mp:].reshape(NB, TS, 6, NSA_KVH, NSA_HD)
    p_kv_rows = rows_p[:, :, :4]
    p_win_kv = rows_p[:, T - min(WINDOW, T):, 4:]
    s_kv_rows = rows_s[:, :, :4]
    s_win_kv = jnp.concatenate([state_win_kv, rows_s[:, :, 4:]], axis=1)[:, TS:]

    wlo, whi, wbd = _cmp_weights(nsa_cmp_pos_w, nsa_w_cmp)
    rows_bf = rows.astype(BF16)
    cmp_p = compress_prompt(rows, B, T, wlo, whi, wbd)
    attn_ops = attn_prompt_operands(rows_bf, cmp_p, B, T)
    new_rows = rows[Mp:].reshape(NB, TS, 6 * W)
    cache_t = cache_kv.transpose(0, 2, 3, 4, 1).reshape(cache_kv.shape[0], 4, W, PAGE)
    win_t = state_win_kv.transpose(0, 2, 3, 4, 1).reshape(NB, 2, W, state_win_kv.shape[1])
    cmp_d = compress_decode(cache_t, page_table, new_rows, wlo, whi, wbd)
    bias_c = bias_cmp_prompt(rel_bias, T)
    bands = bias_band(rel_bias)
    bias_d = bias_decode(rel_bias, past, TS)

    for j in range(N_B_LAYERS):
        l = N_A_LAYERS + j
        w_in = _pad_cols(nsa_w_in[j], D + LANES).astype(BF16)
        proj = linear(x, w_in, norm_w=norm_mix[l], tn=D + LANES)
        o_p = attn_prompt(proj, B, T, attn_ops, bias_c, bands)
        o_s = attn_decode(proj[Mp:], cache_t, page_table, new_rows, win_t, cmp_d, bias_d)
        x = linear(o_p, nsa_w_out[j].astype(BF16), residual=x, x_tail=o_s, tn=D)
        x = ffn(x, norm_ffn[l], ffn_w_in[l].astype(BF16), ffn_w_out[l].astype(BF16))

    y = final_norm(x, norm_final)
    return (y[:Mp].reshape(B, T, D), y[Mp:].reshape(NB, TS, D),
            jnp.stack(p_S), jnp.stack(p_conv), p_kv_rows, p_win_kv,
            s_S, jnp.stack(s_conv), s_kv_rows, s_win_kv)
```

```python
import functools
import math

import jax
import jax.numpy as jnp
from jax import lax
from jax.experimental import pallas as pl
from jax.experimental.pallas import tpu as pltpu

F32 = jnp.float32
BF16 = jnp.bfloat16

D_MODEL = 1024
N_A_LAYERS = 2
N_B_LAYERS = 2
NORM_EPS = 1e-6
DN_HEADS = 8
DN_HEAD = 128
DN_QKV = 3 * D_MODEL
DN_CONV = 4
DN_CHUNK = 64
NSA_HEADS = 16
NSA_HD = 64
NSA_KVH = 4
NSA_GROUP = 4
CMP_STRIDE = 16
CMP_BLOCK = 32
SEL_BLOCK = 64
N_SEL = 16
WINDOW = 512
PAGE = 128
REL_BUCKETS = 32
REL_MAX_EXACT = 16
REL_MAX_DIST = 1024
NEG = -1e30

V7X_VMEM_LIMIT = 56 * 1024 * 1024
LANES = 128


def _cparams(sem):
    return pltpu.CompilerParams(dimension_semantics=sem, vmem_limit_bytes=V7X_VMEM_LIMIT)


def _rms(x, w):
    ms = jnp.mean(x * x, axis=-1, keepdims=True)
    return x * lax.rsqrt(ms + NORM_EPS) * w


def _silu(x):
    return x * (1.0 / (1.0 + jnp.exp(-x)))


def _sigmoid(x):
    return 1.0 / (1.0 + jnp.exp(-x))


def _softplus(x):
    return jnp.maximum(x, 0.0) + jnp.log1p(jnp.exp(-jnp.abs(x)))


def _dot(a, b):
    return jnp.dot(a.astype(BF16), b.astype(BF16), preferred_element_type=F32)


def _dot_nt(a, b):
    return lax.dot_general(a.astype(BF16), b.astype(BF16), (((1,), (1,)), ((), ())),
                           preferred_element_type=F32)


def _bdot_dims(a, b, ca, cb):
    return lax.dot_general(a.astype(BF16), b.astype(BF16), (((ca,), (cb,)), ((0,), (0,))),
                           preferred_element_type=F32)


def _bdot(a, b):
    return _bdot_dims(a, b, 2, 1)


def _bdot_nt(a, b):
    return _bdot_dims(a, b, 2, 2)


def _bdot_tn(a, b):
    return _bdot_dims(a, b, 1, 1)


def _linear_body(*refs, has_norm, has_res, n_head_tiles):
    it = iter(refs)
    x_ref = next(it)
    xt_ref = next(it) if n_head_tiles else None
    nw_ref = next(it) if has_norm else None
    w_ref = next(it)
    res_ref = next(it) if has_res else None
    o_ref = next(it)
    xn_ref = next(it)

    def stage(src_ref):
        x = src_ref[...]
        if has_norm:
            x = _rms(x, nw_ref[...])
        xn_ref[...] = x.astype(BF16)

    first = pl.program_id(1) == 0
    if n_head_tiles:
        in_head = pl.program_id(0) < n_head_tiles
        pl.when(first & in_head)(lambda: stage(x_ref))
        pl.when(first & jnp.logical_not(in_head))(lambda: stage(xt_ref))
    else:
        pl.when(first)(lambda: stage(x_ref))

    acc = jnp.dot(xn_ref[...], w_ref[...], preferred_element_type=F32)
    if has_res:
        acc = acc + res_ref[...]
    o_ref[...] = acc


def linear(x, w, norm_w=None, residual=None, x_tail=None, tm=512, tn=None):
    K = x.shape[1]
    N = w.shape[1]
    tn = N if tn is None else tn
    n_head_tiles = 0
    M = x.shape[0]
    assert M % tm == 0 and N % tn == 0
    args, specs = [x], [pl.BlockSpec((tm, K), lambda i, j: (i, 0))]
    if x_tail is not None:
        n_head_tiles = M // tm
        assert x_tail.shape[0] % tm == 0
        M += x_tail.shape[0]
        specs = [pl.BlockSpec((tm, K), lambda i, j: (jnp.minimum(i, n_head_tiles - 1), 0)),
                 pl.BlockSpec((tm, K), lambda i, j: (jnp.maximum(i - n_head_tiles, 0), 0))]
        args.append(x_tail)
    has_norm, has_res = norm_w is not None, residual is not None
    if has_norm:
        args.append(norm_w.reshape(1, K))
        specs.append(pl.BlockSpec((1, K), lambda i, j: (0, 0)))
    args.append(w)
    specs.append(pl.BlockSpec((K, tn), lambda i, j: (0, j)))
    if has_res:
        args.append(residual)
        specs.append(pl.BlockSpec((tm, tn), lambda i, j: (i, j)))
    return pl.pallas_call(
        functools.partial(_linear_body, has_norm=has_norm, has_res=has_res, n_head_tiles=n_head_tiles),
        out_shape=jax.ShapeDtypeStruct((M, N), F32),
        grid=(M // tm, N // tn),
        in_specs=specs,
        out_specs=pl.BlockSpec((tm, tn), lambda i, j: (i, j)),
        scratch_shapes=[pltpu.VMEM((tm, K), BF16)],
        compiler_params=_cparams(("parallel", "arbitrary")),
        name="linear",
    )(*args)


def _shared_rows_body(x_ref, nw_ref, w_ref, rows_ref, rows_bf_ref, kvt_ref, wint_ref, *, n_head_tiles):
    acc = jnp.dot(_rms(x_ref[...], nw_ref[...]).astype(BF16), w_ref[...], preferred_element_type=F32)
    rows_ref[...] = acc
    rows_bf_ref[...] = acc.astype(BF16)

    @pl.when(pl.program_id(0) < n_head_tiles)
    def _():
        n_kv = kvt_ref.shape[1]
        for c in range(0, acc.shape[1], LANES):
            dst = kvt_ref.at[0, c:c + LANES] if c < n_kv else wint_ref.at[0, c - n_kv:c - n_kv + LANES]
            dst[...] = acc[:, c:c + LANES].T


def shared_rows(x, norm_w, w, n_seq, T, n_kv, tm=512):
    M, K = x.shape
    N = w.shape[1]
    n_head_tiles = n_seq * T // tm
    per_seq = T // tm
    assert M % tm == 0 and T % tm == 0 and n_kv % LANES == 0 and N % LANES == 0
    seq_block = lambda i: (jnp.minimum(i, n_head_tiles - 1) // per_seq, 0, jnp.minimum(i, n_head_tiles - 1) % per_seq)
    return pl.pallas_call(
        functools.partial(_shared_rows_body, n_head_tiles=n_head_tiles),
        out_shape=[jax.ShapeDtypeStruct((M, N), F32), jax.ShapeDtypeStruct((M, N), BF16),
                   jax.ShapeDtypeStruct((n_seq, n_kv, T), F32), jax.ShapeDtypeStruct((n_seq, N - n_kv, T), F32)],
        grid=(M // tm,),
        in_specs=[pl.BlockSpec((tm, K), lambda i: (i, 0)), pl.BlockSpec((1, K), lambda i: (0, 0)),
                  pl.BlockSpec((K, N), lambda i: (0, 0))],
        out_specs=[pl.BlockSpec((tm, N), lambda i: (i, 0)), pl.BlockSpec((tm, N), lambda i: (i, 0)),
                   pl.BlockSpec((1, n_kv, tm), seq_block), pl.BlockSpec((1, N - n_kv, tm), seq_block)],
        compiler_params=_cparams(("arbitrary",)),
        name="shared_rows",
    )(x, norm_w.reshape(1, K), w)


def _ffn_body(x_ref, nw_ref, wg_ref, wu_ref, wo_ref, o_ref, xn_ref, acc_ref):
    f = pl.program_id(1)

    @pl.when(f == 0)
    def _():
        xn_ref[...] = _rms(x_ref[...], nw_ref[...]).astype(BF16)
        acc_ref[...] = jnp.zeros_like(acc_ref)

    xn = xn_ref[...]
    g = jnp.dot(xn, wg_ref[...], preferred_element_type=F32)
    u = jnp.dot(xn, wu_ref[...], preferred_element_type=F32)
    a = (_silu(g) * u).astype(BF16)
    acc_ref[...] += jnp.dot(a, wo_ref[...], preferred_element_type=F32)

    @pl.when(f == pl.num_programs(1) - 1)
    def _():
        o_ref[...] = x_ref[...] + acc_ref[...]


def ffn(x, norm_w, w_in, w_out, tm=512, tf=256):
    M, D = x.shape
    FF = w_out.shape[0]
    nf = FF // tf
    assert M % tm == 0 and FF % tf == 0
    return pl.pallas_call(
        _ffn_body,
        out_shape=jax.ShapeDtypeStruct((M, D), F32),
        grid=(M // tm, nf),
        in_specs=[
            pl.BlockSpec((tm, D), lambda i, f: (i, 0)),
            pl.BlockSpec((1, D), lambda i, f: (0, 0)),
            pl.BlockSpec((D, tf), lambda i, f: (0, f)),
            pl.BlockSpec((D, tf), lambda i, f: (0, f + nf)),
            pl.BlockSpec((tf, D), lambda i, f: (f, 0)),
        ],
        out_specs=pl.BlockSpec((tm, D), lambda i, f: (i, 0)),
        scratch_shapes=[pltpu.VMEM((tm, D), BF16), pltpu.VMEM((tm, D), F32)],
        compiler_params=_cparams(("parallel", "arbitrary")),
        name="ffn",
    )(x, norm_w.reshape(1, D), w_in, w_in, w_out)


def _final_norm_body(x_ref, w_ref, head_ref, tail_ref, *, n_head_tiles):
    y = _rms(x_ref[...], w_ref[...])
    in_head = pl.program_id(0) < n_head_tiles

    @pl.when(in_head)
    def _():
        head_ref[...] = y

    @pl.when(jnp.logical_not(in_head))
    def _():
        tail_ref[...] = y


def final_norm(x, w, m_head, tm=512):
    M, D = x.shape
    n_head_tiles = m_head // tm
    assert m_head % tm == 0 and (M - m_head) % tm == 0 and 0 < m_head < M
    return pl.pallas_call(
        functools.partial(_final_norm_body, n_head_tiles=n_head_tiles),
        out_shape=[jax.ShapeDtypeStruct((m_head, D), F32), jax.ShapeDtypeStruct((M - m_head, D), F32)],
        grid=(M // tm,),
        in_specs=[pl.BlockSpec((tm, D), lambda i: (i, 0)), pl.BlockSpec((1, D), lambda i: (0, 0))],
        out_specs=[pl.BlockSpec((tm, D), lambda i: (jnp.minimum(i, n_head_tiles - 1), 0)),
                   pl.BlockSpec((tm, D), lambda i: (jnp.maximum(i - n_head_tiles, 0), 0))],
        compiler_params=_cparams(("arbitrary",)),
        name="final_norm",
    )(x, w.reshape(1, D))


def _l2n(x):
    return x * lax.rsqrt(jnp.sum(x * x, axis=-1, keepdims=True) + NORM_EPS)


def _gated_out(o, z, onorm):
    return _rms(o, onorm) * _silu(z)


def _dn_prompt_body(qkv_ref, z_ref, ab_ref, abT_ref, cw_ref, alr_ref, dtr_ref, alc_ref, dtc_ref, on_ref,
                    o_ref, s_out_ref, xbuf_ref, s_ref):
    n = pl.program_id(1)
    C = DN_CHUNK

    @pl.when(n == 0)
    def _():
        xbuf_ref[0:8, :] = jnp.zeros((8, DN_QKV), F32)
        s_ref[...] = jnp.zeros_like(s_ref)

    xbuf_ref[8:8 + C, :] = qkv_ref[...]
    y = xbuf_ref[5:5 + C, :] * cw_ref[0:1, :]
    for i in range(1, DN_CONV):
        y = y + xbuf_ref[5 + i:5 + i + C, :] * cw_ref[i:i + 1, :]
    xbuf_ref[0:8, :] = xbuf_ref[C:C + 8, :]
    y = _silu(y)

    ab = ab_ref[...]
    g8 = -jnp.exp(alr_ref[...]) * _softplus(ab[:, 0:DN_HEADS] + dtr_ref[...])
    beta8 = _sigmoid(ab[:, DN_HEADS:2 * DN_HEADS])
    abT = abT_ref[0, 0]
    g8T = -jnp.exp(alc_ref[...]) * _softplus(abT[0:DN_HEADS, :] + dtc_ref[...])

    ii = lax.broadcasted_iota(jnp.int32, (C, C), 0)
    jj = lax.broadcasted_iota(jnp.int32, (C, C), 1)
    incl = (ii >= jj)[None]
    strict = (ii > jj)[None]
    onorm = on_ref[...]
    H = DN_HEADS

    heads = lambda off: jnp.stack([y[:, off + h * DN_HEAD:off + (h + 1) * DN_HEAD] for h in range(H)], axis=0)
    q = _l2n(heads(0)) * (DN_HEAD ** -0.5)
    k = _l2n(heads(D_MODEL))
    v = heads(2 * D_MODEL)
    g_col = jnp.stack([g8[:, h:h + 1] for h in range(H)], axis=0)
    beta = jnp.stack([beta8[:, h:h + 1] for h in range(H)], axis=0)
    g_row = jnp.stack([g8T[h:h + 1, :] for h in range(H)], axis=0)
    G_col = jnp.sum(jnp.where(incl, g_row, 0.0), axis=2, keepdims=True)
    G_row = jnp.sum(jnp.where((ii <= jj)[None], g_col, 0.0), axis=1, keepdims=True)
    dec = jnp.where(incl, jnp.exp(jnp.where(incl, G_col - G_row, 0.0)), 0.0)
    A = jnp.where(strict, beta * dec * _bdot_nt(k, k), 0.0)
    X = -A
    Tm = X
    for _ in range(int(math.log2(C)) - 1):
        X = _bdot(X, X)
        Tm = Tm + X + _bdot(X, Tm)
    eG = jnp.exp(G_col)
    w = jnp.concatenate([beta * v, (beta * eG) * k], axis=2)
    w = w + _bdot(Tm, w)
    wv, wk = w[:, :, :DN_HEAD], w[:, :, DN_HEAD:]
    aqk = dec * _bdot_nt(q, k)
    qg = eG * q
    G_last = G_col[:, C - 1:C, :]
    kdec = jnp.exp(G_last - G_col) * k
    S = s_ref[...]
    ws = _bdot(jnp.concatenate([wk, qg], axis=1), S)
    U = wv - ws[:, :C]
    O = ws[:, C:] + _bdot(aqk, U)
    s_ref[...] = jnp.exp(G_last) * S + _bdot_tn(kdec, U)
    for h in range(H):
        sl = slice(h * DN_HEAD, (h + 1) * DN_HEAD)
        o_ref[:, sl] = _gated_out(O[h], z_ref[:, sl], onorm)

    @pl.when(n == pl.num_programs(1) - 1)
    def _():
        s_out_ref[0] = s_ref[...]


def dn_prompt(proj, n_seq, T, conv_w, a_log, dt_bias, out_norm):
    M = n_seq * T
    C = DN_CHUNK
    N = T // C
    abT = proj[:M, 4 * D_MODEL:4 * D_MODEL + 2 * DN_HEADS].reshape(n_seq, N, C, 2 * DN_HEADS).transpose(0, 1, 3, 2)
    row = lambda a: a.reshape(1, DN_HEADS)
    col = lambda a: a.reshape(DN_HEADS, 1)
    return pl.pallas_call(
        _dn_prompt_body,
        out_shape=[jax.ShapeDtypeStruct((M, D_MODEL), F32),
                   jax.ShapeDtypeStruct((n_seq, DN_HEADS, DN_HEAD, DN_HEAD), F32)],
        grid=(n_seq, N),
        in_specs=[
            pl.BlockSpec((C, DN_QKV), lambda b, n: (b * N + n, 0)),
            pl.BlockSpec((C, D_MODEL), lambda b, n: (b * N + n, 3)),
            pl.BlockSpec((C, LANES), lambda b, n: (b * N + n, 4 * D_MODEL // LANES)),
            pl.BlockSpec((1, 1, 2 * DN_HEADS, C), lambda b, n: (b, n, 0, 0)),
            pl.BlockSpec((DN_CONV, DN_QKV), lambda b, n: (0, 0)),
            pl.BlockSpec((1, DN_HEADS), lambda b, n: (0, 0)),
            pl.BlockSpec((1, DN_HEADS), lambda b, n: (0, 0)),
            pl.BlockSpec((DN_HEADS, 1), lambda b, n: (0, 0)),
            pl.BlockSpec((DN_HEADS, 1), lambda b, n: (0, 0)),
            pl.BlockSpec((1, DN_HEAD), lambda b, n: (0, 0)),
        ],
        out_specs=[pl.BlockSpec((C, D_MODEL), lambda b, n: (b * N + n, 0)),
                   pl.BlockSpec((1, DN_HEADS, DN_HEAD, DN_HEAD), lambda b, n: (b, 0, 0, 0))],
        scratch_shapes=[pltpu.VMEM((C + 8, DN_QKV), F32), pltpu.VMEM((DN_HEADS, DN_HEAD, DN_HEAD), F32)],
        compiler_params=_cparams(("parallel", "arbitrary")),
        name="dn_prompt",
    )(proj, proj, proj, abT, conv_w, row(a_log), row(dt_bias), col(a_log), col(dt_bias), out_norm.reshape(1, DN_HEAD))


def _dn_decode_body(proj_ref, cbuf_ref, s0_ref, cw_ref, alr_ref, dtr_ref, on_ref, *rest, out_layer):
    o_ref, s_out_ref, xbuf_ref, oacc_ref = rest[-4:]
    for l in range(s_out_ref.shape[0]):
        if l != out_layer:
            s_out_ref[l] = jnp.zeros(s_out_ref.shape[1:], F32)
    T = proj_ref.shape[1]
    x = proj_ref[0]
    xbuf_ref[8 - (DN_CONV - 1):8, :] = cbuf_ref[0]
    xbuf_ref[8:8 + T, :] = x[:, :DN_QKV]
    y = xbuf_ref[5:5 + T, :] * cw_ref[0:1, :]
    for i in range(1, DN_CONV):
        y = y + xbuf_ref[5 + i:5 + i + T, :] * cw_ref[i:i + 1, :]
    y = _silu(y)
    ab = x[:, 4 * D_MODEL:4 * D_MODEL + LANES]
    a8 = jnp.exp(-jnp.exp(alr_ref[...]) * _softplus(ab[:, 0:DN_HEADS] + dtr_ref[...]))
    beta8 = _sigmoid(ab[:, DN_HEADS:2 * DN_HEADS])
    ii = lax.broadcasted_iota(jnp.int32, (DN_HEAD, DN_HEAD), 0)
    jj = lax.broadcasted_iota(jnp.int32, (DN_HEAD, DN_HEAD), 1)
    eye = ii == jj

    def to_col(r):
        return jnp.sum(jnp.where(eye, r, 0.0), axis=1, keepdims=True)

    heads = [slice(h * DN_HEAD, (h + 1) * DN_HEAD) for h in range(DN_HEADS)]
    q = [_l2n(y[:, sl]) * (DN_HEAD ** -0.5) for sl in heads]
    k = [_l2n(y[:, D_MODEL + h * DN_HEAD:D_MODEL + (h + 1) * DN_HEAD]) for h in range(DN_HEADS)]
    v = [y[:, 2 * D_MODEL + h * DN_HEAD:2 * D_MODEL + (h + 1) * DN_HEAD] for h in range(DN_HEADS)]
    for t in range(T):
        for h, sl in enumerate(heads):
            S = s0_ref[0, 0, h] if t == 0 else s_out_ref[out_layer, 0, h]
            k_col = to_col(k[h][t:t + 1, :])
            q_col = to_col(q[h][t:t + 1, :])
            a = a8[t:t + 1, h:h + 1]
            b = beta8[t:t + 1, h:h + 1]
            kS = jnp.sum(k_col * S, axis=0, keepdims=True)
            S = a * S + k_col * (b * (v[h][t:t + 1, :] - a * kS))
            s_out_ref[out_layer, 0, h] = S
            oacc_ref[t:t + 1, sl] = jnp.sum(q_col * S, axis=0, keepdims=True)
    onorm = on_ref[...]
    for h in range(DN_HEADS):
        sl = slice(h * DN_HEAD, (h + 1) * DN_HEAD)
        o_ref[0, :, sl] = _gated_out(oacc_ref[0:T, sl], x[:, DN_QKV + h * DN_HEAD:DN_QKV + (h + 1) * DN_HEAD], onorm)


def dn_decode(proj, n_seq, conv_buf, S_all, layer, S_new, conv_w, a_log, dt_bias, out_norm):
    M, W = proj.shape
    T = M // n_seq
    row = lambda a: a.reshape(1, DN_HEADS)
    state_spec = pl.BlockSpec((1, 1, DN_HEADS, DN_HEAD, DN_HEAD), lambda b: (layer, b, 0, 0, 0))
    args = [proj.reshape(n_seq, T, W), conv_buf, S_all, conv_w, row(a_log), row(dt_bias), out_norm.reshape(1, DN_HEAD)]
    in_specs = [
        pl.BlockSpec((1, T, W), lambda b: (b, 0, 0)),
        pl.BlockSpec((1, DN_CONV - 1, DN_QKV), lambda b: (b, 0, 0)),
        state_spec,
        pl.BlockSpec((DN_CONV, DN_QKV), lambda b: (0, 0)),
        pl.BlockSpec((1, DN_HEADS), lambda b: (0, 0)),
        pl.BlockSpec((1, DN_HEADS), lambda b: (0, 0)),
        pl.BlockSpec((1, DN_HEAD), lambda b: (0, 0)),
    ]
    if S_new is None:
        aliases, out_layer = {}, layer
        out_state_spec = pl.BlockSpec((S_all.shape[0], 1, DN_HEADS, DN_HEAD, DN_HEAD), lambda b: (0, b, 0, 0, 0))
    else:
        aliases, out_layer = {len(args): 1}, 0
        out_state_spec = state_spec
        args.append(S_new)
        in_specs.append(pl.BlockSpec(memory_space=pl.ANY))
    o, S = pl.pallas_call(
        functools.partial(_dn_decode_body, out_layer=out_layer),
        out_shape=[jax.ShapeDtypeStruct((n_seq, T, D_MODEL), F32), jax.ShapeDtypeStruct(S_all.shape, F32)],
        grid=(n_seq,),
        in_specs=in_specs,
        out_specs=[pl.BlockSpec((1, T, D_MODEL), lambda b: (b, 0, 0)), out_state_spec],
        scratch_shapes=[pltpu.VMEM((16, DN_QKV), F32), pltpu.VMEM((8, D_MODEL), F32)],
        input_output_aliases=aliases,
        compiler_params=_cparams(("parallel",)),
        name="dn_decode",
    )(*args)
    return o.reshape(M, D_MODEL), S


def _bucket_thresholds():
    thr, prev = [], REL_MAX_EXACT
    for d in range(REL_MAX_EXACT, REL_MAX_DIST + 1):
        val = min(REL_MAX_EXACT + int(math.log(d / REL_MAX_EXACT) / math.log(REL_MAX_DIST / REL_MAX_EXACT)
                                      * (REL_BUCKETS - REL_MAX_EXACT)), REL_BUCKETS - 1)
        thr += [d] * (val - prev)
        prev = val
    assert len(thr) == REL_BUCKETS - 1 - REL_MAX_EXACT
    return tuple(thr)


_BUCKET_THR = _bucket_thresholds()
TQ = 256
BAND_ROWS = 128
TK = 512
BAND_TOP = (REL_MAX_DIST + TK + LANES - 1) // LANES * LANES
BAND_W = BAND_TOP + max(TK, WINDOW + TQ)
LOG2E = 1.4426950408889634


def _bucket(d):
    n = jnp.maximum(d, 0)
    big = jnp.full(n.shape, REL_MAX_EXACT, jnp.int32)
    for t in _BUCKET_THR:
        big = big + (n >= t).astype(jnp.int32)
    return jnp.where(n < REL_MAX_EXACT, n, big)


def _bias_lookup(bucket, table_row):
    acc = jnp.zeros(bucket.shape, F32)
    for k in range(REL_BUCKETS):
        acc = acc + jnp.where(bucket == k, table_row(k), 0.0)
    return acc


def _bias_cmp_body(tab_ref, o_ref, *, n_cmp):
    q0 = pl.program_id(0) * TQ
    shp = o_ref.shape[1:]
    t = q0 + lax.broadcasted_iota(jnp.int32, shp, 0)
    j = lax.broadcasted_iota(jnp.int32, shp, 1)
    d = t - (j * CMP_STRIDE + CMP_BLOCK - 1)
    dead = (d < 0) | (j >= n_cmp)
    bucket = _bucket(d)
    for h in range(NSA_HEADS):
        o_ref[h] = jnp.where(dead, NEG, _bias_lookup(bucket, lambda k: tab_ref[k, h]))


def bias_cmp_prompt(rel_bias, T):
    n_sub = T // CMP_STRIDE
    return pl.pallas_call(
        functools.partial(_bias_cmp_body, n_cmp=n_sub - 1),
        out_shape=jax.ShapeDtypeStruct((NSA_HEADS, T, n_sub), F32),
        grid=(T // TQ,),
        in_specs=[pl.BlockSpec(memory_space=pltpu.SMEM)],
        out_specs=pl.BlockSpec((NSA_HEADS, TQ, n_sub), lambda i: (0, i, 0)),
        compiler_params=_cparams(("parallel",)),
        name="bias_cmp",
    )(rel_bias)


def _bias_band_body(tab_ref, o_ref, ow_ref):
    h = pl.program_id(0)
    shp = o_ref.shape[1:]
    d = BAND_TOP + lax.broadcasted_iota(jnp.int32, shp, 0) - lax.broadcasted_iota(jnp.int32, shp, 1)
    bias = LOG2E * _bias_lookup(_bucket(d), lambda k: tab_ref[k, h])
    o_ref[0] = jnp.where(d < 0, NEG, bias)
    ow_ref[0] = jnp.where((d < 0) | (d >= WINDOW), NEG, bias)


def bias_band(rel_bias):
    shape = jax.ShapeDtypeStruct((NSA_HEADS, BAND_ROWS, BAND_W), F32)
    spec = pl.BlockSpec((1, BAND_ROWS, BAND_W), lambda h: (h, 0, 0))
    return pl.pallas_call(
        _bias_band_body,
        out_shape=[shape, shape],
        grid=(NSA_HEADS,),
        in_specs=[pl.BlockSpec(memory_space=pltpu.SMEM)],
        out_specs=[spec, spec],
        compiler_params=_cparams(("parallel",)),
        name="bias_band",
    )(rel_bias)


def _bias_decode_body(tabc_ref, cmp_ref, sel_ref, win_ref, *, past, tq):
    shp = (PAGE, NSA_HEADS * tq)
    i = lax.broadcasted_iota(jnp.int32, shp, 0)
    c = lax.broadcasted_iota(jnp.int32, shp, 1)
    qpos = past + (c & (tq - 1))
    row = lambda k: tabc_ref[k:k + 1, :]

    def table(d, dead):
        return jnp.where(dead | (d < 0), NEG, _bias_lookup(_bucket(d), row))

    ic = lax.broadcasted_iota(jnp.int32, cmp_ref.shape, 0)
    qc = past + (lax.broadcasted_iota(jnp.int32, cmp_ref.shape, 1) & (tq - 1))
    cmp_ref[...] = table(qc - (ic * CMP_STRIDE + CMP_BLOCK - 1), ic < 0)
    n_pages = past // PAGE
    for p in range(n_pages):
        sel_ref[p] = table(qpos - (p * PAGE + i), i < 0)
    sel_ref[n_pages] = table(qpos - (past + i), i >= tq)
    n_wt = WINDOW // PAGE
    for t in range(n_wt):
        d = qpos - (past - WINDOW + t * PAGE + i)
        win_ref[t] = table(d, d >= WINDOW)
    win_ref[n_wt] = table(qpos - (past + i), i >= tq)


def bias_decode(rel_bias, past, tq):
    assert tq & (tq - 1) == 0
    tabc = jnp.repeat(rel_bias, tq, axis=1)
    n_pages = past // PAGE
    nc = NSA_HEADS * tq
    return pl.pallas_call(
        functools.partial(_bias_decode_body, past=past, tq=tq),
        out_shape=[jax.ShapeDtypeStruct((past // CMP_STRIDE, nc), F32),
                   jax.ShapeDtypeStruct((n_pages + 1, PAGE, nc), F32),
                   jax.ShapeDtypeStruct((WINDOW // PAGE + 1, PAGE, nc), F32)],
        name="bias_decode",
    )(tabc)


def _cmp_weights(cmp_pos_w, w_cmp):
    w = jnp.concatenate([cmp_pos_w[0].reshape(CMP_BLOCK, -1), cmp_pos_w[1].reshape(CMP_BLOCK, -1)], axis=1)
    blocks = w_cmp.reshape(2 * NSA_KVH, NSA_HD, NSA_HD)
    n = 2 * NSA_KVH
    wbd = (jnp.eye(n, dtype=F32)[:, None, :, None] * blocks[:, :, None, :]).reshape(n * NSA_HD, n * NSA_HD)
    return w[:CMP_STRIDE], w[CMP_STRIDE:], wbd.astype(BF16)


def _pool16(x, w):
    n = x.shape[0] // CMP_STRIDE
    return jnp.sum(x.reshape(n, CMP_STRIDE, x.shape[1]) * w[None], axis=1)


def _compress_prompt_body(x_ref, wlo_ref, whi_ref, wbd_ref, o_ref, lo_ref, hi_ref):
    T = x_ref.shape[0]
    n_sub = T // CMP_STRIDE
    step = 512
    for c in range(T // step):
        xs = x_ref[c * step:(c + 1) * step, :]
        r = slice(c * step // CMP_STRIDE, (c + 1) * step // CMP_STRIDE)
        lo_ref[r, :] = _pool16(xs, wlo_ref[...])
        hi_ref[r, :] = _pool16(xs, whi_ref[...])
    hi_ref[n_sub:n_sub + 8, :] = jnp.zeros((8, hi_ref.shape[1]), F32)
    blocks = lo_ref[...] + hi_ref[1:n_sub + 1, :]
    o_ref[...] = _dot(blocks, wbd_ref[...]).astype(BF16)


def compress_prompt(rows, n_seq, T, wlo, whi, wbd):
    n_sub = T // CMP_STRIDE
    W = 2 * NSA_KVH * NSA_HD
    return pl.pallas_call(
        _compress_prompt_body,
        out_shape=jax.ShapeDtypeStruct((n_seq * n_sub, W), BF16),
        grid=(n_seq,),
        in_specs=[pl.BlockSpec((T, W), lambda b: (b, 0)),
                  pl.BlockSpec((CMP_STRIDE, W), lambda b: (0, 0)),
                  pl.BlockSpec((CMP_STRIDE, W), lambda b: (0, 0)),
                  pl.BlockSpec((W, W), lambda b: (0, 0))],
        out_specs=pl.BlockSpec((n_sub, W), lambda b: (b, 0)),
        scratch_shapes=[pltpu.VMEM((n_sub, W), F32), pltpu.VMEM((n_sub + 8, W), F32)],
        compiler_params=_cparams(("parallel",)),
        name="compress_prompt",
    )(rows, wlo, whi, wbd)


def _masked_softmax(s, mask, axis):
    l = jnp.where(mask, s, NEG)
    m = jnp.max(l, axis=axis, keepdims=True)
    e = jnp.where(mask, jnp.exp(l - m), 0.0)
    return e / jnp.maximum(jnp.sum(e, axis=axis, keepdims=True), 1e-30)


def _split3(x):
    hi = x.astype(BF16)
    r = x - hi.astype(F32)
    mid = r.astype(BF16)
    lo = (r - mid.astype(F32)).astype(BF16)
    return hi, mid, lo


def _topk_rows(score, blk, n_rows, n_pick):
    beaten = jnp.zeros(score.shape, F32)
    for other in range(n_rows):
        row = score[other:other + 1, :]
        beaten = beaten + jnp.where(blk > other, jnp.where(row >= score, 1.0, 0.0), jnp.where(row > score, 1.0, 0.0))
    return jnp.where((beaten < n_pick) & (blk < n_rows), 1.0, 0.0)


def _attn_prompt_body(q_ref, g_ref, ka_ref, vs_ref, kw_ref, vw_ref, kc_ref, vc_ref, bc_ref, band_ref, bandw_ref,
                      o_ref, qp_scr, qw_scr, qa_scr, s_scr, p_scr, m_scr, acc_scr, sc_scr, sw_scr, pw_scr):
    qb = pl.program_id(2)
    q0 = qb * TQ
    G = NSA_GROUP
    R = G * TQ
    HD = NSA_HD
    groups = [slice(g * TQ, (g + 1) * TQ) for g in range(G)]
    slabs = [(slice(g * TQ + h, g * TQ + h + BAND_ROWS), g, h) for g in range(G) for h in range(0, TQ, BAND_ROWS)]
    qblk = q_ref[...] * (HD ** -0.5)
    Q = jnp.concatenate([qblk[:, g * HD:(g + 1) * HD] for g in range(G)], axis=0)
    zeros = jnp.zeros((R, HD), F32)
    qp_scr[...] = jnp.concatenate([Q, zeros], axis=1).astype(BF16)
    qw_scr[...] = jnp.concatenate([Q * LOG2E, zeros], axis=1).astype(BF16)

    left = lax.broadcasted_iota(jnp.int32, (TQ, 2 * HD), 1) < HD

    def normalized(acc):
        tiles = []
        for g in range(0, G, 2):
            a0, a1 = acc[groups[g]], acc[groups[g + 1]]
            r0, r1 = pltpu.roll(a0, HD, 1), pltpu.roll(a1, HD, 1)
            num = jnp.where(left, a0, r1)
            den = jnp.where(left, r0, a1)
            tiles.append(num / jnp.maximum(den, 1e-30))
        return jnp.concatenate(tiles, axis=1)

    gates = _split3(_sigmoid(g_ref[0]))
    W = G * HD
    gi = lax.broadcasted_iota(jnp.int32, (3 * G, W), 0)
    gc = lax.broadcasted_iota(jnp.int32, (3 * G, W), 1) // HD

    def gate(branch):
        spread = (gi == 3 * gc + branch).astype(BF16)
        return sum(jnp.dot(part, spread, preferred_element_type=F32) for part in gates)

    WK = WINDOW + TQ
    kw0 = pl.multiple_of(jnp.maximum(q0 - WINDOW, 0), LANES)
    sw_scr[...] = _dot_nt(qw_scr[...], kw_ref[pl.ds(kw0, WK), :])
    for r, g, h in slabs:
        cw0 = pl.multiple_of(BAND_TOP - (q0 + h - kw0), LANES)
        s = sw_scr[r, :] + bandw_ref[g, :, pl.ds(cw0, WK)]
        m = jnp.maximum(jnp.max(s, axis=1, keepdims=True), 0.5 * NEG)
        pw_scr[r, :] = jnp.exp2(s - m).astype(BF16)
    o_ref[...] = normalized(jnp.dot(pw_scr[...], vw_ref[pl.ds(kw0, WK), :], preferred_element_type=F32)) * gate(2)

    n_sub = bc_ref.shape[2]
    n_blk = n_sub * CMP_STRIDE // SEL_BLOCK
    n_pick = min(N_SEL, n_blk)
    per = SEL_BLOCK // CMP_STRIDE
    bb = lax.broadcasted_iota(jnp.int32, (n_blk, n_sub), 0)
    mm = lax.broadcasted_iota(jnp.int32, (n_blk, n_sub), 1)
    pool = ((mm // per == bb).astype(F32) + ((mm + 1) // per == bb).astype(F32)).astype(BF16)
    s_scr[:, :n_sub] = _dot_nt(qp_scr[...], kc_ref[...])
    for h in range(0, TQ, BAND_ROWS):
        imp = None
        for g in range(G):
            r = slice(g * TQ + h, g * TQ + h + BAND_ROWS)
            s = s_scr[r, :n_sub] + bc_ref[g, h:h + BAND_ROWS, :]
            e = jnp.exp(s - jnp.maximum(jnp.max(s, axis=1, keepdims=True), 0.5 * NEG))
            p_scr[r, :n_sub] = e.astype(BF16)
            p = e / jnp.maximum(jnp.sum(e, axis=1, keepdims=True), 1e-30)
            imp = p if imp is None else imp + p
        sc_scr[:, h:h + BAND_ROWS] = sum(_dot_nt(pool, part) for part in _split3(imp))
    o_ref[...] += normalized(jnp.dot(p_scr[:, :n_sub], vc_ref[...], preferred_element_type=F32)) * gate(0)

    p_slc = sc_scr[...]
    blk = lax.broadcasted_iota(jnp.int32, (n_blk, TQ), 0)
    t = q0 + lax.broadcasted_iota(jnp.int32, (n_blk, TQ), 1)
    cur = t // SEL_BLOCK
    forced = (blk == 0) | (blk == cur) | (blk == cur - 1)
    score = jnp.where(forced, 1e4, jnp.where(blk * SEL_BLOCK <= t, p_slc, -1.0))
    sc_scr[...] = score
    beaten = jnp.zeros((n_blk, TQ), F32)
    for other in range(n_blk):
        row = sc_scr[other:other + 1, :]
        ge = jnp.where(row >= score, 1.0, 0.0)
        gt = jnp.where(row > score, 1.0, 0.0)
        beaten = beaten + jnp.where(blk > other, ge, gt)
    selneg = jnp.where(beaten < n_pick, 0.0, NEG)
    selneg_q = selneg.T
    if n_blk < HD:
        selneg_q = jnp.concatenate([selneg_q, jnp.zeros((TQ, HD - n_blk), F32)], axis=1)
    qa_scr[...] = jnp.concatenate([jnp.concatenate([Q[r] * LOG2E, selneg_q], axis=1) for r in groups],
                                  axis=0).astype(BF16)

    def soften(kt, pv):
        rowmax = []
        for r, g, h in slabs:
            c0 = pl.multiple_of(jnp.maximum(BAND_TOP - (q0 + h - kt * TK), 0), LANES)
            s = s_scr[r, :] + band_ref[g, :, pl.ds(c0, TK)]
            s_scr[r, :] = s
            rowmax.append(jnp.max(s, axis=1, keepdims=True))
        for i, (r, g, h) in enumerate(slabs):
            m_old = m_scr[r, :]
            m_new = jnp.maximum(m_old, rowmax[i])
            m_scr[r, :] = m_new
            a = acc_scr[r, :] if pv is None else acc_scr[r, :] + pv[r]
            acc_scr[r, :] = jnp.exp2(m_old - m_new) * a
            for c in range(TK // LANES):
                cs = slice(c * LANES, (c + 1) * LANES)
                p_scr[r, cs] = jnp.exp2(s_scr[r, cs] - m_new).astype(BF16)

    def sel_qk(kt):
        s_scr[...] = _dot_nt(qa_scr[...], ka_ref[pl.ds(pl.multiple_of(kt * TK, TK), TK), :])

    def sel_pv(kt):
        v = vs_ref[pl.ds(pl.multiple_of(kt * TK, TK), TK), :]
        return jnp.dot(p_scr[...], v, preferred_element_type=F32)

    m_scr[...] = jnp.full(m_scr.shape, 0.5 * NEG, F32)
    acc_scr[...] = jnp.zeros(acc_scr.shape, F32)
    sel_qk(0)
    soften(0, None)

    def step(kt, carry):
        pv = sel_pv(kt - 1)
        sel_qk(kt)
        soften(kt, pv)
        return carry

    n_kt = (q0 + TQ + TK - 1) // TK
    lax.fori_loop(1, n_kt, step, 0)
    o_ref[...] += normalized(acc_scr[...] + sel_pv(n_kt - 1)) * gate(1)


def attn_prompt_operands(rows_bf, cmp_p, n_seq, T):
    M = n_seq * T
    W = NSA_KVH * NSA_HD

    def per_head(x, aux):
        x = x.reshape(x.shape[0], NSA_KVH, NSA_HD)
        out = jnp.concatenate([x, jnp.broadcast_to(aux[:, None, :], x.shape)], axis=2)
        return out.reshape(x.shape[0], NSA_KVH * 2 * NSA_HD)

    blk_of_key = (jnp.arange(M, dtype=jnp.int32) % T) // SEL_BLOCK
    onehot = (blk_of_key[:, None] == jnp.arange(NSA_HD, dtype=jnp.int32)[None, :]).astype(BF16)
    zeros, ones = jnp.zeros((M, NSA_HD), BF16), jnp.ones((M, NSA_HD), BF16)
    kind = lambda i: rows_bf[:M, i * W:(i + 1) * W]
    nc = cmp_p.shape[0]
    return (per_head(kind(2), onehot), per_head(kind(3), ones), per_head(kind(4), zeros), per_head(kind(5), ones),
            per_head(cmp_p[:, :W], zeros[:nc]), per_head(cmp_p[:, W:], ones[:nc]))


def attn_prompt(proj, n_seq, T, operands, bias_c, bands):
    M = n_seq * T
    NQ = T // TQ
    n_sub = T // CMP_STRIDE
    n_blk = T // SEL_BLOCK
    assert n_blk <= NSA_HD and T % TK == 0 and T >= WINDOW + TQ and n_sub <= TK
    gl = proj[:M, NSA_HEADS * NSA_HD:NSA_HEADS * NSA_HD + 3 * NSA_HEADS]
    gl = gl.reshape(M, NSA_KVH, 3 * NSA_GROUP).transpose(1, 0, 2)
    pair = 2 * NSA_HD
    ks_a, vs_a, kw_a, vw_a, kc_a, vc_a = operands
    kv_spec = pl.BlockSpec((T, pair), lambda k, b, i: (b, k))
    cmp_spec = pl.BlockSpec((n_sub, pair), lambda k, b, i: (b, k))
    band_spec = pl.BlockSpec((NSA_GROUP, BAND_ROWS, BAND_W), lambda k, b, i: (k, 0, 0))
    WO = NSA_GROUP * NSA_HD
    R = NSA_GROUP * TQ
    WK = WINDOW + TQ
    band, bandw = bands
    return pl.pallas_call(
        _attn_prompt_body,
        out_shape=jax.ShapeDtypeStruct((M, NSA_HEADS * NSA_HD), F32),
        grid=(NSA_KVH, n_seq, NQ),
        in_specs=[
            pl.BlockSpec((TQ, WO), lambda k, b, i: (b * NQ + i, k)),
            pl.BlockSpec((1, TQ, 3 * NSA_GROUP), lambda k, b, i: (k, b * NQ + i, 0)),
            kv_spec, kv_spec, kv_spec, kv_spec,
            cmp_spec, cmp_spec,
            pl.BlockSpec((NSA_GROUP, TQ, n_sub), lambda k, b, i: (k, i, 0)),
            band_spec, band_spec,
        ],
        out_specs=pl.BlockSpec((TQ, WO), lambda k, b, i: (b * NQ + i, k)),
        scratch_shapes=[pltpu.VMEM((R, pair), BF16), pltpu.VMEM((R, pair), BF16), pltpu.VMEM((R, pair), BF16),
                        pltpu.VMEM((R, TK), F32), pltpu.VMEM((R, TK), BF16),
                        pltpu.VMEM((R, LANES), F32), pltpu.VMEM((R, pair), F32),
                        pltpu.VMEM((n_blk, TQ), F32),
                        pltpu.VMEM((R, WK), F32), pltpu.VMEM((R, WK), BF16)],
        compiler_params=_cparams(("parallel", "parallel", "arbitrary")),
        name="attn_prompt",
    )(proj, gl, ks_a, vs_a, kw_a, vw_a, kc_a, vc_a, bias_c, band, bandw)


def _compress_decode_body(pt_ref, *refs, n_pages):
    pages = refs[:n_pages]
    new_ref, wlo_ref, whi_ref, wbd_ref, o_ref = refs[n_pages:]
    per = PAGE // CMP_STRIDE
    n_sub = n_pages * per
    W = wlo_ref.shape[0]
    pos = lax.broadcasted_iota(jnp.int32, (2 * PAGE, n_sub), 0)
    blk = lax.broadcasted_iota(jnp.int32, (2 * PAGE, n_sub), 1)

    def pooled(x, p):
        sub = p * per + (pos % PAGE) // CMP_STRIDE
        place = (blk == jnp.where(pos < PAGE, sub, sub - 1)).astype(BF16)
        y = jnp.concatenate([x * wlo_ref[...], x * whi_ref[...]], axis=1).astype(BF16)
        return jnp.dot(y, place, preferred_element_type=F32)

    blocks = pooled(new_ref[0], n_pages)
    for p in range(n_pages):
        blocks = blocks + pooled(pages[p][0].reshape(W, PAGE), p)
    o_ref[0] = jnp.dot(wbd_ref[...], blocks.astype(BF16), preferred_element_type=F32).astype(BF16)


def compress_decode(cache_t, page_table, new_rows, wlo, whi, wbd):
    nb, n_pages = page_table.shape
    W = 2 * NSA_KVH * NSA_HD
    tq = new_rows.shape[1]
    assert tq <= CMP_STRIDE
    n_sub = n_pages * PAGE // CMP_STRIDE
    wlo_t = jnp.tile(wlo.T, (1, PAGE // CMP_STRIDE))
    whi_t = jnp.tile(whi.T, (1, PAGE // CMP_STRIDE))
    new_t = jnp.pad(new_rows[:, :, :W].transpose(0, 2, 1), ((0, 0), (0, 0), (0, PAGE - tq)))
    page_spec = lambda p: pl.BlockSpec((1, 2, W // 2, PAGE), lambda b, pt: (pt[b, p], 0, 0, 0))
    const = lambda shape: pl.BlockSpec(shape, lambda b, pt: (0,) * len(shape))
    return pl.pallas_call(
        functools.partial(_compress_decode_body, n_pages=n_pages),
        out_shape=jax.ShapeDtypeStruct((nb, W, n_sub), BF16),
        grid_spec=pltpu.PrefetchScalarGridSpec(
            num_scalar_prefetch=1,
            grid=(nb,),
            in_specs=[page_spec(p) for p in range(n_pages)] + [
                pl.BlockSpec((1, W, PAGE), lambda b, pt: (b, 0, 0)),
                const((W, PAGE)), const((W, PAGE)), const((W, W))],
            out_specs=pl.BlockSpec((1, W, n_sub), lambda b, pt: (b, 0, 0)),
        ),
        compiler_params=_cparams(("parallel",)),
        name="compress_decode",
    )(page_table, *([cache_t] * n_pages), new_t, wlo_t, whi_t, wbd.T)


def _attn_decode_body(pt_ref, *refs, n_pages, tq):
    pages = refs[:n_pages]
    q_ref, qt_ref, g_ref, new_ref, win_ref, cmp_ref, bcmp_ref, bsel_ref, bwin_ref, o_ref = refs[n_pages:]
    W = NSA_KVH * NSA_HD
    NC = NSA_HEADS * tq
    Qbd = q_ref[0]
    QbdT = qt_ref[0]
    past = n_pages * PAGE
    tn = (((0,), (0,)), ((), ()))

    def logits_t(kT):
        return lax.dot_general(kT.astype(BF16), QbdT, tn, preferred_element_type=F32)

    cm = cmp_ref[0]
    bc = bcmp_ref[...]
    p_c = _masked_softmax(logits_t(cm[:W]) + bc, bc > 0.5 * NEG, 0)
    o_cmp = jnp.dot(cm[W:], p_c.astype(BF16), preferred_element_type=F32)

    n_sub = cm.shape[1]
    per = SEL_BLOCK // CMP_STRIDE
    n_blk = past // SEL_BLOCK + 1
    nb_pad = (n_blk + 7) // 8 * 8
    ci = lax.broadcasted_iota(jnp.int32, (NC, NC), 0)
    cj = lax.broadcasted_iota(jnp.int32, (NC, NC), 1)
    gq = NSA_GROUP * tq
    same = ((ci // gq == cj // gq) & ((ci & (tq - 1)) == (cj & (tq - 1)))).astype(BF16)
    imp = sum(jnp.dot(part, same, preferred_element_type=F32) for part in _split3(p_c))
    bb = lax.broadcasted_iota(jnp.int32, (nb_pad, n_sub), 0)
    mm = lax.broadcasted_iota(jnp.int32, (nb_pad, n_sub), 1)
    pool = ((mm // per == bb).astype(F32) + ((mm + 1) // per == bb).astype(F32)).astype(BF16)
    p_slc = sum(jnp.dot(pool, part, preferred_element_type=F32) for part in _split3(imp))
    blk = lax.broadcasted_iota(jnp.int32, (nb_pad, NC), 0)
    qpos = past + (lax.broadcasted_iota(jnp.int32, (nb_pad, NC), 1) & (tq - 1))
    cur = qpos // SEL_BLOCK
    forced = (blk == 0) | (blk == cur) | (blk == cur - 1)
    score = jnp.where(forced, 1e4, jnp.where(blk * SEL_BLOCK <= qpos, p_slc, -1.0))
    score = jnp.where(blk < n_blk, score, -3.0)
    sel = _topk_rows(score, blk, n_blk, min(N_SEL, n_blk))

    def attend(tiles):
        logits = []
        for k, _, bias, mask, stored in tiles:
            b = bias()
            logits.append(jnp.where(mask(b), (logits_t(k()) if stored else _dot_nt(k(), Qbd)) + b, NEG))
        m = functools.reduce(jnp.maximum, [jnp.max(s, axis=0, keepdims=True) for s in logits])
        den = jnp.zeros((1, NC), F32)
        acc = jnp.zeros((W, NC), F32)
        for s, (_, v, _, _, stored) in zip(logits, tiles):
            p = jnp.where(s > 0.5 * NEG, jnp.exp(s - m), 0.0)
            den = den + jnp.sum(p, axis=0, keepdims=True)
            if stored:
                acc = acc + jnp.dot(v().astype(BF16), p.astype(BF16), preferred_element_type=F32)
            else:
                acc = acc + lax.dot_general(v().astype(BF16), p.astype(BF16), tn, preferred_element_type=F32)
        return acc / jnp.maximum(den, 1e-30)

    new = new_ref[0]
    pad = jnp.zeros((8 - tq, W), F32)
    new_tile = lambda kind: (lambda: jnp.concatenate([new[:, kind * W:(kind + 1) * W], pad], axis=0))

    half = lax.broadcasted_iota(jnp.int32, (PAGE, NC), 0) < SEL_BLOCK
    def page_mask(p):
        b0 = p * (PAGE // SEL_BLOCK)
        return lambda bias: (jnp.where(half, sel[b0:b0 + 1, :], sel[b0 + 1:b0 + 2, :]) > 0.5) & (bias > 0.5 * NEG)

    tiles = [(lambda p=p: pages[p][0, 0], lambda p=p: pages[p][0, 1], lambda p=p: bsel_ref[p], page_mask(p), True)
             for p in range(n_pages)]
    tiles.append((new_tile(2), new_tile(3), lambda: bsel_ref[n_pages][0:8],
                  lambda bias: (sel[n_blk - 1:n_blk, :] > 0.5) & (bias > 0.5 * NEG), False))
    o_sel = attend(tiles)

    causal = lambda bias: bias > 0.5 * NEG
    tiles = []
    for t in range(WINDOW // PAGE):
        ts = slice(t * PAGE, (t + 1) * PAGE)
        tiles.append((lambda ts=ts: win_ref[0, 0, :, ts], lambda ts=ts: win_ref[0, 1, :, ts],
                      lambda t=t: bwin_ref[t], causal, True))
    tiles.append((new_tile(4), new_tile(5), lambda: bwin_ref[WINDOW // PAGE][0:8], causal, False))
    o_win = attend(tiles)

    gt = _sigmoid(g_ref[0])
    o = o_cmp * gt[0:1] + o_sel * gt[1:2] + o_win * gt[2:3]
    kvh_of_col = lax.broadcasted_iota(jnp.int32, (NSA_HD, NC), 1) // gq
    out = jnp.zeros((NSA_HD, NC), F32)
    for k in range(NSA_KVH):
        out = out + jnp.where(kvh_of_col == k, o[k * NSA_HD:(k + 1) * NSA_HD, :], 0.0)
    o_ref[0] = out


def attn_decode(proj, cache_t, page_table, new_rows, win_t, cmp_d, bias_tabs):
    nb, n_pages = page_table.shape
    tq = new_rows.shape[1]
    W = NSA_KVH * NSA_HD
    NC = NSA_HEADS * tq
    q = proj[:, :NSA_HEADS * NSA_HD] * (NSA_HD ** -0.5)
    q = q.reshape(nb, tq, NSA_KVH, NSA_GROUP, NSA_HD).transpose(0, 2, 3, 1, 4)
    qbd = q[:, :, :, :, None, :] * jnp.eye(NSA_KVH, dtype=F32)[None, :, None, None, :, None]
    qbd = qbd.reshape(nb, NC, W).astype(BF16)
    qbd_t = qbd.transpose(0, 2, 1)
    gl = proj[:, NSA_HEADS * NSA_HD:NSA_HEADS * NSA_HD + 3 * NSA_HEADS]
    gl = gl.reshape(nb, tq, NSA_HEADS, 3).transpose(0, 3, 2, 1).reshape(nb, 3, NC)
    bcmp, bsel, bwin = bias_tabs
    page_spec = lambda p: pl.BlockSpec((1, 2, W, PAGE), lambda b, pt: (pt[b, p], 1, 0, 0))
    const = lambda shape: pl.BlockSpec(shape, lambda b, pt: (0,) * len(shape))
    per_b = lambda shape: pl.BlockSpec((1,) + shape, lambda b, pt: (b,) + (0,) * len(shape))
    o = pl.pallas_call(
        functools.partial(_attn_decode_body, n_pages=n_pages, tq=tq),
        out_shape=jax.ShapeDtypeStruct((nb, NSA_HD, NC), F32),
        grid_spec=pltpu.PrefetchScalarGridSpec(
            num_scalar_prefetch=1,
            grid=(nb,),
            in_specs=[page_spec(p) for p in range(n_pages)] + [
                per_b((NC, W)), per_b((W, NC)), per_b((3, NC)), per_b((tq, 6 * W)), per_b((2, W, WINDOW)),
                per_b((2 * W, cmp_d.shape[2])),
                const(bcmp.shape), const(bsel.shape), const(bwin.shape)],
            out_specs=per_b((NSA_HD, NC)),
        ),
        compiler_params=_cparams(("parallel",)),
        name="attn_decode",
    )(page_table, *([cache_t] * n_pages), qbd, qbd_t, gl, new_rows, win_t, cmp_d, bcmp, bsel, bwin)
    return o.reshape(nb, NSA_HD, NSA_HEADS, tq).transpose(0, 3, 2, 1).reshape(nb * tq, NSA_HEADS * NSA_HD)


def _pad_cols(w, n):
    return jnp.pad(w, ((0, 0), (0, n - w.shape[1])))


def kernel(x_prompt, x_sample, state_dn_S, state_dn_conv, cache_kv, state_win_kv, page_table, norm_mix, norm_ffn, norm_kv, norm_final, ffn_w_in, ffn_w_out, dn_w_in, dn_conv_w, dn_A_log, dn_dt_bias, dn_out_norm, dn_w_out, nsa_w_kv, nsa_cmp_pos_w, nsa_w_cmp, nsa_w_in, nsa_w_out, rel_bias):
    B, T, D = x_prompt.shape
    NB, TS, _ = x_sample.shape
    Mp, Ms = B * T, NB * TS
    past = page_table.shape[1] * PAGE
    x = jnp.concatenate([x_prompt.reshape(Mp, D), x_sample.reshape(Ms, D)], axis=0)

    p_S, p_conv, s_conv = [], [], []
    s_S = None
    for l in range(N_A_LAYERS):
        w_in = _pad_cols(dn_w_in[l], 4 * D + LANES).astype(BF16)
        proj = linear(x, w_in, norm_w=norm_mix[l], tn=(4 * D + LANES) // 3)
        o_p, S_p = dn_prompt(proj, B, T, dn_conv_w[l], dn_A_log[l], dn_dt_bias[l], dn_out_norm[l])
        o_s, s_S = dn_decode(proj[Mp:], NB, state_dn_conv[l], state_dn_S, l, s_S, dn_conv_w[l], dn_A_log[l],
                             dn_dt_bias[l], dn_out_norm[l])
        tail = DN_CONV - 1
        qkv_s = proj[Mp:, :DN_QKV].reshape(NB, TS, DN_QKV)
        p_S.append(S_p)
        p_conv.append(jnp.stack([proj[(b + 1) * T - tail:(b + 1) * T, :DN_QKV] for b in range(B)]))
        s_conv.append(jnp.concatenate([state_dn_conv[l], qkv_s], axis=1)[:, TS:])
        x = linear(o_p, dn_w_out[l].astype(BF16), residual=x, x_tail=o_s, tn=D)
        x = ffn(x, norm_ffn[l], ffn_w_in[l].astype(BF16), ffn_w_out[l].astype(BF16))

    W = NSA_KVH * NSA_HD
    rows, rows_bf, kv_t, win_t_p = shared_rows(x, norm_kv, nsa_w_kv.astype(BF16), B, T, 4 * W)
    rows_s = rows[Mp:].reshape(NB, TS, 6, NSA_KVH, NSA_HD)
    p_kv_rows = kv_t.reshape(B, 4, NSA_KVH, NSA_HD, T).transpose(0, 4, 1, 2, 3)
    p_win_kv = win_t_p.reshape(B, 2, NSA_KVH, NSA_HD, T).transpose(0, 4, 1, 2, 3)[:, T - min(WINDOW, T):]
    s_kv_rows = rows_s[:, :, :4]
    s_win_kv = jnp.concatenate([state_win_kv, rows_s[:, :, 4:]], axis=1)[:, TS:]

    wlo, whi, wbd = _cmp_weights(nsa_cmp_pos_w, nsa_w_cmp)
    cmp_p = compress_prompt(rows, B, T, wlo, whi, wbd)
    attn_ops = attn_prompt_operands(rows_bf, cmp_p, B, T)
    new_rows = rows[Mp:].reshape(NB, TS, 6 * W)
    cache_t = cache_kv.transpose(0, 2, 3, 4, 1).reshape(cache_kv.shape[0], 4, W, PAGE)
    win_t = state_win_kv.transpose(0, 2, 3, 4, 1).reshape(NB, 2, W, state_win_kv.shape[1])
    cmp_d = compress_decode(cache_t, page_table, new_rows, wlo, whi, wbd)
    bias_c = bias_cmp_prompt(rel_bias, T)
    bands = bias_band(rel_bias)
    bias_d = bias_decode(rel_bias, past, TS)

    for j in range(N_B_LAYERS):
        l = N_A_LAYERS + j
        w_in = _pad_cols(nsa_w_in[j], D + LANES).astype(BF16)
        proj = linear(x, w_in, norm_w=norm_mix[l], tn=D + LANES)
        o_p = attn_prompt(proj, B, T, attn_ops, bias_c, bands)
        o_s = attn_decode(proj[Mp:], cache_t, page_table, new_rows, win_t, cmp_d, bias_d)
        x = linear(o_p, nsa_w_out[j].astype(BF16), residual=x, x_tail=o_s, tn=D)
        x = ffn(x, norm_ffn[l], ffn_w_in[l].astype(BF16), ffn_w_out[l].astype(BF16))

    y_p, y_s = final_norm(x, norm_final, Mp)
    return (y_p.reshape(B, T, D), y_s.reshape(NB, TS, D),
            jnp.stack(p_S), jnp.stack(p_conv), p_kv_rows, p_win_kv,
            s_S, jnp.stack(s_conv), s_kv_rows, s_win_kv)
```

```python
import functools
import math

import jax
import jax.numpy as jnp
from jax import lax
from jax.experimental import pallas as pl
from jax.experimental.pallas import tpu as pltpu

F32 = jnp.float32
BF16 = jnp.bfloat16

D_MODEL = 1024
N_A_LAYERS = 2
N_B_LAYERS = 2
NORM_EPS = 1e-6
DN_HEADS = 8
DN_HEAD = 128
DN_QKV = 3 * D_MODEL
DN_CONV = 4
DN_CHUNK = 64
NSA_HEADS = 16
NSA_HD = 64
NSA_KVH = 4
NSA_GROUP = 4
CMP_STRIDE = 16
CMP_BLOCK = 32
SEL_BLOCK = 64
N_SEL = 16
WINDOW = 512
PAGE = 128
REL_BUCKETS = 32
REL_MAX_EXACT = 16
REL_MAX_DIST = 1024
NEG = -1e30

V7X_VMEM_LIMIT = 56 * 1024 * 1024
LANES = 128


def _cparams(sem):
    return pltpu.CompilerParams(dimension_semantics=sem, vmem_limit_bytes=V7X_VMEM_LIMIT)


def _rms(x, w):
    ms = jnp.mean(x * x, axis=-1, keepdims=True)
    return x * lax.rsqrt(ms + NORM_EPS) * w


def _silu(x):
    return x * (1.0 / (1.0 + jnp.exp(-x)))


def _sigmoid(x):
    return 1.0 / (1.0 + jnp.exp(-x))


def _softplus(x):
    return jnp.maximum(x, 0.0) + jnp.log1p(jnp.exp(-jnp.abs(x)))


def _dot(a, b):
    return jnp.dot(a.astype(BF16), b.astype(BF16), preferred_element_type=F32)


def _dot_nt(a, b):
    return lax.dot_general(a.astype(BF16), b.astype(BF16), (((1,), (1,)), ((), ())),
                           preferred_element_type=F32)


def _bdot_dims(a, b, ca, cb):
    return lax.dot_general(a.astype(BF16), b.astype(BF16), (((ca,), (cb,)), ((0,), (0,))),
                           preferred_element_type=F32)


def _bdot(a, b):
    return _bdot_dims(a, b, 2, 1)


def _bdot_nt(a, b):
    return _bdot_dims(a, b, 2, 2)


def _bdot_tn(a, b):
    return _bdot_dims(a, b, 1, 1)


def _linear_body(*refs, has_norm, has_res, n_head_tiles):
    it = iter(refs)
    x_ref = next(it)
    xt_ref = next(it) if n_head_tiles else None
    nw_ref = next(it) if has_norm else None
    w_ref = next(it)
    res_ref = next(it) if has_res else None
    o_ref = next(it)
    xn_ref = next(it)

    def stage(src_ref):
        x = src_ref[...]
        if has_norm:
            x = _rms(x, nw_ref[...])
        xn_ref[...] = x.astype(BF16)

    first = pl.program_id(1) == 0
    if n_head_tiles:
        in_head = pl.program_id(0) < n_head_tiles
        pl.when(first & in_head)(lambda: stage(x_ref))
        pl.when(first & jnp.logical_not(in_head))(lambda: stage(xt_ref))
    else:
        pl.when(first)(lambda: stage(x_ref))

    acc = jnp.dot(xn_ref[...], w_ref[...], preferred_element_type=F32)
    if has_res:
        acc = acc + res_ref[...]
    o_ref[...] = acc


def linear(x, w, norm_w=None, residual=None, x_tail=None, tm=512, tn=None):
    K = x.shape[1]
    N = w.shape[1]
    tn = N if tn is None else tn
    n_head_tiles = 0
    M = x.shape[0]
    assert M % tm == 0 and N % tn == 0
    args, specs = [x], [pl.BlockSpec((tm, K), lambda i, j: (i, 0))]
    if x_tail is not None:
        n_head_tiles = M // tm
        assert x_tail.shape[0] % tm == 0
        M += x_tail.shape[0]
        specs = [pl.BlockSpec((tm, K), lambda i, j: (jnp.minimum(i, n_head_tiles - 1), 0)),
                 pl.BlockSpec((tm, K), lambda i, j: (jnp.maximum(i - n_head_tiles, 0), 0))]
        args.append(x_tail)
    has_norm, has_res = norm_w is not None, residual is not None
    if has_norm:
        args.append(norm_w.reshape(1, K))
        specs.append(pl.BlockSpec((1, K), lambda i, j: (0, 0)))
    args.append(w)
    specs.append(pl.BlockSpec((K, tn), lambda i, j: (0, j)))
    if has_res:
        args.append(residual)
        specs.append(pl.BlockSpec((tm, tn), lambda i, j: (i, j)))
    return pl.pallas_call(
        functools.partial(_linear_body, has_norm=has_norm, has_res=has_res, n_head_tiles=n_head_tiles),
        out_shape=jax.ShapeDtypeStruct((M, N), F32),
        grid=(M // tm, N // tn),
        in_specs=specs,
        out_specs=pl.BlockSpec((tm, tn), lambda i, j: (i, j)),
        scratch_shapes=[pltpu.VMEM((tm, K), BF16)],
        compiler_params=_cparams(("parallel", "arbitrary")),
        name="linear",
    )(*args)


def _shared_rows_body(x_ref, nw_ref, w_ref, rows_ref, rows_bf_ref, kvt_ref, wint_ref, *, n_head_tiles):
    acc = jnp.dot(_rms(x_ref[...], nw_ref[...]).astype(BF16), w_ref[...], preferred_element_type=F32)
    rows_ref[...] = acc
    rows_bf_ref[...] = acc.astype(BF16)

    @pl.when(pl.program_id(0) < n_head_tiles)
    def _():
        n_kv = kvt_ref.shape[1]
        for c in range(0, acc.shape[1], LANES):
            dst = kvt_ref.at[0, c:c + LANES] if c < n_kv else wint_ref.at[0, c - n_kv:c - n_kv + LANES]
            dst[...] = acc[:, c:c + LANES].T


def shared_rows(x, norm_w, w, n_seq, T, n_kv, tm=512):
    M, K = x.shape
    N = w.shape[1]
    n_head_tiles = n_seq * T // tm
    per_seq = T // tm
    assert M % tm == 0 and T % tm == 0 and n_kv % LANES == 0 and N % LANES == 0
    seq_block = lambda i: (jnp.minimum(i, n_head_tiles - 1) // per_seq, 0, jnp.minimum(i, n_head_tiles - 1) % per_seq)
    return pl.pallas_call(
        functools.partial(_shared_rows_body, n_head_tiles=n_head_tiles),
        out_shape=[jax.ShapeDtypeStruct((M, N), F32), jax.ShapeDtypeStruct((M, N), BF16),
                   jax.ShapeDtypeStruct((n_seq, n_kv, T), F32), jax.ShapeDtypeStruct((n_seq, N - n_kv, T), F32)],
        grid=(M // tm,),
        in_specs=[pl.BlockSpec((tm, K), lambda i: (i, 0)), pl.BlockSpec((1, K), lambda i: (0, 0)),
                  pl.BlockSpec((K, N), lambda i: (0, 0))],
        out_specs=[pl.BlockSpec((tm, N), lambda i: (i, 0)), pl.BlockSpec((tm, N), lambda i: (i, 0)),
                   pl.BlockSpec((1, n_kv, tm), seq_block), pl.BlockSpec((1, N - n_kv, tm), seq_block)],
        compiler_params=_cparams(("arbitrary",)),
        name="shared_rows",
    )(x, norm_w.reshape(1, K), w)


def _ffn_body(x_ref, nw_ref, wg_ref, wu_ref, wo_ref, o_ref, xn_ref, acc_ref):
    f = pl.program_id(1)

    @pl.when(f == 0)
    def _():
        xn_ref[...] = _rms(x_ref[...], nw_ref[...]).astype(BF16)
        acc_ref[...] = jnp.zeros_like(acc_ref)

    xn = xn_ref[...]
    g = jnp.dot(xn, wg_ref[...], preferred_element_type=F32)
    u = jnp.dot(xn, wu_ref[...], preferred_element_type=F32)
    a = (_silu(g) * u).astype(BF16)
    acc_ref[...] += jnp.dot(a, wo_ref[...], preferred_element_type=F32)

    @pl.when(f == pl.num_programs(1) - 1)
    def _():
        o_ref[...] = x_ref[...] + acc_ref[...]


def _cast_body(x_ref, o_ref):
    o_ref[...] = x_ref[...].astype(o_ref.dtype)


def to_bf16(w, rows=256):
    L, R, C = w.shape
    assert R % rows == 0
    spec = pl.BlockSpec((1, rows, C), lambda l, r: (l, r, 0))
    return pl.pallas_call(
        _cast_body,
        out_shape=jax.ShapeDtypeStruct(w.shape, BF16),
        grid=(L, R // rows),
        in_specs=[spec],
        out_specs=spec,
        compiler_params=_cparams(("parallel", "parallel")),
        name="to_bf16",
    )(w)


def ffn(x, norm_w, w_in, w_out, layer, tm=512, tf=256):
    M, D = x.shape
    FF = w_out.shape[1]
    nf = FF // tf
    assert M % tm == 0 and FF % tf == 0
    return pl.pallas_call(
        _ffn_body,
        out_shape=jax.ShapeDtypeStruct((M, D), F32),
        grid=(M // tm, nf),
        in_specs=[
            pl.BlockSpec((tm, D), lambda i, f: (i, 0)),
            pl.BlockSpec((1, D), lambda i, f: (0, 0)),
            pl.BlockSpec((None, D, tf), lambda i, f: (layer, 0, f)),
            pl.BlockSpec((None, D, tf), lambda i, f: (layer, 0, f + nf)),
            pl.BlockSpec((None, tf, D), lambda i, f: (layer, f, 0)),
        ],
        out_specs=pl.BlockSpec((tm, D), lambda i, f: (i, 0)),
        scratch_shapes=[pltpu.VMEM((tm, D), BF16), pltpu.VMEM((tm, D), F32)],
        compiler_params=_cparams(("parallel", "arbitrary")),
        name="ffn",
    )(x, norm_w.reshape(1, D), w_in, w_in, w_out)


def _final_norm_body(x_ref, w_ref, head_ref, tail_ref, *, n_head_tiles):
    y = _rms(x_ref[...], w_ref[...])
    in_head = pl.program_id(0) < n_head_tiles

    @pl.when(in_head)
    def _():
        head_ref[...] = y

    @pl.when(jnp.logical_not(in_head))
    def _():
        tail_ref[...] = y


def final_norm(x, w, m_head, tm=512):
    M, D = x.shape
    n_head_tiles = m_head // tm
    assert m_head % tm == 0 and (M - m_head) % tm == 0 and 0 < m_head < M
    return pl.pallas_call(
        functools.partial(_final_norm_body, n_head_tiles=n_head_tiles),
        out_shape=[jax.ShapeDtypeStruct((m_head, D), F32), jax.ShapeDtypeStruct((M - m_head, D), F32)],
        grid=(M // tm,),
        in_specs=[pl.BlockSpec((tm, D), lambda i: (i, 0)), pl.BlockSpec((1, D), lambda i: (0, 0))],
        out_specs=[pl.BlockSpec((tm, D), lambda i: (jnp.minimum(i, n_head_tiles - 1), 0)),
                   pl.BlockSpec((tm, D), lambda i: (jnp.maximum(i - n_head_tiles, 0), 0))],
        compiler_params=_cparams(("arbitrary",)),
        name="final_norm",
    )(x, w.reshape(1, D))


def _l2n(x):
    return x * lax.rsqrt(jnp.sum(x * x, axis=-1, keepdims=True) + NORM_EPS)


def _gated_out(o, z, onorm):
    return _rms(o, onorm) * _silu(z)


def _dn_prompt_body(qkv_ref, z_ref, ab_ref, abT_ref, cw_ref, alr_ref, dtr_ref, alc_ref, dtc_ref, on_ref,
                    o_ref, s_out_ref, xbuf_ref, s_ref):
    n = pl.program_id(1)
    C = DN_CHUNK

    @pl.when(n == 0)
    def _():
        xbuf_ref[0:8, :] = jnp.zeros((8, DN_QKV), F32)
        s_ref[...] = jnp.zeros_like(s_ref)

    xbuf_ref[8:8 + C, :] = qkv_ref[...]
    y = xbuf_ref[5:5 + C, :] * cw_ref[0:1, :]
    for i in range(1, DN_CONV):
        y = y + xbuf_ref[5 + i:5 + i + C, :] * cw_ref[i:i + 1, :]
    xbuf_ref[0:8, :] = xbuf_ref[C:C + 8, :]
    y = _silu(y)

    ab = ab_ref[...]
    g8 = -jnp.exp(alr_ref[...]) * _softplus(ab[:, 0:DN_HEADS] + dtr_ref[...])
    beta8 = _sigmoid(ab[:, DN_HEADS:2 * DN_HEADS])
    abT = abT_ref[0, 0]
    g8T = -jnp.exp(alc_ref[...]) * _softplus(abT[0:DN_HEADS, :] + dtc_ref[...])

    ii = lax.broadcasted_iota(jnp.int32, (C, C), 0)
    jj = lax.broadcasted_iota(jnp.int32, (C, C), 1)
    incl = (ii >= jj)[None]
    strict = (ii > jj)[None]
    onorm = on_ref[...]
    H = DN_HEADS

    heads = lambda off: jnp.stack([y[:, off + h * DN_HEAD:off + (h + 1) * DN_HEAD] for h in range(H)], axis=0)
    q = _l2n(heads(0)) * (DN_HEAD ** -0.5)
    k = _l2n(heads(D_MODEL))
    v = heads(2 * D_MODEL)
    g_col = jnp.stack([g8[:, h:h + 1] for h in range(H)], axis=0)
    beta = jnp.stack([beta8[:, h:h + 1] for h in range(H)], axis=0)
    g_row = jnp.stack([g8T[h:h + 1, :] for h in range(H)], axis=0)
    G_col = jnp.sum(jnp.where(incl, g_row, 0.0), axis=2, keepdims=True)
    G_row = jnp.sum(jnp.where((ii <= jj)[None], g_col, 0.0), axis=1, keepdims=True)
    dec = jnp.where(incl, jnp.exp(jnp.where(incl, G_col - G_row, 0.0)), 0.0)
    A = jnp.where(strict, beta * dec * _bdot_nt(k, k), 0.0)
    X = -A
    Tm = X
    for _ in range(int(math.log2(C)) - 1):
        X = _bdot(X, X)
        Tm = Tm + X + _bdot(X, Tm)
    eG = jnp.exp(G_col)
    w = jnp.concatenate([beta * v, (beta * eG) * k], axis=2)
    w = w + _bdot(Tm, w)
    wv, wk = w[:, :, :DN_HEAD], w[:, :, DN_HEAD:]
    aqk = dec * _bdot_nt(q, k)
    qg = eG * q
    G_last = G_col[:, C - 1:C, :]
    kdec = jnp.exp(G_last - G_col) * k
    S = s_ref[...]
    ws = _bdot(jnp.concatenate([wk, qg], axis=1), S)
    U = wv - ws[:, :C]
    O = ws[:, C:] + _bdot(aqk, U)
    s_ref[...] = jnp.exp(G_last) * S + _bdot_tn(kdec, U)
    for h in range(H):
        sl = slice(h * DN_HEAD, (h + 1) * DN_HEAD)
        o_ref[:, sl] = _gated_out(O[h], z_ref[:, sl], onorm)

    @pl.when(n == pl.num_programs(1) - 1)
    def _():
        s_out_ref[0] = s_ref[...]


def dn_prompt(proj, n_seq, T, conv_w, a_log, dt_bias, out_norm):
    M = n_seq * T
    C = DN_CHUNK
    N = T // C
    abT = proj[:M, 4 * D_MODEL:4 * D_MODEL + 2 * DN_HEADS].reshape(n_seq, N, C, 2 * DN_HEADS).transpose(0, 1, 3, 2)
    row = lambda a: a.reshape(1, DN_HEADS)
    col = lambda a: a.reshape(DN_HEADS, 1)
    return pl.pallas_call(
        _dn_prompt_body,
        out_shape=[jax.ShapeDtypeStruct((M, D_MODEL), F32),
                   jax.ShapeDtypeStruct((n_seq, DN_HEADS, DN_HEAD, DN_HEAD), F32)],
        grid=(n_seq, N),
        in_specs=[
            pl.BlockSpec((C, DN_QKV), lambda b, n: (b * N + n, 0)),
            pl.BlockSpec((C, D_MODEL), lambda b, n: (b * N + n, 3)),
            pl.BlockSpec((C, LANES), lambda b, n: (b * N + n, 4 * D_MODEL // LANES)),
            pl.BlockSpec((1, 1, 2 * DN_HEADS, C), lambda b, n: (b, n, 0, 0)),
            pl.BlockSpec((DN_CONV, DN_QKV), lambda b, n: (0, 0)),
            pl.BlockSpec((1, DN_HEADS), lambda b, n: (0, 0)),
            pl.BlockSpec((1, DN_HEADS), lambda b, n: (0, 0)),
            pl.BlockSpec((DN_HEADS, 1), lambda b, n: (0, 0)),
            pl.BlockSpec((DN_HEADS, 1), lambda b, n: (0, 0)),
            pl.BlockSpec((1, DN_HEAD), lambda b, n: (0, 0)),
        ],
        out_specs=[pl.BlockSpec((C, D_MODEL), lambda b, n: (b * N + n, 0)),
                   pl.BlockSpec((1, DN_HEADS, DN_HEAD, DN_HEAD), lambda b, n: (b, 0, 0, 0))],
        scratch_shapes=[pltpu.VMEM((C + 8, DN_QKV), F32), pltpu.VMEM((DN_HEADS, DN_HEAD, DN_HEAD), F32)],
        compiler_params=_cparams(("parallel", "arbitrary")),
        name="dn_prompt",
    )(proj, proj, proj, abT, conv_w, row(a_log), row(dt_bias), col(a_log), col(dt_bias), out_norm.reshape(1, DN_HEAD))


def _dn_decode_body(proj_ref, cbuf_ref, s0_ref, cw_ref, alr_ref, dtr_ref, on_ref, *rest, out_layer):
    o_ref, s_out_ref, xbuf_ref, oacc_ref = rest[-4:]
    for l in range(s_out_ref.shape[0]):
        if l != out_layer:
            s_out_ref[l] = jnp.zeros(s_out_ref.shape[1:], F32)
    T = proj_ref.shape[1]
    x = proj_ref[0]
    xbuf_ref[8 - (DN_CONV - 1):8, :] = cbuf_ref[0]
    xbuf_ref[8:8 + T, :] = x[:, :DN_QKV]
    y = xbuf_ref[5:5 + T, :] * cw_ref[0:1, :]
    for i in range(1, DN_CONV):
        y = y + xbuf_ref[5 + i:5 + i + T, :] * cw_ref[i:i + 1, :]
    y = _silu(y)
    ab = x[:, 4 * D_MODEL:4 * D_MODEL + LANES]
    a8 = jnp.exp(-jnp.exp(alr_ref[...]) * _softplus(ab[:, 0:DN_HEADS] + dtr_ref[...]))
    beta8 = _sigmoid(ab[:, DN_HEADS:2 * DN_HEADS])
    ii = lax.broadcasted_iota(jnp.int32, (DN_HEAD, DN_HEAD), 0)
    jj = lax.broadcasted_iota(jnp.int32, (DN_HEAD, DN_HEAD), 1)
    eye = ii == jj

    def to_col(r):
        return jnp.sum(jnp.where(eye, r, 0.0), axis=1, keepdims=True)

    heads = [slice(h * DN_HEAD, (h + 1) * DN_HEAD) for h in range(DN_HEADS)]
    q = [_l2n(y[:, sl]) * (DN_HEAD ** -0.5) for sl in heads]
    k = [_l2n(y[:, D_MODEL + h * DN_HEAD:D_MODEL + (h + 1) * DN_HEAD]) for h in range(DN_HEADS)]
    v = [y[:, 2 * D_MODEL + h * DN_HEAD:2 * D_MODEL + (h + 1) * DN_HEAD] for h in range(DN_HEADS)]
    for t in range(T):
        for h, sl in enumerate(heads):
            S = s0_ref[0, 0, h] if t == 0 else s_out_ref[out_layer, 0, h]
            k_col = to_col(k[h][t:t + 1, :])
            q_col = to_col(q[h][t:t + 1, :])
            a = a8[t:t + 1, h:h + 1]
            b = beta8[t:t + 1, h:h + 1]
            kS = jnp.sum(k_col * S, axis=0, keepdims=True)
            S = a * S + k_col * (b * (v[h][t:t + 1, :] - a * kS))
            s_out_ref[out_layer, 0, h] = S
            oacc_ref[t:t + 1, sl] = jnp.sum(q_col * S, axis=0, keepdims=True)
    onorm = on_ref[...]
    for h in range(DN_HEADS):
        sl = slice(h * DN_HEAD, (h + 1) * DN_HEAD)
        o_ref[0, :, sl] = _gated_out(oacc_ref[0:T, sl], x[:, DN_QKV + h * DN_HEAD:DN_QKV + (h + 1) * DN_HEAD], onorm)


def dn_decode(proj, n_seq, conv_buf, S_all, layer, S_new, conv_w, a_log, dt_bias, out_norm):
    M, W = proj.shape
    T = M // n_seq
    row = lambda a: a.reshape(1, DN_HEADS)
    state_spec = pl.BlockSpec((1, 1, DN_HEADS, DN_HEAD, DN_HEAD), lambda b: (layer, b, 0, 0, 0))
    args = [proj.reshape(n_seq, T, W), conv_buf, S_all, conv_w, row(a_log), row(dt_bias), out_norm.reshape(1, DN_HEAD)]
    in_specs = [
        pl.BlockSpec((1, T, W), lambda b: (b, 0, 0)),
        pl.BlockSpec((1, DN_CONV - 1, DN_QKV), lambda b: (b, 0, 0)),
        state_spec,
        pl.BlockSpec((DN_CONV, DN_QKV), lambda b: (0, 0)),
        pl.BlockSpec((1, DN_HEADS), lambda b: (0, 0)),
        pl.BlockSpec((1, DN_HEADS), lambda b: (0, 0)),
        pl.BlockSpec((1, DN_HEAD), lambda b: (0, 0)),
    ]
    if S_new is None:
        aliases, out_layer = {}, layer
        out_state_spec = pl.BlockSpec((S_all.shape[0], 1, DN_HEADS, DN_HEAD, DN_HEAD), lambda b: (0, b, 0, 0, 0))
    else:
        aliases, out_layer = {len(args): 1}, 0
        out_state_spec = state_spec
        args.append(S_new)
        in_specs.append(pl.BlockSpec(memory_space=pl.ANY))
    o, S = pl.pallas_call(
        functools.partial(_dn_decode_body, out_layer=out_layer),
        out_shape=[jax.ShapeDtypeStruct((n_seq, T, D_MODEL), F32), jax.ShapeDtypeStruct(S_all.shape, F32)],
        grid=(n_seq,),
        in_specs=in_specs,
        out_specs=[pl.BlockSpec((1, T, D_MODEL), lambda b: (b, 0, 0)), out_state_spec],
        scratch_shapes=[pltpu.VMEM((16, DN_QKV), F32), pltpu.VMEM((8, D_MODEL), F32)],
        input_output_aliases=aliases,
        compiler_params=_cparams(("parallel",)),
        name="dn_decode",
    )(*args)
    return o.reshape(M, D_MODEL), S


def _bucket_thresholds():
    thr, prev = [], REL_MAX_EXACT
    for d in range(REL_MAX_EXACT, REL_MAX_DIST + 1):
        val = min(REL_MAX_EXACT + int(math.log(d / REL_MAX_EXACT) / math.log(REL_MAX_DIST / REL_MAX_EXACT)
                                      * (REL_BUCKETS - REL_MAX_EXACT)), REL_BUCKETS - 1)
        thr += [d] * (val - prev)
        prev = val
    assert len(thr) == REL_BUCKETS - 1 - REL_MAX_EXACT
    return tuple(thr)


_BUCKET_THR = _bucket_thresholds()
TQ = 256
BAND_ROWS = 128
TK = 512
BAND_TOP = (REL_MAX_DIST + TK + LANES - 1) // LANES * LANES
BAND_W = BAND_TOP + max(TK, WINDOW + TQ)
LOG2E = 1.4426950408889634


def _bucket(d):
    n = jnp.maximum(d, 0)
    big = jnp.full(n.shape, REL_MAX_EXACT, jnp.int32)
    for t in _BUCKET_THR:
        big = big + (n >= t).astype(jnp.int32)
    return jnp.where(n < REL_MAX_EXACT, n, big)


def _bias_lookup(bucket, table_row):
    acc = jnp.zeros(bucket.shape, F32)
    for k in range(REL_BUCKETS):
        acc = acc + jnp.where(bucket == k, table_row(k), 0.0)
    return acc


def _bias_gather(bucket, table_lanes):
    rows = bucket.shape[0]
    table = jnp.broadcast_to(table_lanes, (rows, LANES))
    tiles = [jnp.take_along_axis(table, bucket[:, c:c + LANES], axis=1) for c in range(0, bucket.shape[1], LANES)]
    return tiles[0] if len(tiles) == 1 else jnp.concatenate(tiles, axis=1)


def _head_tables(rel_bias):
    return jnp.pad(rel_bias.T, ((0, 0), (0, LANES - REL_BUCKETS)))


def _bias_cmp_body(tab_ref, o_ref, *, n_cmp):
    q0 = pl.program_id(0) * TQ
    shp = o_ref.shape[1:]
    t = q0 + lax.broadcasted_iota(jnp.int32, shp, 0)
    j = lax.broadcasted_iota(jnp.int32, shp, 1)
    d = t - (j * CMP_STRIDE + CMP_BLOCK - 1)
    dead = (d < 0) | (j >= n_cmp)
    bucket = _bucket(d)
    for h in range(NSA_HEADS):
        o_ref[h] = jnp.where(dead, NEG, _bias_gather(bucket, tab_ref[h:h + 1, :]))


def bias_cmp_prompt(rel_bias, T):
    n_sub = T // CMP_STRIDE
    return pl.pallas_call(
        functools.partial(_bias_cmp_body, n_cmp=n_sub - 1),
        out_shape=jax.ShapeDtypeStruct((NSA_HEADS, T, n_sub), F32),
        grid=(T // TQ,),
        in_specs=[pl.BlockSpec((NSA_HEADS, LANES), lambda i: (0, 0))],
        out_specs=pl.BlockSpec((NSA_HEADS, TQ, n_sub), lambda i: (0, i, 0)),
        compiler_params=_cparams(("parallel",)),
        name="bias_cmp",
    )(_head_tables(rel_bias))


def _bias_band_body(tab_ref, o_ref, ow_ref):
    h = pl.program_id(0)
    shp = o_ref.shape[1:]
    d = BAND_TOP + lax.broadcasted_iota(jnp.int32, shp, 0) - lax.broadcasted_iota(jnp.int32, shp, 1)
    bias = LOG2E * _bias_gather(_bucket(d), tab_ref[pl.ds(h, 1), :])
    o_ref[0] = jnp.where(d < 0, NEG, bias)
    ow_ref[0] = jnp.where((d < 0) | (d >= WINDOW), NEG, bias)


def bias_band(rel_bias):
    shape = jax.ShapeDtypeStruct((NSA_HEADS, BAND_ROWS, BAND_W), F32)
    spec = pl.BlockSpec((1, BAND_ROWS, BAND_W), lambda h: (h, 0, 0))
    return pl.pallas_call(
        _bias_band_body,
        out_shape=[shape, shape],
        grid=(NSA_HEADS,),
        in_specs=[pl.BlockSpec((NSA_HEADS, LANES), lambda h: (0, 0))],
        out_specs=[spec, spec],
        compiler_params=_cparams(("parallel",)),
        name="bias_band",
    )(_head_tables(rel_bias))


def _bias_decode_body(tabc_ref, cmp_ref, sel_ref, win_ref, *, past, tq):
    shp = (PAGE, NSA_HEADS * tq)
    i = lax.broadcasted_iota(jnp.int32, shp, 0)
    c = lax.broadcasted_iota(jnp.int32, shp, 1)
    qpos = past + (c & (tq - 1))
    row = lambda k: tabc_ref[k:k + 1, :]

    def table(d, dead):
        return jnp.where(dead | (d < 0), NEG, _bias_lookup(_bucket(d), row))

    ic = lax.broadcasted_iota(jnp.int32, cmp_ref.shape, 0)
    qc = past + (lax.broadcasted_iota(jnp.int32, cmp_ref.shape, 1) & (tq - 1))
    cmp_ref[...] = table(qc - (ic * CMP_STRIDE + CMP_BLOCK - 1), ic < 0)
    n_pages = past // PAGE
    for p in range(n_pages):
        sel_ref[p] = table(qpos - (p * PAGE + i), i < 0)
    sel_ref[n_pages] = table(qpos - (past + i), i >= tq)
    n_wt = WINDOW // PAGE
    for t in range(n_wt):
        d = qpos - (past - WINDOW + t * PAGE + i)
        win_ref[t] = table(d, d >= WINDOW)
    win_ref[n_wt] = table(qpos - (past + i), i >= tq)


def bias_decode(rel_bias, past, tq):
    assert tq & (tq - 1) == 0
    tabc = jnp.repeat(rel_bias, tq, axis=1)
    n_pages = past // PAGE
    nc = NSA_HEADS * tq
    return pl.pallas_call(
        functools.partial(_bias_decode_body, past=past, tq=tq),
        out_shape=[jax.ShapeDtypeStruct((past // CMP_STRIDE, nc), F32),
                   jax.ShapeDtypeStruct((n_pages + 1, PAGE, nc), F32),
                   jax.ShapeDtypeStruct((WINDOW // PAGE + 1, PAGE, nc), F32)],
        name="bias_decode",
    )(tabc)


def _cmp_weights(cmp_pos_w, w_cmp):
    w = jnp.concatenate([cmp_pos_w[0].reshape(CMP_BLOCK, -1), cmp_pos_w[1].reshape(CMP_BLOCK, -1)], axis=1)
    blocks = w_cmp.reshape(2 * NSA_KVH, NSA_HD, NSA_HD)
    n = 2 * NSA_KVH
    wbd = (jnp.eye(n, dtype=F32)[:, None, :, None] * blocks[:, :, None, :]).reshape(n * NSA_HD, n * NSA_HD)
    return w[:CMP_STRIDE], w[CMP_STRIDE:], wbd.astype(BF16)


def _pool16(x, w):
    n = x.shape[0] // CMP_STRIDE
    return jnp.sum(x.reshape(n, CMP_STRIDE, x.shape[1]) * w[None], axis=1)


def _compress_prompt_body(x_ref, wlo_ref, whi_ref, wbd_ref, o_ref, lo_ref, hi_ref):
    T = x_ref.shape[0]
    n_sub = T // CMP_STRIDE
    step = 512
    for c in range(T // step):
        xs = x_ref[c * step:(c + 1) * step, :]
        r = slice(c * step // CMP_STRIDE, (c + 1) * step // CMP_STRIDE)
        lo_ref[r, :] = _pool16(xs, wlo_ref[...])
        hi_ref[r, :] = _pool16(xs, whi_ref[...])
    hi_ref[n_sub:n_sub + 8, :] = jnp.zeros((8, hi_ref.shape[1]), F32)
    blocks = lo_ref[...] + hi_ref[1:n_sub + 1, :]
    o_ref[...] = _dot(blocks, wbd_ref[...]).astype(BF16)


def compress_prompt(rows, n_seq, T, wlo, whi, wbd):
    n_sub = T // CMP_STRIDE
    W = 2 * NSA_KVH * NSA_HD
    return pl.pallas_call(
        _compress_prompt_body,
        out_shape=jax.ShapeDtypeStruct((n_seq * n_sub, W), BF16),
        grid=(n_seq,),
        in_specs=[pl.BlockSpec((T, W), lambda b: (b, 0)),
                  pl.BlockSpec((CMP_STRIDE, W), lambda b: (0, 0)),
                  pl.BlockSpec((CMP_STRIDE, W), lambda b: (0, 0)),
                  pl.BlockSpec((W, W), lambda b: (0, 0))],
        out_specs=pl.BlockSpec((n_sub, W), lambda b: (b, 0)),
        scratch_shapes=[pltpu.VMEM((n_sub, W), F32), pltpu.VMEM((n_sub + 8, W), F32)],
        compiler_params=_cparams(("parallel",)),
        name="compress_prompt",
    )(rows, wlo, whi, wbd)


def _masked_softmax(s, mask, axis):
    l = jnp.where(mask, s, NEG)
    m = jnp.max(l, axis=axis, keepdims=True)
    e = jnp.where(mask, jnp.exp(l - m), 0.0)
    return e / jnp.maximum(jnp.sum(e, axis=axis, keepdims=True), 1e-30)


def _split3(x):
    hi = x.astype(BF16)
    r = x - hi.astype(F32)
    mid = r.astype(BF16)
    lo = (r - mid.astype(F32)).astype(BF16)
    return hi, mid, lo


def _topk_rows(score, blk, n_rows, n_pick):
    beaten = jnp.zeros(score.shape, F32)
    for other in range(n_rows):
        row = score[other:other + 1, :]
        beaten = beaten + jnp.where(blk > other, jnp.where(row >= score, 1.0, 0.0), jnp.where(row > score, 1.0, 0.0))
    return jnp.where((beaten < n_pick) & (blk < n_rows), 1.0, 0.0)


def _attn_prompt_body(q_ref, g_ref, ka_ref, vs_ref, kw_ref, vw_ref, kc_ref, vc_ref, bc_ref, band_ref, bandw_ref,
                      o_ref, qp_scr, qw_scr, qa_scr, s_scr, p_scr, m_scr, acc_scr, sc_scr, sw_scr, pw_scr):
    qb = pl.program_id(2)
    q0 = qb * TQ
    G = NSA_GROUP
    R = G * TQ
    HD = NSA_HD
    groups = [slice(g * TQ, (g + 1) * TQ) for g in range(G)]
    slabs = [(slice(g * TQ + h, g * TQ + h + BAND_ROWS), g, h) for g in range(G) for h in range(0, TQ, BAND_ROWS)]
    qblk = q_ref[...] * (HD ** -0.5)
    Q = jnp.concatenate([qblk[:, g * HD:(g + 1) * HD] for g in range(G)], axis=0)
    zeros = jnp.zeros((R, HD), F32)
    qp_scr[...] = jnp.concatenate([Q, zeros], axis=1).astype(BF16)
    qw_scr[...] = jnp.concatenate([Q * LOG2E, zeros], axis=1).astype(BF16)

    left = lax.broadcasted_iota(jnp.int32, (TQ, 2 * HD), 1) < HD

    def normalized(acc):
        tiles = []
        for g in range(0, G, 2):
            a0, a1 = acc[groups[g]], acc[groups[g + 1]]
            r0, r1 = pltpu.roll(a0, HD, 1), pltpu.roll(a1, HD, 1)
            num = jnp.where(left, a0, r1)
            den = jnp.where(left, r0, a1)
            tiles.append(num / jnp.maximum(den, 1e-30))
        return jnp.concatenate(tiles, axis=1)

    gates = _split3(_sigmoid(g_ref[0]))
    W = G * HD
    gi = lax.broadcasted_iota(jnp.int32, (3 * G, W), 0)
    gc = lax.broadcasted_iota(jnp.int32, (3 * G, W), 1) // HD

    def gate(branch):
        spread = (gi == 3 * gc + branch).astype(BF16)
        return sum(jnp.dot(part, spread, preferred_element_type=F32) for part in gates)

    WK = WINDOW + TQ
    kw0 = pl.multiple_of(jnp.maximum(q0 - WINDOW, 0), LANES)
    sw_scr[...] = _dot_nt(qw_scr[...], kw_ref[pl.ds(kw0, WK), :])
    for r, g, h in slabs:
        cw0 = pl.multiple_of(BAND_TOP - (q0 + h - kw0), LANES)
        s = sw_scr[r, :] + bandw_ref[g, :, pl.ds(cw0, WK)]
        m = jnp.maximum(jnp.max(s, axis=1, keepdims=True), 0.5 * NEG)
        pw_scr[r, :] = jnp.exp2(s - m).astype(BF16)
    o_ref[...] = normalized(jnp.dot(pw_scr[...], vw_ref[pl.ds(kw0, WK), :], preferred_element_type=F32)) * gate(2)

    n_sub = bc_ref.shape[2]
    n_blk = n_sub * CMP_STRIDE // SEL_BLOCK
    n_pick = min(N_SEL, n_blk)
    per = SEL_BLOCK // CMP_STRIDE
    bb = lax.broadcasted_iota(jnp.int32, (n_blk, n_sub), 0)
    mm = lax.broadcasted_iota(jnp.int32, (n_blk, n_sub), 1)
    pool = ((mm // per == bb).astype(F32) + ((mm + 1) // per == bb).astype(F32)).astype(BF16)
    s_scr[:, :n_sub] = _dot_nt(qp_scr[...], kc_ref[...])
    for h in range(0, TQ, BAND_ROWS):
        imp = None
        for g in range(G):
            r = slice(g * TQ + h, g * TQ + h + BAND_ROWS)
            s = s_scr[r, :n_sub] + bc_ref[g, h:h + BAND_ROWS, :]
            e = jnp.exp(s - jnp.maximum(jnp.max(s, axis=1, keepdims=True), 0.5 * NEG))
            p_scr[r, :n_sub] = e.astype(BF16)
            p = e / jnp.maximum(jnp.sum(e, axis=1, keepdims=True), 1e-30)
            imp = p if imp is None else imp + p
        sc_scr[:, h:h + BAND_ROWS] = sum(_dot_nt(pool, part) for part in _split3(imp))
    o_ref[...] += normalized(jnp.dot(p_scr[:, :n_sub], vc_ref[...], preferred_element_type=F32)) * gate(0)

    p_slc = sc_scr[...]
    blk = lax.broadcasted_iota(jnp.int32, (n_blk, TQ), 0)
    t = q0 + lax.broadcasted_iota(jnp.int32, (n_blk, TQ), 1)
    cur = t // SEL_BLOCK
    forced = (blk == 0) | (blk == cur) | (blk == cur - 1)
    score = jnp.where(forced, 1e4, jnp.where(blk * SEL_BLOCK <= t, p_slc, -1.0))
    sc_scr[...] = score
    beaten = jnp.zeros((n_blk, TQ), F32)
    for other in range(n_blk):
        row = sc_scr[other:other + 1, :]
        ge = jnp.where(row >= score, 1.0, 0.0)
        gt = jnp.where(row > score, 1.0, 0.0)
        beaten = beaten + jnp.where(blk > other, ge, gt)
    selneg = jnp.where(beaten < n_pick, 0.0, NEG)
    selneg_q = selneg.T
    if n_blk < HD:
        selneg_q = jnp.concatenate([selneg_q, jnp.zeros((TQ, HD - n_blk), F32)], axis=1)
    qa_scr[...] = jnp.concatenate([jnp.concatenate([Q[r] * LOG2E, selneg_q], axis=1) for r in groups],
                                  axis=0).astype(BF16)

    def soften(kt, pv):
        rowmax = []
        for r, g, h in slabs:
            c0 = pl.multiple_of(jnp.maximum(BAND_TOP - (q0 + h - kt * TK), 0), LANES)
            s = s_scr[r, :] + band_ref[g, :, pl.ds(c0, TK)]
            s_scr[r, :] = s
            rowmax.append(jnp.max(s, axis=1, keepdims=True))
        for i, (r, g, h) in enumerate(slabs):
            m_old = m_scr[r, :]
            m_new = jnp.maximum(m_old, rowmax[i])
            m_scr[r, :] = m_new
            a = acc_scr[r, :] if pv is None else acc_scr[r, :] + pv[r]
            acc_scr[r, :] = jnp.exp2(m_old - m_new) * a
            for c in range(TK // LANES):
                cs = slice(c * LANES, (c + 1) * LANES)
                p_scr[r, cs] = jnp.exp2(s_scr[r, cs] - m_new).astype(BF16)

    def sel_qk(kt):
        s_scr[...] = _dot_nt(qa_scr[...], ka_ref[pl.ds(pl.multiple_of(kt * TK, TK), TK), :])

    def sel_pv(kt):
        v = vs_ref[pl.ds(pl.multiple_of(kt * TK, TK), TK), :]
        return jnp.dot(p_scr[...], v, preferred_element_type=F32)

    m_scr[...] = jnp.full(m_scr.shape, 0.5 * NEG, F32)
    acc_scr[...] = jnp.zeros(acc_scr.shape, F32)
    sel_qk(0)
    soften(0, None)

    def step(kt, carry):
        pv = sel_pv(kt - 1)
        sel_qk(kt)
        soften(kt, pv)
        return carry

    n_kt = (q0 + TQ + TK - 1) // TK
    lax.fori_loop(1, n_kt, step, 0)
    o_ref[...] += normalized(acc_scr[...] + sel_pv(n_kt - 1)) * gate(1)


def attn_prompt_operands(rows_bf, cmp_p, n_seq, T):
    M = n_seq * T
    W = NSA_KVH * NSA_HD

    def per_head(x, aux):
        x = x.reshape(x.shape[0], NSA_KVH, NSA_HD)
        out = jnp.concatenate([x, jnp.broadcast_to(aux[:, None, :], x.shape)], axis=2)
        return out.reshape(x.shape[0], NSA_KVH * 2 * NSA_HD)

    blk_of_key = (jnp.arange(M, dtype=jnp.int32) % T) // SEL_BLOCK
    onehot = (blk_of_key[:, None] == jnp.arange(NSA_HD, dtype=jnp.int32)[None, :]).astype(BF16)
    zeros, ones = jnp.zeros((M, NSA_HD), BF16), jnp.ones((M, NSA_HD), BF16)
    kind = lambda i: rows_bf[:M, i * W:(i + 1) * W]
    nc = cmp_p.shape[0]
    return (per_head(kind(2), onehot), per_head(kind(3), ones), per_head(kind(4), zeros), per_head(kind(5), ones),
            per_head(cmp_p[:, :W], zeros[:nc]), per_head(cmp_p[:, W:], ones[:nc]))


def attn_prompt(proj, n_seq, T, operands, bias_c, bands):
    M = n_seq * T
    NQ = T // TQ
    n_sub = T // CMP_STRIDE
    n_blk = T // SEL_BLOCK
    assert n_blk <= NSA_HD and T % TK == 0 and T >= WINDOW + TQ and n_sub <= TK
    gl = proj[:M, NSA_HEADS * NSA_HD:NSA_HEADS * NSA_HD + 3 * NSA_HEADS]
    gl = gl.reshape(M, NSA_KVH, 3 * NSA_GROUP).transpose(1, 0, 2)
    pair = 2 * NSA_HD
    ks_a, vs_a, kw_a, vw_a, kc_a, vc_a = operands
    kv_spec = pl.BlockSpec((T, pair), lambda k, b, i: (b, k))
    cmp_spec = pl.BlockSpec((n_sub, pair), lambda k, b, i: (b, k))
    band_spec = pl.BlockSpec((NSA_GROUP, BAND_ROWS, BAND_W), lambda k, b, i: (k, 0, 0))
    WO = NSA_GROUP * NSA_HD
    R = NSA_GROUP * TQ
    WK = WINDOW + TQ
    band, bandw = bands
    return pl.pallas_call(
        _attn_prompt_body,
        out_shape=jax.ShapeDtypeStruct((M, NSA_HEADS * NSA_HD), F32),
        grid=(NSA_KVH, n_seq, NQ),
        in_specs=[
            pl.BlockSpec((TQ, WO), lambda k, b, i: (b * NQ + i, k)),
            pl.BlockSpec((1, TQ, 3 * NSA_GROUP), lambda k, b, i: (k, b * NQ + i, 0)),
            kv_spec, kv_spec, kv_spec, kv_spec,
            cmp_spec, cmp_spec,
            pl.BlockSpec((NSA_GROUP, TQ, n_sub), lambda k, b, i: (k, i, 0)),
            band_spec, band_spec,
        ],
        out_specs=pl.BlockSpec((TQ, WO), lambda k, b, i: (b * NQ + i, k)),
        scratch_shapes=[pltpu.VMEM((R, pair), BF16), pltpu.VMEM((R, pair), BF16), pltpu.VMEM((R, pair), BF16),
                        pltpu.VMEM((R, TK), F32), pltpu.VMEM((R, TK), BF16),
                        pltpu.VMEM((R, LANES), F32), pltpu.VMEM((R, pair), F32),
                        pltpu.VMEM((n_blk, TQ), F32),
                        pltpu.VMEM((R, WK), F32), pltpu.VMEM((R, WK), BF16)],
        compiler_params=_cparams(("parallel", "parallel", "arbitrary")),
        name="attn_prompt",
    )(proj, gl, ks_a, vs_a, kw_a, vw_a, kc_a, vc_a, bias_c, band, bandw)


def _compress_decode_body(pt_ref, *refs, n_pages):
    pages = refs[:n_pages]
    new_ref, wlo_ref, whi_ref, wbd_ref, o_ref = refs[n_pages:]
    per = PAGE // CMP_STRIDE
    n_sub = n_pages * per
    W = wlo_ref.shape[0]
    pos = lax.broadcasted_iota(jnp.int32, (2 * PAGE, n_sub), 0)
    blk = lax.broadcasted_iota(jnp.int32, (2 * PAGE, n_sub), 1)

    def pooled(x, p):
        sub = p * per + (pos % PAGE) // CMP_STRIDE
        place = (blk == jnp.where(pos < PAGE, sub, sub - 1)).astype(BF16)
        y = jnp.concatenate([x * wlo_ref[...], x * whi_ref[...]], axis=1).astype(BF16)
        return jnp.dot(y, place, preferred_element_type=F32)

    blocks = pooled(new_ref[0], n_pages)
    for p in range(n_pages):
        blocks = blocks + pooled(pages[p][0].reshape(W, PAGE), p)
    o_ref[0] = jnp.dot(wbd_ref[...], blocks.astype(BF16), preferred_element_type=F32).astype(BF16)


def compress_decode(cache_t, page_table, new_rows, wlo, whi, wbd):
    nb, n_pages = page_table.shape
    W = 2 * NSA_KVH * NSA_HD
    tq = new_rows.shape[1]
    assert tq <= CMP_STRIDE
    n_sub = n_pages * PAGE // CMP_STRIDE
    wlo_t = jnp.tile(wlo.T, (1, PAGE // CMP_STRIDE))
    whi_t = jnp.tile(whi.T, (1, PAGE // CMP_STRIDE))
    new_t = jnp.pad(new_rows[:, :, :W].transpose(0, 2, 1), ((0, 0), (0, 0), (0, PAGE - tq)))
    page_spec = lambda p: pl.BlockSpec((1, 2, W // 2, PAGE), lambda b, pt: (pt[b, p], 0, 0, 0))
    const = lambda shape: pl.BlockSpec(shape, lambda b, pt: (0,) * len(shape))
    return pl.pallas_call(
        functools.partial(_compress_decode_body, n_pages=n_pages),
        out_shape=jax.ShapeDtypeStruct((nb, W, n_sub), BF16),
        grid_spec=pltpu.PrefetchScalarGridSpec(
            num_scalar_prefetch=1,
            grid=(nb,),
            in_specs=[page_spec(p) for p in range(n_pages)] + [
                pl.BlockSpec((1, W, PAGE), lambda b, pt: (b, 0, 0)),
                const((W, PAGE)), const((W, PAGE)), const((W, W))],
            out_specs=pl.BlockSpec((1, W, n_sub), lambda b, pt: (b, 0, 0)),
        ),
        compiler_params=_cparams(("parallel",)),
        name="compress_decode",
    )(page_table, *([cache_t] * n_pages), new_t, wlo_t, whi_t, wbd.T)


def _attn_decode_body(pt_ref, *refs, n_pages, tq):
    pages = refs[:n_pages]
    q_ref, qt_ref, g_ref, new_ref, win_ref, cmp_ref, bcmp_ref, bsel_ref, bwin_ref, o_ref = refs[n_pages:]
    W = NSA_KVH * NSA_HD
    NC = NSA_HEADS * tq
    Qbd = q_ref[0]
    QbdT = qt_ref[0]
    past = n_pages * PAGE
    tn = (((0,), (0,)), ((), ()))

    def logits_t(kT):
        return lax.dot_general(kT.astype(BF16), QbdT, tn, preferred_element_type=F32)

    cm = cmp_ref[0]
    bc = bcmp_ref[...]
    p_c = _masked_softmax(logits_t(cm[:W]) + bc, bc > 0.5 * NEG, 0)
    o_cmp = jnp.dot(cm[W:], p_c.astype(BF16), preferred_element_type=F32)

    n_sub = cm.shape[1]
    per = SEL_BLOCK // CMP_STRIDE
    n_blk = past // SEL_BLOCK + 1
    nb_pad = (n_blk + 7) // 8 * 8
    ci = lax.broadcasted_iota(jnp.int32, (NC, NC), 0)
    cj = lax.broadcasted_iota(jnp.int32, (NC, NC), 1)
    gq = NSA_GROUP * tq
    same = ((ci // gq == cj // gq) & ((ci & (tq - 1)) == (cj & (tq - 1)))).astype(BF16)
    imp = sum(jnp.dot(part, same, preferred_element_type=F32) for part in _split3(p_c))
    bb = lax.broadcasted_iota(jnp.int32, (nb_pad, n_sub), 0)
    mm = lax.broadcasted_iota(jnp.int32, (nb_pad, n_sub), 1)
    pool = ((mm // per == bb).astype(F32) + ((mm + 1) // per == bb).astype(F32)).astype(BF16)
    p_slc = sum(jnp.dot(pool, part, preferred_element_type=F32) for part in _split3(imp))
    blk = lax.broadcasted_iota(jnp.int32, (nb_pad, NC), 0)
    qpos = past + (lax.broadcasted_iota(jnp.int32, (nb_pad, NC), 1) & (tq - 1))
    cur = qpos // SEL_BLOCK
    forced = (blk == 0) | (blk == cur) | (blk == cur - 1)
    score = jnp.where(forced, 1e4, jnp.where(blk * SEL_BLOCK <= qpos, p_slc, -1.0))
    score = jnp.where(blk < n_blk, score, -3.0)
    sel = _topk_rows(score, blk, n_blk, min(N_SEL, n_blk))

    def attend(tiles):
        logits = []
        for k, _, bias, mask, stored in tiles:
            b = bias()
            logits.append(jnp.where(mask(b), (logits_t(k()) if stored else _dot_nt(k(), Qbd)) + b, NEG))
        m = functools.reduce(jnp.maximum, [jnp.max(s, axis=0, keepdims=True) for s in logits])
        den = jnp.zeros((1, NC), F32)
        acc = jnp.zeros((W, NC), F32)
        for s, (_, v, _, _, stored) in zip(logits, tiles):
            p = jnp.where(s > 0.5 * NEG, jnp.exp(s - m), 0.0)
            den = den + jnp.sum(p, axis=0, keepdims=True)
            if stored:
                acc = acc + jnp.dot(v().astype(BF16), p.astype(BF16), preferred_element_type=F32)
            else:
                acc = acc + lax.dot_general(v().astype(BF16), p.astype(BF16), tn, preferred_element_type=F32)
        return acc / jnp.maximum(den, 1e-30)

    new = new_ref[0]
    pad = jnp.zeros((8 - tq, W), F32)
    new_tile = lambda kind: (lambda: jnp.concatenate([new[:, kind * W:(kind + 1) * W], pad], axis=0))

    half = lax.broadcasted_iota(jnp.int32, (PAGE, NC), 0) < SEL_BLOCK
    def page_mask(p):
        b0 = p * (PAGE // SEL_BLOCK)
        return lambda bias: (jnp.where(half, sel[b0:b0 + 1, :], sel[b0 + 1:b0 + 2, :]) > 0.5) & (bias > 0.5 * NEG)

    tiles = [(lambda p=p: pages[p][0, 0], lambda p=p: pages[p][0, 1], lambda p=p: bsel_ref[p], page_mask(p), True)
             for p in range(n_pages)]
    tiles.append((new_tile(2), new_tile(3), lambda: bsel_ref[n_pages][0:8],
                  lambda bias: (sel[n_blk - 1:n_blk, :] > 0.5) & (bias > 0.5 * NEG), False))
    o_sel = attend(tiles)

    causal = lambda bias: bias > 0.5 * NEG
    tiles = []
    for t in range(WINDOW // PAGE):
        ts = slice(t * PAGE, (t + 1) * PAGE)
        tiles.append((lambda ts=ts: win_ref[0, 0, :, ts], lambda ts=ts: win_ref[0, 1, :, ts],
                      lambda t=t: bwin_ref[t], causal, True))
    tiles.append((new_tile(4), new_tile(5), lambda: bwin_ref[WINDOW // PAGE][0:8], causal, False))
    o_win = attend(tiles)

    gt = _sigmoid(g_ref[0])
    o = o_cmp * gt[0:1] + o_sel * gt[1:2] + o_win * gt[2:3]
    kvh_of_col = lax.broadcasted_iota(jnp.int32, (NSA_HD, NC), 1) // gq
    out = jnp.zeros((NSA_HD, NC), F32)
    for k in range(NSA_KVH):
        out = out + jnp.where(kvh_of_col == k, o[k * NSA_HD:(k + 1) * NSA_HD, :], 0.0)
    o_ref[0] = out


def attn_decode(proj, cache_t, page_table, new_rows, win_t, cmp_d, bias_tabs):
    nb, n_pages = page_table.shape
    tq = new_rows.shape[1]
    W = NSA_KVH * NSA_HD
    NC = NSA_HEADS * tq
    q = proj[:, :NSA_HEADS * NSA_HD] * (NSA_HD ** -0.5)
    q = q.reshape(nb, tq, NSA_KVH, NSA_GROUP, NSA_HD).transpose(0, 2, 3, 1, 4)
    qbd = q[:, :, :, :, None, :] * jnp.eye(NSA_KVH, dtype=F32)[None, :, None, None, :, None]
    qbd = qbd.reshape(nb, NC, W).astype(BF16)
    qbd_t = qbd.transpose(0, 2, 1)
    gl = proj[:, NSA_HEADS * NSA_HD:NSA_HEADS * NSA_HD + 3 * NSA_HEADS]
    gl = gl.reshape(nb, tq, NSA_HEADS, 3).transpose(0, 3, 2, 1).reshape(nb, 3, NC)
    bcmp, bsel, bwin = bias_tabs
    page_spec = lambda p: pl.BlockSpec((1, 2, W, PAGE), lambda b, pt: (pt[b, p], 1, 0, 0))
    const = lambda shape: pl.BlockSpec(shape, lambda b, pt: (0,) * len(shape))
    per_b = lambda shape: pl.BlockSpec((1,) + shape, lambda b, pt: (b,) + (0,) * len(shape))
    o = pl.pallas_call(
        functools.partial(_attn_decode_body, n_pages=n_pages, tq=tq),
        out_shape=jax.ShapeDtypeStruct((nb, NSA_HD, NC), F32),
        grid_spec=pltpu.PrefetchScalarGridSpec(
            num_scalar_prefetch=1,
            grid=(nb,),
            in_specs=[page_spec(p) for p in range(n_pages)] + [
                per_b((NC, W)), per_b((W, NC)), per_b((3, NC)), per_b((tq, 6 * W)), per_b((2, W, WINDOW)),
                per_b((2 * W, cmp_d.shape[2])),
                const(bcmp.shape), const(bsel.shape), const(bwin.shape)],
            out_specs=per_b((NSA_HD, NC)),
        ),
        compiler_params=_cparams(("parallel",)),
        name="attn_decode",
    )(page_table, *([cache_t] * n_pages), qbd, qbd_t, gl, new_rows, win_t, cmp_d, bcmp, bsel, bwin)
    return o.reshape(nb, NSA_HD, NSA_HEADS, tq).transpose(0, 3, 2, 1).reshape(nb * tq, NSA_HEADS * NSA_HD)


def _pad_cols(w, n):
    return jnp.pad(w, ((0, 0), (0, n - w.shape[1])))


def kernel(x_prompt, x_sample, state_dn_S, state_dn_conv, cache_kv, state_win_kv, page_table, norm_mix, norm_ffn, norm_kv, norm_final, ffn_w_in, ffn_w_out, dn_w_in, dn_conv_w, dn_A_log, dn_dt_bias, dn_out_norm, dn_w_out, nsa_w_kv, nsa_cmp_pos_w, nsa_w_cmp, nsa_w_in, nsa_w_out, rel_bias):
    B, T, D = x_prompt.shape
    NB, TS, _ = x_sample.shape
    Mp, Ms = B * T, NB * TS
    past = page_table.shape[1] * PAGE
    x = jnp.concatenate([x_prompt.reshape(Mp, D), x_sample.reshape(Ms, D)], axis=0)

    ffn_in_bf, ffn_out_bf = to_bf16(ffn_w_in), to_bf16(ffn_w_out)
    p_S, p_conv, s_conv = [], [], []
    s_S = None
    for l in range(N_A_LAYERS):
        w_in = _pad_cols(dn_w_in[l], 4 * D + LANES).astype(BF16)
        proj = linear(x, w_in, norm_w=norm_mix[l], tn=(4 * D + LANES) // 3)
        o_p, S_p = dn_prompt(proj, B, T, dn_conv_w[l], dn_A_log[l], dn_dt_bias[l], dn_out_norm[l])
        o_s, s_S = dn_decode(proj[Mp:], NB, state_dn_conv[l], state_dn_S, l, s_S, dn_conv_w[l], dn_A_log[l],
                             dn_dt_bias[l], dn_out_norm[l])
        tail = DN_CONV - 1
        qkv_s = proj[Mp:, :DN_QKV].reshape(NB, TS, DN_QKV)
        p_S.append(S_p)
        p_conv.append(jnp.stack([proj[(b + 1) * T - tail:(b + 1) * T, :DN_QKV] for b in range(B)]))
        s_conv.append(jnp.concatenate([state_dn_conv[l], qkv_s], axis=1)[:, TS:])
        x = linear(o_p, dn_w_out[l].astype(BF16), residual=x, x_tail=o_s, tn=D)
        x = ffn(x, norm_ffn[l], ffn_in_bf, ffn_out_bf, l)

    W = NSA_KVH * NSA_HD
    rows, rows_bf, kv_t, win_t_p = shared_rows(x, norm_kv, nsa_w_kv.astype(BF16), B, T, 4 * W)
    rows_s = rows[Mp:].reshape(NB, TS, 6, NSA_KVH, NSA_HD)
    p_kv_rows = kv_t.reshape(B, 4, NSA_KVH, NSA_HD, T).transpose(0, 4, 1, 2, 3)
    p_win_kv = win_t_p.reshape(B, 2, NSA_KVH, NSA_HD, T).transpose(0, 4, 1, 2, 3)[:, T - min(WINDOW, T):]
    s_kv_rows = rows_s[:, :, :4]
    s_win_kv = jnp.concatenate([state_win_kv, rows_s[:, :, 4:]], axis=1)[:, TS:]

    wlo, whi, wbd = _cmp_weights(nsa_cmp_pos_w, nsa_w_cmp)
    cmp_p = compress_prompt(rows, B, T, wlo, whi, wbd)
    attn_ops = attn_prompt_operands(rows_bf, cmp_p, B, T)
    new_rows = rows[Mp:].reshape(NB, TS, 6 * W)
    cache_t = cache_kv.transpose(0, 2, 3, 4, 1).reshape(cache_kv.shape[0], 4, W, PAGE)
    win_t = state_win_kv.transpose(0, 2, 3, 4, 1).reshape(NB, 2, W, state_win_kv.shape[1])
    cmp_d = compress_decode(cache_t, page_table, new_rows, wlo, whi, wbd)
    bias_c = bias_cmp_prompt(rel_bias, T)
    bands = bias_band(rel_bias)
    bias_d = bias_decode(rel_bias, past, TS)

    for j in range(N_B_LAYERS):
        l = N_A_LAYERS + j
        w_in = _pad_cols(nsa_w_in[j], D + LANES).astype(BF16)
        proj = linear(x, w_in, norm_w=norm_mix[l], tn=D + LANES)
        o_p = attn_prompt(proj, B, T, attn_ops, bias_c, bands)
        o_s = attn_decode(proj[Mp:], cache_t, page_table, new_rows, win_t, cmp_d, bias_d)
        x = linear(o_p, nsa_w_out[j].astype(BF16), residual=x, x_tail=o_s, tn=D)
        x = ffn(x, norm_ffn[l], ffn_in_bf, ffn_out_bf, l)

    y_p, y_s = final_norm(x, norm_final, Mp)
    return (y_p.reshape(B, T, D), y_s.reshape(NB, TS, D),
            jnp.stack(p_S), jnp.stack(p_conv), p_kv_rows, p_win_kv,
            s_S, jnp.stack(s_conv), s_kv_rows, s_win_kv)
```

```python
import functools
import math

import jax
import jax.numpy as jnp
from jax import lax
from jax.experimental import pallas as pl
from jax.experimental.pallas import tpu as pltpu

F32 = jnp.float32
BF16 = jnp.bfloat16

D_MODEL = 1024
N_A_LAYERS = 2
N_B_LAYERS = 2
NORM_EPS = 1e-6
DN_HEADS = 8
DN_HEAD = 128
DN_QKV = 3 * D_MODEL
DN_CONV = 4
DN_CHUNK = 64
NSA_HEADS = 16
NSA_HD = 64
NSA_KVH = 4
NSA_GROUP = 4
CMP_STRIDE = 16
CMP_BLOCK = 32
SEL_BLOCK = 64
N_SEL = 16
WINDOW = 512
PAGE = 128
REL_BUCKETS = 32
REL_MAX_EXACT = 16
REL_MAX_DIST = 1024
NEG = -1e30

V7X_VMEM_LIMIT = 56 * 1024 * 1024
LANES = 128


def _cparams(sem):
    return pltpu.CompilerParams(dimension_semantics=sem, vmem_limit_bytes=V7X_VMEM_LIMIT)


def _rms(x, w):
    ms = jnp.mean(x * x, axis=-1, keepdims=True)
    return x * lax.rsqrt(ms + NORM_EPS) * w


def _silu(x):
    return x * (1.0 / (1.0 + jnp.exp(-x)))


def _sigmoid(x):
    return 1.0 / (1.0 + jnp.exp(-x))


def _softplus(x):
    return jnp.maximum(x, 0.0) + jnp.log1p(jnp.exp(-jnp.abs(x)))


def _dot(a, b):
    return jnp.dot(a.astype(BF16), b.astype(BF16), preferred_element_type=F32)


def _dot_nt(a, b):
    return lax.dot_general(a.astype(BF16), b.astype(BF16), (((1,), (1,)), ((), ())),
                           preferred_element_type=F32)


def _bdot_dims(a, b, ca, cb):
    return lax.dot_general(a.astype(BF16), b.astype(BF16), (((ca,), (cb,)), ((0,), (0,))),
                           preferred_element_type=F32)


def _bdot(a, b):
    return _bdot_dims(a, b, 2, 1)


def _bdot_nt(a, b):
    return _bdot_dims(a, b, 2, 2)


def _bdot_tn(a, b):
    return _bdot_dims(a, b, 1, 1)


def _linear_body(*refs, has_norm, has_res, n_head_tiles):
    it = iter(refs)
    x_ref = next(it)
    xt_ref = next(it) if n_head_tiles else None
    nw_ref = next(it) if has_norm else None
    w_ref = next(it)
    res_ref = next(it) if has_res else None
    o_ref = next(it)
    xn_ref = next(it)

    def stage(src_ref):
        x = src_ref[...]
        if has_norm:
            x = _rms(x, nw_ref[...])
        xn_ref[...] = x.astype(BF16)

    first = pl.program_id(1) == 0
    if n_head_tiles:
        in_head = pl.program_id(0) < n_head_tiles
        pl.when(first & in_head)(lambda: stage(x_ref))
        pl.when(first & jnp.logical_not(in_head))(lambda: stage(xt_ref))
    else:
        pl.when(first)(lambda: stage(x_ref))

    acc = jnp.dot(xn_ref[...], w_ref[...], preferred_element_type=F32)
    if has_res:
        acc = acc + res_ref[...]
    o_ref[...] = acc


def linear(x, w, norm_w=None, residual=None, x_tail=None, tm=512, tn=None):
    K = x.shape[1]
    N = w.shape[1]
    tn = N if tn is None else tn
    n_head_tiles = 0
    M = x.shape[0]
    assert M % tm == 0 and N % tn == 0
    args, specs = [x], [pl.BlockSpec((tm, K), lambda i, j: (i, 0))]
    if x_tail is not None:
        n_head_tiles = M // tm
        assert x_tail.shape[0] % tm == 0
        M += x_tail.shape[0]
        specs = [pl.BlockSpec((tm, K), lambda i, j: (jnp.minimum(i, n_head_tiles - 1), 0)),
                 pl.BlockSpec((tm, K), lambda i, j: (jnp.maximum(i - n_head_tiles, 0), 0))]
        args.append(x_tail)
    has_norm, has_res = norm_w is not None, residual is not None
    if has_norm:
        args.append(norm_w.reshape(1, K))
        specs.append(pl.BlockSpec((1, K), lambda i, j: (0, 0)))
    args.append(w)
    specs.append(pl.BlockSpec((K, tn), lambda i, j: (0, j)))
    if has_res:
        args.append(residual)
        specs.append(pl.BlockSpec((tm, tn), lambda i, j: (i, j)))
    return pl.pallas_call(
        functools.partial(_linear_body, has_norm=has_norm, has_res=has_res, n_head_tiles=n_head_tiles),
        out_shape=jax.ShapeDtypeStruct((M, N), F32),
        grid=(M // tm, N // tn),
        in_specs=specs,
        out_specs=pl.BlockSpec((tm, tn), lambda i, j: (i, j)),
        scratch_shapes=[pltpu.VMEM((tm, K), BF16)],
        compiler_params=_cparams(("parallel", "arbitrary")),
        name="linear",
    )(*args)


def _shared_rows_body(x_ref, nw_ref, w_ref, rows_ref, rows_bf_ref, kvt_ref, wint_ref, *, n_head_tiles):
    acc = jnp.dot(_rms(x_ref[...], nw_ref[...]).astype(BF16), w_ref[...], preferred_element_type=F32)
    rows_ref[...] = acc
    rows_bf_ref[...] = acc.astype(BF16)

    @pl.when(pl.program_id(0) < n_head_tiles)
    def _():
        n_kv = kvt_ref.shape[1]
        for c in range(0, acc.shape[1], LANES):
            dst = kvt_ref.at[0, c:c + LANES] if c < n_kv else wint_ref.at[0, c - n_kv:c - n_kv + LANES]
            dst[...] = acc[:, c:c + LANES].T


def shared_rows(x, norm_w, w, n_seq, T, n_kv, tm=512):
    M, K = x.shape
    N = w.shape[1]
    n_head_tiles = n_seq * T // tm
    per_seq = T // tm
    assert M % tm == 0 and T % tm == 0 and n_kv % LANES == 0 and N % LANES == 0
    seq_block = lambda i: (jnp.minimum(i, n_head_tiles - 1) // per_seq, 0, jnp.minimum(i, n_head_tiles - 1) % per_seq)
    return pl.pallas_call(
        functools.partial(_shared_rows_body, n_head_tiles=n_head_tiles),
        out_shape=[jax.ShapeDtypeStruct((M, N), F32), jax.ShapeDtypeStruct((M, N), BF16),
                   jax.ShapeDtypeStruct((n_seq, n_kv, T), F32), jax.ShapeDtypeStruct((n_seq, N - n_kv, T), F32)],
        grid=(M // tm,),
        in_specs=[pl.BlockSpec((tm, K), lambda i: (i, 0)), pl.BlockSpec((1, K), lambda i: (0, 0)),
                  pl.BlockSpec((K, N), lambda i: (0, 0))],
        out_specs=[pl.BlockSpec((tm, N), lambda i: (i, 0)), pl.BlockSpec((tm, N), lambda i: (i, 0)),
                   pl.BlockSpec((1, n_kv, tm), seq_block), pl.BlockSpec((1, N - n_kv, tm), seq_block)],
        compiler_params=_cparams(("arbitrary",)),
        name="shared_rows",
    )(x, norm_w.reshape(1, K), w)


def _ffn_body(oh_ref, ot_ref, wm_ref, x_ref, nw_ref, wg_ref, wu_ref, wo_ref, o_ref, x1_ref, xn_ref, acc_ref, *,
              n_head_tiles):
    f = pl.program_id(1)

    def stage(mix_ref):
        x1 = x_ref[...] + jnp.dot(mix_ref[...].astype(BF16), wm_ref[...], preferred_element_type=F32)
        x1_ref[...] = x1
        xn_ref[...] = _rms(x1, nw_ref[...]).astype(BF16)
        acc_ref[...] = jnp.zeros_like(acc_ref)

    in_head = pl.program_id(0) < n_head_tiles
    pl.when((f == 0) & in_head)(lambda: stage(oh_ref))
    pl.when((f == 0) & jnp.logical_not(in_head))(lambda: stage(ot_ref))

    xn = xn_ref[...]
    g = jnp.dot(xn, wg_ref[...], preferred_element_type=F32)
    u = jnp.dot(xn, wu_ref[...], preferred_element_type=F32)
    a = (_silu(g) * u).astype(BF16)
    acc_ref[...] += jnp.dot(a, wo_ref[...], preferred_element_type=F32)

    @pl.when(f == pl.num_programs(1) - 1)
    def _():
        o_ref[...] = x1_ref[...] + acc_ref[...]


def _cast_body(x_ref, o_ref):
    o_ref[...] = x_ref[...].astype(o_ref.dtype)


def to_bf16(w, rows=256):
    L, R, C = w.shape
    assert R % rows == 0
    spec = pl.BlockSpec((1, rows, C), lambda l, r: (l, r, 0))
    return pl.pallas_call(
        _cast_body,
        out_shape=jax.ShapeDtypeStruct(w.shape, BF16),
        grid=(L, R // rows),
        in_specs=[spec],
        out_specs=spec,
        compiler_params=_cparams(("parallel", "parallel")),
        name="to_bf16",
    )(w)


def mix_out_ffn(o_head, o_tail, w_mix, x, norm_w, w_in, w_out, layer, tm=512, tf=256):
    M, D = x.shape
    FF = w_out.shape[1]
    nf = FF // tf
    n_head_tiles = o_head.shape[0] // tm
    assert M % tm == 0 and FF % tf == 0 and o_head.shape[0] % tm == 0 and o_head.shape[0] + o_tail.shape[0] == M
    return pl.pallas_call(
        functools.partial(_ffn_body, n_head_tiles=n_head_tiles),
        out_shape=jax.ShapeDtypeStruct((M, D), F32),
        grid=(M // tm, nf),
        in_specs=[
            pl.BlockSpec((tm, D), lambda i, f: (jnp.minimum(i, n_head_tiles - 1), 0)),
            pl.BlockSpec((tm, D), lambda i, f: (jnp.maximum(i - n_head_tiles, 0), 0)),
            pl.BlockSpec((D, D), lambda i, f: (0, 0)),
            pl.BlockSpec((tm, D), lambda i, f: (i, 0)),
            pl.BlockSpec((1, D), lambda i, f: (0, 0)),
            pl.BlockSpec((None, D, tf), lambda i, f: (layer, 0, f)),
            pl.BlockSpec((None, D, tf), lambda i, f: (layer, 0, f + nf)),
            pl.BlockSpec((None, tf, D), lambda i, f: (layer, f, 0)),
        ],
        out_specs=pl.BlockSpec((tm, D), lambda i, f: (i, 0)),
        scratch_shapes=[pltpu.VMEM((tm, D), F32), pltpu.VMEM((tm, D), BF16), pltpu.VMEM((tm, D), F32)],
        compiler_params=_cparams(("parallel", "arbitrary")),
        name="ffn",
    )(o_head, o_tail, w_mix, x, norm_w.reshape(1, D), w_in, w_in, w_out)


def _final_norm_body(x_ref, w_ref, head_ref, tail_ref, *, n_head_tiles):
    y = _rms(x_ref[...], w_ref[...])
    in_head = pl.program_id(0) < n_head_tiles

    @pl.when(in_head)
    def _():
        head_ref[...] = y

    @pl.when(jnp.logical_not(in_head))
    def _():
        tail_ref[...] = y


def final_norm(x, w, m_head, tm=512):
    M, D = x.shape
    n_head_tiles = m_head // tm
    assert m_head % tm == 0 and (M - m_head) % tm == 0 and 0 < m_head < M
    return pl.pallas_call(
        functools.partial(_final_norm_body, n_head_tiles=n_head_tiles),
        out_shape=[jax.ShapeDtypeStruct((m_head, D), F32), jax.ShapeDtypeStruct((M - m_head, D), F32)],
        grid=(M // tm,),
        in_specs=[pl.BlockSpec((tm, D), lambda i: (i, 0)), pl.BlockSpec((1, D), lambda i: (0, 0))],
        out_specs=[pl.BlockSpec((tm, D), lambda i: (jnp.minimum(i, n_head_tiles - 1), 0)),
                   pl.BlockSpec((tm, D), lambda i: (jnp.maximum(i - n_head_tiles, 0), 0))],
        compiler_params=_cparams(("arbitrary",)),
        name="final_norm",
    )(x, w.reshape(1, D))


def _l2n(x):
    return x * lax.rsqrt(jnp.sum(x * x, axis=-1, keepdims=True) + NORM_EPS)


def _gated_out(o, z, onorm):
    return _rms(o, onorm) * _silu(z)


def _dn_prompt_body(qkv_ref, z_ref, ab_ref, abT_ref, cw_ref, alr_ref, dtr_ref, alc_ref, dtc_ref, on_ref,
                    o_ref, s_out_ref, xbuf_ref, s_ref):
    n = pl.program_id(1)
    C = DN_CHUNK

    @pl.when(n == 0)
    def _():
        xbuf_ref[0:8, :] = jnp.zeros((8, DN_QKV), F32)
        s_ref[...] = jnp.zeros_like(s_ref)

    xbuf_ref[8:8 + C, :] = qkv_ref[...]
    y = xbuf_ref[5:5 + C, :] * cw_ref[0:1, :]
    for i in range(1, DN_CONV):
        y = y + xbuf_ref[5 + i:5 + i + C, :] * cw_ref[i:i + 1, :]
    xbuf_ref[0:8, :] = xbuf_ref[C:C + 8, :]
    y = _silu(y)

    ab = ab_ref[...]
    g8 = -jnp.exp(alr_ref[...]) * _softplus(ab[:, 0:DN_HEADS] + dtr_ref[...])
    beta8 = _sigmoid(ab[:, DN_HEADS:2 * DN_HEADS])
    abT = abT_ref[0, 0]
    g8T = -jnp.exp(alc_ref[...]) * _softplus(abT[0:DN_HEADS, :] + dtc_ref[...])

    ii = lax.broadcasted_iota(jnp.int32, (C, C), 0)
    jj = lax.broadcasted_iota(jnp.int32, (C, C), 1)
    incl = (ii >= jj)[None]
    strict = (ii > jj)[None]
    onorm = on_ref[...]
    H = DN_HEADS

    heads = lambda off: jnp.stack([y[:, off + h * DN_HEAD:off + (h + 1) * DN_HEAD] for h in range(H)], axis=0)
    q = _l2n(heads(0)) * (DN_HEAD ** -0.5)
    k = _l2n(heads(D_MODEL))
    v = heads(2 * D_MODEL)
    g_col = jnp.stack([g8[:, h:h + 1] for h in range(H)], axis=0)
    beta = jnp.stack([beta8[:, h:h + 1] for h in range(H)], axis=0)
    g_row = jnp.stack([g8T[h:h + 1, :] for h in range(H)], axis=0)
    G_col = jnp.sum(jnp.where(incl, g_row, 0.0), axis=2, keepdims=True)
    G_row = jnp.sum(jnp.where((ii <= jj)[None], g_col, 0.0), axis=1, keepdims=True)
    dec = jnp.where(incl, jnp.exp(jnp.where(incl, G_col - G_row, 0.0)), 0.0)
    A = jnp.where(strict, beta * dec * _bdot_nt(k, k), 0.0)
    X = -A
    Tm = X
    for _ in range(int(math.log2(C)) - 1):
        X = _bdot(X, X)
        Tm = Tm + X + _bdot(X, Tm)
    eG = jnp.exp(G_col)
    w = jnp.concatenate([beta * v, (beta * eG) * k], axis=2)
    w = w + _bdot(Tm, w)
    wv, wk = w[:, :, :DN_HEAD], w[:, :, DN_HEAD:]
    aqk = dec * _bdot_nt(q, k)
    qg = eG * q
    G_last = G_col[:, C - 1:C, :]
    kdec = jnp.exp(G_last - G_col) * k
    S = s_ref[...]
    ws = _bdot(jnp.concatenate([wk, qg], axis=1), S)
    U = wv - ws[:, :C]
    O = ws[:, C:] + _bdot(aqk, U)
    s_ref[...] = jnp.exp(G_last) * S + _bdot_tn(kdec, U)
    for h in range(H):
        sl = slice(h * DN_HEAD, (h + 1) * DN_HEAD)
        o_ref[:, sl] = _gated_out(O[h], z_ref[:, sl], onorm)

    @pl.when(n == pl.num_programs(1) - 1)
    def _():
        s_out_ref[0] = s_ref[...]


def dn_prompt(proj, n_seq, T, conv_w, a_log, dt_bias, out_norm):
    M = n_seq * T
    C = DN_CHUNK
    N = T // C
    abT = proj[:M, 4 * D_MODEL:4 * D_MODEL + 2 * DN_HEADS].reshape(n_seq, N, C, 2 * DN_HEADS).transpose(0, 1, 3, 2)
    row = lambda a: a.reshape(1, DN_HEADS)
    col = lambda a: a.reshape(DN_HEADS, 1)
    return pl.pallas_call(
        _dn_prompt_body,
        out_shape=[jax.ShapeDtypeStruct((M, D_MODEL), F32),
                   jax.ShapeDtypeStruct((n_seq, DN_HEADS, DN_HEAD, DN_HEAD), F32)],
        grid=(n_seq, N),
        in_specs=[
            pl.BlockSpec((C, DN_QKV), lambda b, n: (b * N + n, 0)),
            pl.BlockSpec((C, D_MODEL), lambda b, n: (b * N + n, 3)),
            pl.BlockSpec((C, LANES), lambda b, n: (b * N + n, 4 * D_MODEL // LANES)),
            pl.BlockSpec((1, 1, 2 * DN_HEADS, C), lambda b, n: (b, n, 0, 0)),
            pl.BlockSpec((DN_CONV, DN_QKV), lambda b, n: (0, 0)),
            pl.BlockSpec((1, DN_HEADS), lambda b, n: (0, 0)),
            pl.BlockSpec((1, DN_HEADS), lambda b, n: (0, 0)),
            pl.BlockSpec((DN_HEADS, 1), lambda b, n: (0, 0)),
            pl.BlockSpec((DN_HEADS, 1), lambda b, n: (0, 0)),
            pl.BlockSpec((1, DN_HEAD), lambda b, n: (0, 0)),
        ],
        out_specs=[pl.BlockSpec((C, D_MODEL), lambda b, n: (b * N + n, 0)),
                   pl.BlockSpec((1, DN_HEADS, DN_HEAD, DN_HEAD), lambda b, n: (b, 0, 0, 0))],
        scratch_shapes=[pltpu.VMEM((C + 8, DN_QKV), F32), pltpu.VMEM((DN_HEADS, DN_HEAD, DN_HEAD), F32)],
        compiler_params=_cparams(("parallel", "arbitrary")),
        name="dn_prompt",
    )(proj, proj, proj, abT, conv_w, row(a_log), row(dt_bias), col(a_log), col(dt_bias), out_norm.reshape(1, DN_HEAD))


def _dn_decode_body(proj_ref, cbuf_ref, s0_ref, cw_ref, alr_ref, dtr_ref, on_ref, *rest, out_layer):
    o_ref, s_out_ref, xbuf_ref, oacc_ref = rest[-4:]
    for l in range(s_out_ref.shape[0]):
        if l != out_layer:
            s_out_ref[l] = jnp.zeros(s_out_ref.shape[1:], F32)
    T = proj_ref.shape[1]
    x = proj_ref[0]
    xbuf_ref[8 - (DN_CONV - 1):8, :] = cbuf_ref[0]
    xbuf_ref[8:8 + T, :] = x[:, :DN_QKV]
    y = xbuf_ref[5:5 + T, :] * cw_ref[0:1, :]
    for i in range(1, DN_CONV):
        y = y + xbuf_ref[5 + i:5 + i + T, :] * cw_ref[i:i + 1, :]
    y = _silu(y)
    ab = x[:, 4 * D_MODEL:4 * D_MODEL + LANES]
    a8 = jnp.exp(-jnp.exp(alr_ref[...]) * _softplus(ab[:, 0:DN_HEADS] + dtr_ref[...]))
    beta8 = _sigmoid(ab[:, DN_HEADS:2 * DN_HEADS])
    ii = lax.broadcasted_iota(jnp.int32, (DN_HEAD, DN_HEAD), 0)
    jj = lax.broadcasted_iota(jnp.int32, (DN_HEAD, DN_HEAD), 1)
    eye = ii == jj

    def to_col(r):
        return jnp.sum(jnp.where(eye, r, 0.0), axis=1, keepdims=True)

    heads = [slice(h * DN_HEAD, (h + 1) * DN_HEAD) for h in range(DN_HEADS)]
    q = [_l2n(y[:, sl]) * (DN_HEAD ** -0.5) for sl in heads]
    k = [_l2n(y[:, D_MODEL + h * DN_HEAD:D_MODEL + (h + 1) * DN_HEAD]) for h in range(DN_HEADS)]
    v = [y[:, 2 * D_MODEL + h * DN_HEAD:2 * D_MODEL + (h + 1) * DN_HEAD] for h in range(DN_HEADS)]
    for t in range(T):
        for h, sl in enumerate(heads):
            S = s0_ref[0, 0, h] if t == 0 else s_out_ref[out_layer, 0, h]
            k_col = to_col(k[h][t:t + 1, :])
            q_col = to_col(q[h][t:t + 1, :])
            a = a8[t:t + 1, h:h + 1]
            b = beta8[t:t + 1, h:h + 1]
            kS = jnp.sum(k_col * S, axis=0, keepdims=True)
            S = a * S + k_col * (b * (v[h][t:t + 1, :] - a * kS))
            s_out_ref[out_layer, 0, h] = S
            oacc_ref[t:t + 1, sl] = jnp.sum(q_col * S, axis=0, keepdims=True)
    onorm = on_ref[...]
    for h in range(DN_HEADS):
        sl = slice(h * DN_HEAD, (h + 1) * DN_HEAD)
        o_ref[0, :, sl] = _gated_out(oacc_ref[0:T, sl], x[:, DN_QKV + h * DN_HEAD:DN_QKV + (h + 1) * DN_HEAD], onorm)


def dn_decode(proj, n_seq, conv_buf, S_all, layer, S_new, conv_w, a_log, dt_bias, out_norm):
    M, W = proj.shape
    T = M // n_seq
    row = lambda a: a.reshape(1, DN_HEADS)
    state_spec = pl.BlockSpec((1, 1, DN_HEADS, DN_HEAD, DN_HEAD), lambda b: (layer, b, 0, 0, 0))
    args = [proj.reshape(n_seq, T, W), conv_buf, S_all, conv_w, row(a_log), row(dt_bias), out_norm.reshape(1, DN_HEAD)]
    in_specs = [
        pl.BlockSpec((1, T, W), lambda b: (b, 0, 0)),
        pl.BlockSpec((1, DN_CONV - 1, DN_QKV), lambda b: (b, 0, 0)),
        state_spec,
        pl.BlockSpec((DN_CONV, DN_QKV), lambda b: (0, 0)),
        pl.BlockSpec((1, DN_HEADS), lambda b: (0, 0)),
        pl.BlockSpec((1, DN_HEADS), lambda b: (0, 0)),
        pl.BlockSpec((1, DN_HEAD), lambda b: (0, 0)),
    ]
    if S_new is None:
        aliases, out_layer = {}, layer
        out_state_spec = pl.BlockSpec((S_all.shape[0], 1, DN_HEADS, DN_HEAD, DN_HEAD), lambda b: (0, b, 0, 0, 0))
    else:
        aliases, out_layer = {len(args): 1}, 0
        out_state_spec = state_spec
        args.append(S_new)
        in_specs.append(pl.BlockSpec(memory_space=pl.ANY))
    o, S = pl.pallas_call(
        functools.partial(_dn_decode_body, out_layer=out_layer),
        out_shape=[jax.ShapeDtypeStruct((n_seq, T, D_MODEL), F32), jax.ShapeDtypeStruct(S_all.shape, F32)],
        grid=(n_seq,),
        in_specs=in_specs,
        out_specs=[pl.BlockSpec((1, T, D_MODEL), lambda b: (b, 0, 0)), out_state_spec],
        scratch_shapes=[pltpu.VMEM((16, DN_QKV), F32), pltpu.VMEM((8, D_MODEL), F32)],
        input_output_aliases=aliases,
        compiler_params=_cparams(("parallel",)),
        name="dn_decode",
    )(*args)
    return o.reshape(M, D_MODEL), S


def _bucket_thresholds():
    thr, prev = [], REL_MAX_EXACT
    for d in range(REL_MAX_EXACT, REL_MAX_DIST + 1):
        val = min(REL_MAX_EXACT + int(math.log(d / REL_MAX_EXACT) / math.log(REL_MAX_DIST / REL_MAX_EXACT)
                                      * (REL_BUCKETS - REL_MAX_EXACT)), REL_BUCKETS - 1)
        thr += [d] * (val - prev)
        prev = val
    assert len(thr) == REL_BUCKETS - 1 - REL_MAX_EXACT
    return tuple(thr)


_BUCKET_THR = _bucket_thresholds()
TQ = 256
BAND_ROWS = 128
TK = 512
BAND_TOP = (REL_MAX_DIST + TK + LANES - 1) // LANES * LANES
BAND_W = BAND_TOP + max(TK, WINDOW + TQ)
LOG2E = 1.4426950408889634


def _bucket(d):
    n = jnp.maximum(d, 0)
    big = jnp.full(n.shape, REL_MAX_EXACT, jnp.int32)
    for t in _BUCKET_THR:
        big = big + (n >= t).astype(jnp.int32)
    return jnp.where(n < REL_MAX_EXACT, n, big)


def _bias_lookup(bucket, table_row):
    acc = jnp.zeros(bucket.shape, F32)
    for k in range(REL_BUCKETS):
        acc = acc + jnp.where(bucket == k, table_row(k), 0.0)
    return acc


def _bias_gather(bucket, table_lanes):
    rows = bucket.shape[0]
    table = jnp.broadcast_to(table_lanes, (rows, LANES))
    tiles = [jnp.take_along_axis(table, bucket[:, c:c + LANES], axis=1) for c in range(0, bucket.shape[1], LANES)]
    return tiles[0] if len(tiles) == 1 else jnp.concatenate(tiles, axis=1)


def _head_tables(rel_bias):
    return jnp.pad(rel_bias.T, ((0, 0), (0, LANES - REL_BUCKETS)))


def _bias_cmp_body(tab_ref, o_ref, *, n_cmp):
    q0 = pl.program_id(0) * TQ
    shp = o_ref.shape[1:]
    t = q0 + lax.broadcasted_iota(jnp.int32, shp, 0)
    j = lax.broadcasted_iota(jnp.int32, shp, 1)
    d = t - (j * CMP_STRIDE + CMP_BLOCK - 1)
    dead = (d < 0) | (j >= n_cmp)
    bucket = _bucket(d)
    for h in range(NSA_HEADS):
        o_ref[h] = jnp.where(dead, NEG, _bias_gather(bucket, tab_ref[h:h + 1, :]))


def bias_cmp_prompt(rel_bias, T):
    n_sub = T // CMP_STRIDE
    return pl.pallas_call(
        functools.partial(_bias_cmp_body, n_cmp=n_sub - 1),
        out_shape=jax.ShapeDtypeStruct((NSA_HEADS, T, n_sub), F32),
        grid=(T // TQ,),
        in_specs=[pl.BlockSpec((NSA_HEADS, LANES), lambda i: (0, 0))],
        out_specs=pl.BlockSpec((NSA_HEADS, TQ, n_sub), lambda i: (0, i, 0)),
        compiler_params=_cparams(("parallel",)),
        name="bias_cmp",
    )(_head_tables(rel_bias))


def _bias_band_body(tab_ref, o_ref, ow_ref):
    h = pl.program_id(0)
    shp = o_ref.shape[1:]
    d = BAND_TOP + lax.broadcasted_iota(jnp.int32, shp, 0) - lax.broadcasted_iota(jnp.int32, shp, 1)
    bias = LOG2E * _bias_gather(_bucket(d), tab_ref[pl.ds(h, 1), :])
    o_ref[0] = jnp.where(d < 0, NEG, bias)
    ow_ref[0] = jnp.where((d < 0) | (d >= WINDOW), NEG, bias)


def bias_band(rel_bias):
    shape = jax.ShapeDtypeStruct((NSA_HEADS, BAND_ROWS, BAND_W), F32)
    spec = pl.BlockSpec((1, BAND_ROWS, BAND_W), lambda h: (h, 0, 0))
    return pl.pallas_call(
        _bias_band_body,
        out_shape=[shape, shape],
        grid=(NSA_HEADS,),
        in_specs=[pl.BlockSpec((NSA_HEADS, LANES), lambda h: (0, 0))],
        out_specs=[spec, spec],
        compiler_params=_cparams(("parallel",)),
        name="bias_band",
    )(_head_tables(rel_bias))


def _bias_decode_body(tabc_ref, cmp_ref, sel_ref, win_ref, *, past, tq):
    shp = (PAGE, NSA_HEADS * tq)
    i = lax.broadcasted_iota(jnp.int32, shp, 0)
    c = lax.broadcasted_iota(jnp.int32, shp, 1)
    qpos = past + (c & (tq - 1))
    row = lambda k: tabc_ref[k:k + 1, :]

    def table(d, dead):
        return jnp.where(dead | (d < 0), NEG, _bias_lookup(_bucket(d), row))

    ic = lax.broadcasted_iota(jnp.int32, cmp_ref.shape, 0)
    qc = past + (lax.broadcasted_iota(jnp.int32, cmp_ref.shape, 1) & (tq - 1))
    cmp_ref[...] = table(qc - (ic * CMP_STRIDE + CMP_BLOCK - 1), ic < 0)
    n_pages = past // PAGE
    for p in range(n_pages):
        sel_ref[p] = table(qpos - (p * PAGE + i), i < 0)
    sel_ref[n_pages] = table(qpos - (past + i), i >= tq)
    n_wt = WINDOW // PAGE
    for t in range(n_wt):
        d = qpos - (past - WINDOW + t * PAGE + i)
        win_ref[t] = table(d, d >= WINDOW)
    win_ref[n_wt] = table(qpos - (past + i), i >= tq)


def bias_decode(rel_bias, past, tq):
    assert tq & (tq - 1) == 0
    tabc = jnp.repeat(rel_bias, tq, axis=1)
    n_pages = past // PAGE
    nc = NSA_HEADS * tq
    return pl.pallas_call(
        functools.partial(_bias_decode_body, past=past, tq=tq),
        out_shape=[jax.ShapeDtypeStruct((past // CMP_STRIDE, nc), F32),
                   jax.ShapeDtypeStruct((n_pages + 1, PAGE, nc), F32),
                   jax.ShapeDtypeStruct((WINDOW // PAGE + 1, PAGE, nc), F32)],
        name="bias_decode",
    )(tabc)


def _cmp_weights(cmp_pos_w, w_cmp):
    w = jnp.concatenate([cmp_pos_w[0].reshape(CMP_BLOCK, -1), cmp_pos_w[1].reshape(CMP_BLOCK, -1)], axis=1)
    blocks = w_cmp.reshape(2 * NSA_KVH, NSA_HD, NSA_HD)
    n = 2 * NSA_KVH
    wbd = (jnp.eye(n, dtype=F32)[:, None, :, None] * blocks[:, :, None, :]).reshape(n * NSA_HD, n * NSA_HD)
    return w[:CMP_STRIDE], w[CMP_STRIDE:], wbd.astype(BF16)


def _pool16(x, w):
    n = x.shape[0] // CMP_STRIDE
    return jnp.sum(x.reshape(n, CMP_STRIDE, x.shape[1]) * w[None], axis=1)


def _compress_prompt_body(x_ref, wlo_ref, whi_ref, wbd_ref, o_ref, lo_ref, hi_ref):
    T = x_ref.shape[0]
    n_sub = T // CMP_STRIDE
    step = 512
    for c in range(T // step):
        xs = x_ref[c * step:(c + 1) * step, :]
        r = slice(c * step // CMP_STRIDE, (c + 1) * step // CMP_STRIDE)
        lo_ref[r, :] = _pool16(xs, wlo_ref[...])
        hi_ref[r, :] = _pool16(xs, whi_ref[...])
    hi_ref[n_sub:n_sub + 8, :] = jnp.zeros((8, hi_ref.shape[1]), F32)
    blocks = lo_ref[...] + hi_ref[1:n_sub + 1, :]
    o_ref[...] = _dot(blocks, wbd_ref[...]).astype(BF16)


def compress_prompt(rows, n_seq, T, wlo, whi, wbd):
    n_sub = T // CMP_STRIDE
    W = 2 * NSA_KVH * NSA_HD
    return pl.pallas_call(
        _compress_prompt_body,
        out_shape=jax.ShapeDtypeStruct((n_seq * n_sub, W), BF16),
        grid=(n_seq,),
        in_specs=[pl.BlockSpec((T, W), lambda b: (b, 0)),
                  pl.BlockSpec((CMP_STRIDE, W), lambda b: (0, 0)),
                  pl.BlockSpec((CMP_STRIDE, W), lambda b: (0, 0)),
                  pl.BlockSpec((W, W), lambda b: (0, 0))],
        out_specs=pl.BlockSpec((n_sub, W), lambda b: (b, 0)),
        scratch_shapes=[pltpu.VMEM((n_sub, W), F32), pltpu.VMEM((n_sub + 8, W), F32)],
        compiler_params=_cparams(("parallel",)),
        name="compress_prompt",
    )(rows, wlo, whi, wbd)


def _masked_softmax(s, mask, axis):
    l = jnp.where(mask, s, NEG)
    m = jnp.max(l, axis=axis, keepdims=True)
    e = jnp.where(mask, jnp.exp(l - m), 0.0)
    return e / jnp.maximum(jnp.sum(e, axis=axis, keepdims=True), 1e-30)


def _split3(x):
    hi = x.astype(BF16)
    r = x - hi.astype(F32)
    mid = r.astype(BF16)
    lo = (r - mid.astype(F32)).astype(BF16)
    return hi, mid, lo


def _topk_rows(score, blk, n_rows, n_pick):
    beaten = jnp.zeros(score.shape, F32)
    for other in range(n_rows):
        row = score[other:other + 1, :]
        beaten = beaten + jnp.where(blk > other, jnp.where(row >= score, 1.0, 0.0), jnp.where(row > score, 1.0, 0.0))
    return jnp.where((beaten < n_pick) & (blk < n_rows), 1.0, 0.0)


def _attn_prompt_body(q_ref, g_ref, ka_ref, vs_ref, kw_ref, vw_ref, kc_ref, vc_ref, bc_ref, band_ref, bandw_ref,
                      o_ref, qp_scr, qw_scr, qa_scr, s_scr, p_scr, m_scr, acc_scr, sc_scr, sw_scr, pw_scr):
    qb = pl.program_id(2)
    q0 = qb * TQ
    G = NSA_GROUP
    R = G * TQ
    HD = NSA_HD
    groups = [slice(g * TQ, (g + 1) * TQ) for g in range(G)]
    slabs = [(slice(g * TQ + h, g * TQ + h + BAND_ROWS), g, h) for g in range(G) for h in range(0, TQ, BAND_ROWS)]
    qblk = q_ref[...] * (HD ** -0.5)
    Q = jnp.concatenate([qblk[:, g * HD:(g + 1) * HD] for g in range(G)], axis=0)
    zeros = jnp.zeros((R, HD), F32)
    qp_scr[...] = jnp.concatenate([Q, zeros], axis=1).astype(BF16)
    qw_scr[...] = jnp.concatenate([Q * LOG2E, zeros], axis=1).astype(BF16)

    left = lax.broadcasted_iota(jnp.int32, (TQ, 2 * HD), 1) < HD

    def normalized(acc):
        tiles = []
        for g in range(0, G, 2):
            a0, a1 = acc[groups[g]], acc[groups[g + 1]]
            r0, r1 = pltpu.roll(a0, HD, 1), pltpu.roll(a1, HD, 1)
            num = jnp.where(left, a0, r1)
            den = jnp.where(left, r0, a1)
            tiles.append(num / jnp.maximum(den, 1e-30))
        return jnp.concatenate(tiles, axis=1)

    gates = _split3(_sigmoid(g_ref[0]))
    W = G * HD
    gi = lax.broadcasted_iota(jnp.int32, (3 * G, W), 0)
    gc = lax.broadcasted_iota(jnp.int32, (3 * G, W), 1) // HD

    def gate(branch):
        spread = (gi == 3 * gc + branch).astype(BF16)
        return sum(jnp.dot(part, spread, preferred_element_type=F32) for part in gates)

    WK = WINDOW + TQ
    kw0 = pl.multiple_of(jnp.maximum(q0 - WINDOW, 0), LANES)
    sw_scr[...] = _dot_nt(qw_scr[...], kw_ref[pl.ds(kw0, WK), :])
    for r, g, h in slabs:
        cw0 = pl.multiple_of(BAND_TOP - (q0 + h - kw0), LANES)
        s = sw_scr[r, :] + bandw_ref[g, :, pl.ds(cw0, WK)]
        m = jnp.maximum(jnp.max(s, axis=1, keepdims=True), 0.5 * NEG)
        pw_scr[r, :] = jnp.exp2(s - m).astype(BF16)
    o_ref[...] = normalized(jnp.dot(pw_scr[...], vw_ref[pl.ds(kw0, WK), :], preferred_element_type=F32)) * gate(2)

    n_sub = bc_ref.shape[2]
    n_blk = n_sub * CMP_STRIDE // SEL_BLOCK
    n_pick = min(N_SEL, n_blk)
    per = SEL_BLOCK // CMP_STRIDE
    bb = lax.broadcasted_iota(jnp.int32, (n_blk, n_sub), 0)
    mm = lax.broadcasted_iota(jnp.int32, (n_blk, n_sub), 1)
    pool = ((mm // per == bb).astype(F32) + ((mm + 1) // per == bb).astype(F32)).astype(BF16)
    s_scr[:, :n_sub] = _dot_nt(qp_scr[...], kc_ref[...])
    for h in range(0, TQ, BAND_ROWS):
        imp = None
        for g in range(G):
            r = slice(g * TQ + h, g * TQ + h + BAND_ROWS)
            s = s_scr[r, :n_sub] + bc_ref[g, h:h + BAND_ROWS, :]
            e = jnp.exp(s - jnp.maximum(jnp.max(s, axis=1, keepdims=True), 0.5 * NEG))
            p_scr[r, :n_sub] = e.astype(BF16)
            p = e / jnp.maximum(jnp.sum(e, axis=1, keepdims=True), 1e-30)
            imp = p if imp is None else imp + p
        sc_scr[:, h:h + BAND_ROWS] = sum(_dot_nt(pool, part) for part in _split3(imp))
    o_ref[...] += normalized(jnp.dot(p_scr[:, :n_sub], vc_ref[...], preferred_element_type=F32)) * gate(0)

    p_slc = sc_scr[...]
    blk = lax.broadcasted_iota(jnp.int32, (n_blk, TQ), 0)
    t = q0 + lax.broadcasted_iota(jnp.int32, (n_blk, TQ), 1)
    cur = t // SEL_BLOCK
    forced = (blk == 0) | (blk == cur) | (blk == cur - 1)
    score = jnp.where(forced, 1e4, jnp.where(blk * SEL_BLOCK <= t, p_slc, -1.0))
    sc_scr[...] = score
    beaten = jnp.zeros((n_blk, TQ), F32)
    for other in range(n_blk):
        row = sc_scr[other:other + 1, :]
        ge = jnp.where(row >= score, 1.0, 0.0)
        gt = jnp.where(row > score, 1.0, 0.0)
        beaten = beaten + jnp.where(blk > other, ge, gt)
    selneg = jnp.where(beaten < n_pick, 0.0, NEG)
    selneg_q = selneg.T
    if n_blk < HD:
        selneg_q = jnp.concatenate([selneg_q, jnp.zeros((TQ, HD - n_blk), F32)], axis=1)
    qa_scr[...] = jnp.concatenate([jnp.concatenate([Q[r] * LOG2E, selneg_q], axis=1) for r in groups],
                                  axis=0).astype(BF16)

    def soften(kt, pv):
        rowmax = []
        for r, g, h in slabs:
            c0 = pl.multiple_of(jnp.maximum(BAND_TOP - (q0 + h - kt * TK), 0), LANES)
            s = s_scr[r, :] + band_ref[g, :, pl.ds(c0, TK)]
            s_scr[r, :] = s
            rowmax.append(jnp.max(s, axis=1, keepdims=True))
        for i, (r, g, h) in enumerate(slabs):
            m_old = m_scr[r, :]
            m_new = jnp.maximum(m_old, rowmax[i])
            m_scr[r, :] = m_new
            a = acc_scr[r, :] if pv is None else acc_scr[r, :] + pv[r]
            acc_scr[r, :] = jnp.exp2(m_old - m_new) * a
            for c in range(TK // LANES):
                cs = slice(c * LANES, (c + 1) * LANES)
                p_scr[r, cs] = jnp.exp2(s_scr[r, cs] - m_new).astype(BF16)

    def sel_qk(kt):
        s_scr[...] = _dot_nt(qa_scr[...], ka_ref[pl.ds(pl.multiple_of(kt * TK, TK), TK), :])

    def sel_pv(kt):
        v = vs_ref[pl.ds(pl.multiple_of(kt * TK, TK), TK), :]
        return jnp.dot(p_scr[...], v, preferred_element_type=F32)

    m_scr[...] = jnp.full(m_scr.shape, 0.5 * NEG, F32)
    acc_scr[...] = jnp.zeros(acc_scr.shape, F32)
    sel_qk(0)
    soften(0, None)

    def step(kt, carry):
        pv = sel_pv(kt - 1)
        sel_qk(kt)
        soften(kt, pv)
        return carry

    n_kt = (q0 + TQ + TK - 1) // TK
    lax.fori_loop(1, n_kt, step, 0)
    o_ref[...] += normalized(acc_scr[...] + sel_pv(n_kt - 1)) * gate(1)


def attn_prompt_operands(rows_bf, cmp_p, n_seq, T):
    M = n_seq * T
    W = NSA_KVH * NSA_HD

    def per_head(x, aux):
        x = x.reshape(x.shape[0], NSA_KVH, NSA_HD)
        out = jnp.concatenate([x, jnp.broadcast_to(aux[:, None, :], x.shape)], axis=2)
        return out.reshape(x.shape[0], NSA_KVH * 2 * NSA_HD)

    blk_of_key = (jnp.arange(M, dtype=jnp.int32) % T) // SEL_BLOCK
    onehot = (blk_of_key[:, None] == jnp.arange(NSA_HD, dtype=jnp.int32)[None, :]).astype(BF16)
    zeros, ones = jnp.zeros((M, NSA_HD), BF16), jnp.ones((M, NSA_HD), BF16)
    kind = lambda i: rows_bf[:M, i * W:(i + 1) * W]
    nc = cmp_p.shape[0]
    return (per_head(kind(2), onehot), per_head(kind(3), ones), per_head(kind(4), zeros), per_head(kind(5), ones),
            per_head(cmp_p[:, :W], zeros[:nc]), per_head(cmp_p[:, W:], ones[:nc]))


def attn_prompt(proj, n_seq, T, operands, bias_c, bands):
    M = n_seq * T
    NQ = T // TQ
    n_sub = T // CMP_STRIDE
    n_blk = T // SEL_BLOCK
    assert n_blk <= NSA_HD and T % TK == 0 and T >= WINDOW + TQ and n_sub <= TK
    gl = proj[:M, NSA_HEADS * NSA_HD:NSA_HEADS * NSA_HD + 3 * NSA_HEADS]
    gl = gl.reshape(M, NSA_KVH, 3 * NSA_GROUP).transpose(1, 0, 2)
    pair = 2 * NSA_HD
    ks_a, vs_a, kw_a, vw_a, kc_a, vc_a = operands
    kv_spec = pl.BlockSpec((T, pair), lambda k, b, i: (b, k))
    cmp_spec = pl.BlockSpec((n_sub, pair), lambda k, b, i: (b, k))
    band_spec = pl.BlockSpec((NSA_GROUP, BAND_ROWS, BAND_W), lambda k, b, i: (k, 0, 0))
    WO = NSA_GROUP * NSA_HD
    R = NSA_GROUP * TQ
    WK = WINDOW + TQ
    band, bandw = bands
    return pl.pallas_call(
        _attn_prompt_body,
        out_shape=jax.ShapeDtypeStruct((M, NSA_HEADS * NSA_HD), F32),
        grid=(NSA_KVH, n_seq, NQ),
        in_specs=[
            pl.BlockSpec((TQ, WO), lambda k, b, i: (b * NQ + i, k)),
            pl.BlockSpec((1, TQ, 3 * NSA_GROUP), lambda k, b, i: (k, b * NQ + i, 0)),
            kv_spec, kv_spec, kv_spec, kv_spec,
            cmp_spec, cmp_spec,
            pl.BlockSpec((NSA_GROUP, TQ, n_sub), lambda k, b, i: (k, i, 0)),
            band_spec, band_spec,
        ],
        out_specs=pl.BlockSpec((TQ, WO), lambda k, b, i: (b * NQ + i, k)),
        scratch_shapes=[pltpu.VMEM((R, pair), BF16), pltpu.VMEM((R, pair), BF16), pltpu.VMEM((R, pair), BF16),
                        pltpu.VMEM((R, TK), F32), pltpu.VMEM((R, TK), BF16),
                        pltpu.VMEM((R, LANES), F32), pltpu.VMEM((R, pair), F32),
                        pltpu.VMEM((n_blk, TQ), F32),
                        pltpu.VMEM((R, WK), F32), pltpu.VMEM((R, WK), BF16)],
        compiler_params=_cparams(("parallel", "parallel", "arbitrary")),
        name="attn_prompt",
    )(proj, gl, ks_a, vs_a, kw_a, vw_a, kc_a, vc_a, bias_c, band, bandw)


def _compress_decode_body(pt_ref, *refs, n_pages):
    pages = refs[:n_pages]
    new_ref, wlo_ref, whi_ref, wbd_ref, o_ref = refs[n_pages:]
    per = PAGE // CMP_STRIDE
    n_sub = n_pages * per
    W = wlo_ref.shape[0]
    pos = lax.broadcasted_iota(jnp.int32, (2 * PAGE, n_sub), 0)
    blk = lax.broadcasted_iota(jnp.int32, (2 * PAGE, n_sub), 1)

    def pooled(x, p):
        sub = p * per + (pos % PAGE) // CMP_STRIDE
        place = (blk == jnp.where(pos < PAGE, sub, sub - 1)).astype(BF16)
        y = jnp.concatenate([x * wlo_ref[...], x * whi_ref[...]], axis=1).astype(BF16)
        return jnp.dot(y, place, preferred_element_type=F32)

    blocks = pooled(new_ref[0], n_pages)
    for p in range(n_pages):
        blocks = blocks + pooled(pages[p][0].reshape(W, PAGE), p)
    o_ref[0] = jnp.dot(wbd_ref[...], blocks.astype(BF16), preferred_element_type=F32).astype(BF16)


def compress_decode(cache_t, page_table, new_rows, wlo, whi, wbd):
    nb, n_pages = page_table.shape
    W = 2 * NSA_KVH * NSA_HD
    tq = new_rows.shape[1]
    assert tq <= CMP_STRIDE
    n_sub = n_pages * PAGE // CMP_STRIDE
    wlo_t = jnp.tile(wlo.T, (1, PAGE // CMP_STRIDE))
    whi_t = jnp.tile(whi.T, (1, PAGE // CMP_STRIDE))
    new_t = jnp.pad(new_rows[:, :, :W].transpose(0, 2, 1), ((0, 0), (0, 0), (0, PAGE - tq)))
    page_spec = lambda p: pl.BlockSpec((1, 2, W // 2, PAGE), lambda b, pt: (pt[b, p], 0, 0, 0))
    const = lambda shape: pl.BlockSpec(shape, lambda b, pt: (0,) * len(shape))
    return pl.pallas_call(
        functools.partial(_compress_decode_body, n_pages=n_pages),
        out_shape=jax.ShapeDtypeStruct((nb, W, n_sub), BF16),
        grid_spec=pltpu.PrefetchScalarGridSpec(
            num_scalar_prefetch=1,
            grid=(nb,),
            in_specs=[page_spec(p) for p in range(n_pages)] + [
                pl.BlockSpec((1, W, PAGE), lambda b, pt: (b, 0, 0)),
                const((W, PAGE)), const((W, PAGE)), const((W, W))],
            out_specs=pl.BlockSpec((1, W, n_sub), lambda b, pt: (b, 0, 0)),
        ),
        compiler_params=_cparams(("parallel",)),
        name="compress_decode",
    )(page_table, *([cache_t] * n_pages), new_t, wlo_t, whi_t, wbd.T)


def _attn_decode_body(pt_ref, *refs, n_pages, tq):
    pages = refs[:n_pages]
    q_ref, qt_ref, g_ref, new_ref, win_ref, cmp_ref, bcmp_ref, bsel_ref, bwin_ref, o_ref = refs[n_pages:]
    W = NSA_KVH * NSA_HD
    NC = NSA_HEADS * tq
    Qbd = q_ref[0]
    QbdT = qt_ref[0]
    past = n_pages * PAGE
    tn = (((0,), (0,)), ((), ()))

    def logits_t(kT):
        return lax.dot_general(kT.astype(BF16), QbdT, tn, preferred_element_type=F32)

    cm = cmp_ref[0]
    bc = bcmp_ref[...]
    p_c = _masked_softmax(logits_t(cm[:W]) + bc, bc > 0.5 * NEG, 0)
    o_cmp = jnp.dot(cm[W:], p_c.astype(BF16), preferred_element_type=F32)

    n_sub = cm.shape[1]
    per = SEL_BLOCK // CMP_STRIDE
    n_blk = past // SEL_BLOCK + 1
    nb_pad = (n_blk + 7) // 8 * 8
    ci = lax.broadcasted_iota(jnp.int32, (NC, NC), 0)
    cj = lax.broadcasted_iota(jnp.int32, (NC, NC), 1)
    gq = NSA_GROUP * tq
    same = ((ci // gq == cj // gq) & ((ci & (tq - 1)) == (cj & (tq - 1)))).astype(BF16)
    imp = sum(jnp.dot(part, same, preferred_element_type=F32) for part in _split3(p_c))
    bb = lax.broadcasted_iota(jnp.int32, (nb_pad, n_sub), 0)
    mm = lax.broadcasted_iota(jnp.int32, (nb_pad, n_sub), 1)
    pool = ((mm // per == bb).astype(F32) + ((mm + 1) // per == bb).astype(F32)).astype(BF16)
    p_slc = sum(jnp.dot(pool, part, preferred_element_type=F32) for part in _split3(imp))
    blk = lax.broadcasted_iota(jnp.int32, (nb_pad, NC), 0)
    qpos = past + (lax.broadcasted_iota(jnp.int32, (nb_pad, NC), 1) & (tq - 1))
    cur = qpos // SEL_BLOCK
    forced = (blk == 0) | (blk == cur) | (blk == cur - 1)
    score = jnp.where(forced, 1e4, jnp.where(blk * SEL_BLOCK <= qpos, p_slc, -1.0))
    score = jnp.where(blk < n_blk, score, -3.0)
    sel = _topk_rows(score, blk, n_blk, min(N_SEL, n_blk))

    def attend(tiles):
        logits = []
        for k, _, bias, mask, stored in tiles:
            b = bias()
            logits.append(jnp.where(mask(b), (logits_t(k()) if stored else _dot_nt(k(), Qbd)) + b, NEG))
        m = functools.reduce(jnp.maximum, [jnp.max(s, axis=0, keepdims=True) for s in logits])
        den = jnp.zeros((1, NC), F32)
        acc = jnp.zeros((W, NC), F32)
        for s, (_, v, _, _, stored) in zip(logits, tiles):
            p = jnp.where(s > 0.5 * NEG, jnp.exp(s - m), 0.0)
            den = den + jnp.sum(p, axis=0, keepdims=True)
            if stored:
                acc = acc + jnp.dot(v().astype(BF16), p.astype(BF16), preferred_element_type=F32)
            else:
                acc = acc + lax.dot_general(v().astype(BF16), p.astype(BF16), tn, preferred_element_type=F32)
        return acc / jnp.maximum(den, 1e-30)

    new = new_ref[0]
    pad = jnp.zeros((8 - tq, W), F32)
    new_tile = lambda kind: (lambda: jnp.concatenate([new[:, kind * W:(kind + 1) * W], pad], axis=0))

    half = lax.broadcasted_iota(jnp.int32, (PAGE, NC), 0) < SEL_BLOCK
    def page_mask(p):
        b0 = p * (PAGE // SEL_BLOCK)
        return lambda bias: (jnp.where(half, sel[b0:b0 + 1, :], sel[b0 + 1:b0 + 2, :]) > 0.5) & (bias > 0.5 * NEG)

    tiles = [(lambda p=p: pages[p][0, 0], lambda p=p: pages[p][0, 1], lambda p=p: bsel_ref[p], page_mask(p), True)
             for p in range(n_pages)]
    tiles.append((new_tile(2), new_tile(3), lambda: bsel_ref[n_pages][0:8],
                  lambda bias: (sel[n_blk - 1:n_blk, :] > 0.5) & (bias > 0.5 * NEG), False))
    o_sel = attend(tiles)

    causal = lambda bias: bias > 0.5 * NEG
    tiles = []
    for t in range(WINDOW // PAGE):
        ts = slice(t * PAGE, (t + 1) * PAGE)
        tiles.append((lambda ts=ts: win_ref[0, 0, :, ts], lambda ts=ts: win_ref[0, 1, :, ts],
                      lambda t=t: bwin_ref[t], causal, True))
    tiles.append((new_tile(4), new_tile(5), lambda: bwin_ref[WINDOW // PAGE][0:8], causal, False))
    o_win = attend(tiles)

    gt = _sigmoid(g_ref[0])
    o = o_cmp * gt[0:1] + o_sel * gt[1:2] + o_win * gt[2:3]
    kvh_of_col = lax.broadcasted_iota(jnp.int32, (NSA_HD, NC), 1) // gq
    out = jnp.zeros((NSA_HD, NC), F32)
    for k in range(NSA_KVH):
        out = out + jnp.where(kvh_of_col == k, o[k * NSA_HD:(k + 1) * NSA_HD, :], 0.0)
    o_ref[0] = out


def attn_decode(proj, cache_t, page_table, new_rows, win_t, cmp_d, bias_tabs):
    nb, n_pages = page_table.shape
    tq = new_rows.shape[1]
    W = NSA_KVH * NSA_HD
    NC = NSA_HEADS * tq
    q = proj[:, :NSA_HEADS * NSA_HD] * (NSA_HD ** -0.5)
    q = q.reshape(nb, tq, NSA_KVH, NSA_GROUP, NSA_HD).transpose(0, 2, 3, 1, 4)
    qbd = q[:, :, :, :, None, :] * jnp.eye(NSA_KVH, dtype=F32)[None, :, None, None, :, None]
    qbd = qbd.reshape(nb, NC, W).astype(BF16)
    qbd_t = qbd.transpose(0, 2, 1)
    gl = proj[:, NSA_HEADS * NSA_HD:NSA_HEADS * NSA_HD + 3 * NSA_HEADS]
    gl = gl.reshape(nb, tq, NSA_HEADS, 3).transpose(0, 3, 2, 1).reshape(nb, 3, NC)
    bcmp, bsel, bwin = bias_tabs
    page_spec = lambda p: pl.BlockSpec((1, 2, W, PAGE), lambda b, pt: (pt[b, p], 1, 0, 0))
    const = lambda shape: pl.BlockSpec(shape, lambda b, pt: (0,) * len(shape))
    per_b = lambda shape: pl.BlockSpec((1,) + shape, lambda b, pt: (b,) + (0,) * len(shape))
    o = pl.pallas_call(
        functools.partial(_attn_decode_body, n_pages=n_pages, tq=tq),
        out_shape=jax.ShapeDtypeStruct((nb, NSA_HD, NC), F32),
        grid_spec=pltpu.PrefetchScalarGridSpec(
            num_scalar_prefetch=1,
            grid=(nb,),
            in_specs=[page_spec(p) for p in range(n_pages)] + [
                per_b((NC, W)), per_b((W, NC)), per_b((3, NC)), per_b((tq, 6 * W)), per_b((2, W, WINDOW)),
                per_b((2 * W, cmp_d.shape[2])),
                const(bcmp.shape), const(bsel.shape), const(bwin.shape)],
            out_specs=per_b((NSA_HD, NC)),
        ),
        compiler_params=_cparams(("parallel",)),
        name="attn_decode",
    )(page_table, *([cache_t] * n_pages), qbd, qbd_t, gl, new_rows, win_t, cmp_d, bcmp, bsel, bwin)
    return o.reshape(nb, NSA_HD, NSA_HEADS, tq).transpose(0, 3, 2, 1).reshape(nb * tq, NSA_HEADS * NSA_HD)


def _pad_cols(w, n):
    return jnp.pad(w, ((0, 0), (0, n - w.shape[1])))


def kernel(x_prompt, x_sample, state_dn_S, state_dn_conv, cache_kv, state_win_kv, page_table, norm_mix, norm_ffn, norm_kv, norm_final, ffn_w_in, ffn_w_out, dn_w_in, dn_conv_w, dn_A_log, dn_dt_bias, dn_out_norm, dn_w_out, nsa_w_kv, nsa_cmp_pos_w, nsa_w_cmp, nsa_w_in, nsa_w_out, rel_bias):
    B, T, D = x_prompt.shape
    NB, TS, _ = x_sample.shape
    Mp, Ms = B * T, NB * TS
    past = page_table.shape[1] * PAGE
    x = jnp.concatenate([x_prompt.reshape(Mp, D), x_sample.reshape(Ms, D)], axis=0)

    ffn_in_bf, ffn_out_bf = to_bf16(ffn_w_in), to_bf16(ffn_w_out)
    p_S, p_conv, s_conv = [], [], []
    s_S = None
    for l in range(N_A_LAYERS):
        w_in = _pad_cols(dn_w_in[l], 4 * D + LANES).astype(BF16)
        proj = linear(x, w_in, norm_w=norm_mix[l], tn=(4 * D + LANES) // 3)
        o_p, S_p = dn_prompt(proj, B, T, dn_conv_w[l], dn_A_log[l], dn_dt_bias[l], dn_out_norm[l])
        o_s, s_S = dn_decode(proj[Mp:], NB, state_dn_conv[l], state_dn_S, l, s_S, dn_conv_w[l], dn_A_log[l],
                             dn_dt_bias[l], dn_out_norm[l])
        tail = DN_CONV - 1
        qkv_s = proj[Mp:, :DN_QKV].reshape(NB, TS, DN_QKV)
        p_S.append(S_p)
        p_conv.append(jnp.stack([proj[(b + 1) * T - tail:(b + 1) * T, :DN_QKV] for b in range(B)]))
        s_conv.append(jnp.concatenate([state_dn_conv[l], qkv_s], axis=1)[:, TS:])
        x = mix_out_ffn(o_p, o_s, dn_w_out[l].astype(BF16), x, norm_ffn[l], ffn_in_bf, ffn_out_bf, l)

    W = NSA_KVH * NSA_HD
    rows, rows_bf, kv_t, win_t_p = shared_rows(x, norm_kv, nsa_w_kv.astype(BF16), B, T, 4 * W)
    rows_s = rows[Mp:].reshape(NB, TS, 6, NSA_KVH, NSA_HD)
    p_kv_rows = kv_t.reshape(B, 4, NSA_KVH, NSA_HD, T).transpose(0, 4, 1, 2, 3)
    p_win_kv = win_t_p.reshape(B, 2, NSA_KVH, NSA_HD, T).transpose(0, 4, 1, 2, 3)[:, T - min(WINDOW, T):]
    s_kv_rows = rows_s[:, :, :4]
    s_win_kv = jnp.concatenate([state_win_kv, rows_s[:, :, 4:]], axis=1)[:, TS:]

    wlo, whi, wbd = _cmp_weights(nsa_cmp_pos_w, nsa_w_cmp)
    cmp_p = compress_prompt(rows, B, T, wlo, whi, wbd)
    attn_ops = attn_prompt_operands(rows_bf, cmp_p, B, T)
    new_rows = rows[Mp:].reshape(NB, TS, 6 * W)
    cache_t = cache_kv.transpose(0, 2, 3, 4, 1).reshape(cache_kv.shape[0], 4, W, PAGE)
    win_t = state_win_kv.transpose(0, 2, 3, 4, 1).reshape(NB, 2, W, state_win_kv.shape[1])
    cmp_d = compress_decode(cache_t, page_table, new_rows, wlo, whi, wbd)
    bias_c = bias_cmp_prompt(rel_bias, T)
    bands = bias_band(rel_bias)
    bias_d = bias_decode(rel_bias, past, TS)

    for j in range(N_B_LAYERS):
        l = N_A_LAYERS + j
        w_in = _pad_cols(nsa_w_in[j], D + LANES).astype(BF16)
        proj = linear(x, w_in, norm_w=norm_mix[l], tn=D + LANES)
        o_p = attn_prompt(proj, B, T, attn_ops, bias_c, bands)
        o_s = attn_decode(proj[Mp:], cache_t, page_table, new_rows, win_t, cmp_d, bias_d)
        x = mix_out_ffn(o_p, o_s, nsa_w_out[j].astype(BF16), x, norm_ffn[l], ffn_in_bf, ffn_out_bf, l)

    y_p, y_s = final_norm(x, norm_final, Mp)
    return (y_p.reshape(B, T, D), y_s.reshape(NB, TS, D),
            jnp.stack(p_S), jnp.stack(p_conv), p_kv_rows, p_win_kv,
            s_S, jnp.stack(s_conv), s_kv_rows, s_win_kv)
```
